```python
import math, functools
import jax, jax.numpy as jnp
from jax import lax
import numpy as np

D_MODEL = 1024
BATCH = 4
SEQ = 4096
DEPTH = 2
DEC_BATCH = 128
DEC_SEQ = 4
PAST_LEN = 8192
PAGE_SIZE = 128

EPS = 1e-6
F32 = jnp.float32
RW_HEADS = 8
RW_HD = 64
RW_W = RW_HEADS * RW_HD
RW_DECAY_LORA = 64
RW_AAA_LORA = 64
RW_GATE_LORA = 128
RW_COLS = 3 * RW_W + RW_DECAY_LORA + RW_AAA_LORA + RW_GATE_LORA
RW_SPLITS = (RW_W, 2 * RW_W, 3 * RW_W, 3 * RW_W + RW_DECAY_LORA, 3 * RW_W + RW_DECAY_LORA + RW_AAA_LORA)
RW_GN_EPS = 64e-5
MLA_HEADS = 8
Q_LORA = 256
KV_LORA = 256
NOPE_DIM = 64
ROPE_DIM = 32
V_DIM = 64
QK_DIM = NOPE_DIM + ROPE_DIM
MLA_COLS = Q_LORA + KV_LORA + ROPE_DIM
ROPE_BASE = 10000.0
ATTN_SCALE = QK_DIM ** -0.5
Q_BLOCK = 128
SSM_HEADS = 8
SSM_HD = 64
SSM_W = SSM_HEADS * SSM_HD
SSM_GROUPS = 2
SSM_EPG = SSM_HEADS // SSM_GROUPS
SSM_STATE = 64
CONV_W = 4
CONV_CH = SSM_W + 2 * SSM_GROUPS * SSM_STATE
SSM_COLS = SSM_W + CONV_CH + SSM_HEADS
SSD_CHUNK = 128
N_BRANCH = 3
BR_W = RW_W
IN_COLS = RW_COLS + MLA_COLS + SSM_COLS + N_BRANCH * D_MODEL
IN_SPLITS = (RW_COLS, RW_COLS + MLA_COLS, RW_COLS + MLA_COLS + SSM_COLS)
PK_HEADS = 8
PK_DKEY = 256
PK_DHALF = PK_DKEY // 2
PK_NKEYS = 128
PK_TOPK = 16
PK_EXPERTS = PK_NKEYS * PK_NKEYS
PK_TOK_BLOCK = 256

kernel_name = 'hybrid_rwkv7_mla_ssd_peer_decoder_step'


def rms_norm(x, g):
    xf = x.astype(F32)
    y = xf * lax.rsqrt(jnp.mean(xf * xf, axis=-1, keepdims=True) + EPS)
    return (y * g.astype(F32)).astype(x.dtype)


def rope(x, pos):
    half = ROPE_DIM // 2
    inv = jnp.power(ROPE_BASE, -jnp.arange(half, dtype=F32) / half)
    ang = pos[..., None].astype(F32) * inv
    cos, sin = jnp.cos(ang), jnp.sin(ang)
    x1, x2 = x[..., :half].astype(F32), x[..., half:].astype(F32)
    return jnp.concatenate([x1 * cos - x2 * sin, x1 * sin + x2 * cos], axis=-1).astype(x.dtype)


def rwkv7_scan(r, w, k, v, a, b, s0):
    def step(s, inp):
        r_t, w_t, k_t, v_t, a_t, b_t = inp
        sa = jnp.einsum('bhvk,bhk->bhv', s, a_t)
        s = s * w_t[:, :, None, :] + sa[..., None] * b_t[:, :, None, :] + v_t[..., None] * k_t[:, :, None, :]
        return s, jnp.einsum('bhvk,bhk->bhv', s, r_t)
    xs = tuple(jnp.moveaxis(t, 1, 0) for t in (r, w, k, v, a, b))
    s, ys = lax.scan(step, s0, xs)
    return jnp.moveaxis(ys, 0, 1), s


def rwkv7_branch(cols, prev_row, s0, lp):
    Bsz, T, _ = cols.shape
    odt = cols.dtype
    xprev = jnp.concatenate([prev_row[:, None, :].astype(odt), cols[:, :-1]], axis=1)
    m = cols + (xprev - cols) * lp['rw_mu']
    r, k, v, wl, al, gl = jnp.split(m, RW_SPLITS, axis=-1)
    logw = -jax.nn.softplus(-(lp['rw_w0'] + jnp.tanh(wl) @ lp['rw_w2']).astype(F32)) - 0.5
    decay = jnp.exp(-jnp.exp(logw))
    a = jax.nn.sigmoid((lp['rw_a0'] + al @ lp['rw_a2']).astype(F32))
    g = (jax.nn.sigmoid(gl) @ lp['rw_g2']).astype(F32)
    r, k, v = r.astype(F32), k.astype(F32), v.astype(F32)
    kk = k * lp['rw_kk'].astype(F32)
    k = k * (1.0 + (a - 1.0) * lp['rw_ka'].astype(F32))
    r, k, v, a, decay, kk = tuple(t.reshape(Bsz, T, RW_HEADS, RW_HD) for t in (r, k, v, a, decay, kk))
    kk = kk / jnp.maximum(jnp.sqrt(jnp.sum(kk * kk, axis=-1, keepdims=True)), 1e-12)
    y, s = rwkv7_scan(r, decay, k, v, -kk, kk * a, s0.astype(F32))
    mu = jnp.mean(y, axis=-1, keepdims=True)
    var = jnp.mean(jnp.square(y - mu), axis=-1, keepdims=True)
    y = ((y - mu) * lax.rsqrt(var + RW_GN_EPS)).reshape(Bsz, T, RW_W)
    y = y * lp['rw_ln_w'].astype(F32) + lp['rw_ln_b'].astype(F32)
    bonus = jnp.sum(r * k * lp['rw_rk'].astype(F32), axis=-1, keepdims=True) * v
    y = y + bonus.reshape(Bsz, T, RW_W)
    return (y * g).astype(odt), cols[:, -1], s.astype(s0.dtype)


def mla_project(cols, pos, lp):
    Bsz, T, _ = cols.shape
    qa, ckv, kpe = jnp.split(cols, (Q_LORA, Q_LORA + KV_LORA), axis=-1)
    qa = rms_norm(qa, lp['mla_qa_g'])
    q = (qa @ lp['mla_wqb']).reshape(Bsz, T, MLA_HEADS, QK_DIM)
    q = rms_norm(q, lp['mla_q_g'])
    q_nope = q[..., :NOPE_DIM]
    q_rope = rope(q[..., NOPE_DIM:], pos[:, None])
    ckv = rms_norm(ckv, lp['mla_kv_g'])
    k_nope = jnp.einsum('btr,rhd->bthd', ckv, lp['mla_wkb'])
    kf, pf = k_nope.astype(F32), kpe.astype(F32)
    ms = (jnp.sum(kf * kf, axis=-1) + jnp.sum(pf * pf, axis=-1, keepdims=True)) / QK_DIM
    ksc = lax.rsqrt(ms + EPS).astype(cols.dtype)
    return q_nope, q_rope, k_nope, ckv, kpe, ksc


def blocked_causal_attention(q, k, v):
    Bsz, S, H, Dk = q.shape
    nb = S // Q_BLOCK
    qb = jnp.moveaxis(q.reshape(Bsz, nb, Q_BLOCK, H, Dk), 1, 0)
    kpos = jnp.arange(S)
    def block(args):
        i, qi = args
        s = jnp.einsum('bqhd,bkhd->bhqk', qi, k).astype(F32) * ATTN_SCALE
        qpos = i * Q_BLOCK + jnp.arange(Q_BLOCK)
        s = jnp.where(kpos[None, :] <= qpos[:, None], s, -jnp.inf)
        p = jax.nn.softmax(s, axis=-1).astype(v.dtype)
        return jnp.einsum('bhqk,bkhd->bqhd', p, v)
    o = lax.map(block, (jnp.arange(nb), qb))
    return jnp.moveaxis(o, 0, 1).reshape(Bsz, S, H, v.shape[-1])


def mla_prompt_attend(q_nope, q_rope, k_nope, ckv, kpe, ksc, lp, pos):
    gk = lp['mla_k_g']
    kr = rope(kpe * gk[NOPE_DIM:], pos)
    k = jnp.concatenate([k_nope * gk[:NOPE_DIM], jnp.broadcast_to(kr[:, :, None, :], q_rope.shape)], axis=-1)
    k = k * ksc[..., None]
    q = jnp.concatenate([q_nope, q_rope], axis=-1)
    v = jnp.einsum('btr,rhd->bthd', ckv, lp['mla_wvb'])
    return blocked_causal_attention(q, k, v)


def mla_sample_attend(q_nope, q_rope, k_nope, ckv, kpe, ksc, lp, pos, past_ckv, past_kpe, past_ksc):
    gk = lp['mla_k_g']
    T = q_nope.shape[1]
    L = past_ckv.shape[1]
    q_lat = jnp.einsum('bthd,rhd->bthr', q_nope * gk[:NOPE_DIM], lp['mla_wkb'])
    def scores(c_rows, kpe_rows, ksc_rows, kpos):
        kr = rope(kpe_rows * gk[NOPE_DIM:], kpos)
        s = jnp.einsum('bthr,blr->bhtl', q_lat, c_rows) + jnp.einsum('bthd,bld->bhtl', q_rope, kr)
        return s.astype(F32) * jnp.swapaxes(ksc_rows, 1, 2)[:, :, None, :].astype(F32) * ATTN_SCALE
    s_past = scores(past_ckv, past_kpe, past_ksc, jnp.arange(L))
    s_new = jnp.where(jnp.tril(jnp.ones((T, T), bool)), scores(ckv, kpe, ksc, pos), -jnp.inf)
    p = jax.nn.softmax(jnp.concatenate([s_past, s_new], axis=-1), axis=-1).astype(ckv.dtype)
    o_lat = jnp.einsum('bhtl,blr->bthr', p[..., :L], past_ckv) + jnp.einsum('bhtl,blr->bthr', p[..., L:], ckv)
    return jnp.einsum('bthr,rhd->bthd', o_lat, lp['mla_wvb'])


def ssd_chunked(x, a, Bm, Cm, s0, chunk):
    Bsz, T, G, E, P = x.shape
    N = Bm.shape[-1]
    nc = T // chunk
    x = x.reshape(Bsz, nc, chunk, G, E, P)
    a = a.reshape(Bsz, nc, chunk, G, E)
    Bm = Bm.reshape(Bsz, nc, chunk, G, N)
    Cm = Cm.reshape(Bsz, nc, chunk, G, N)
    acs = jnp.cumsum(a, axis=2)
    causal = jnp.tril(jnp.ones((chunk, chunk), bool))[:, :, None, None]
    seg = acs[:, :, :, None] - acs[:, :, None, :]
    lmat = jnp.exp(jnp.where(causal, seg, -jnp.inf))
    cb = jnp.einsum('bclgn,bcsgn->bclsg', Cm, Bm)
    y_diag = jnp.einsum('bclsge,bcsgep->bclgep', cb[..., None] * lmat, x)
    to_end = jnp.exp(acs[:, :, -1:] - acs)
    chunk_states = jnp.einsum('bclgn,bclgep->bcgepn', Bm, x * to_end[..., None])
    chunk_decay = jnp.exp(acs[:, :, -1])
    def step(s, inp):
        st, dec = inp
        return s * dec[..., None, None] + st, s
    s_final, s_in = lax.scan(step, s0, (jnp.moveaxis(chunk_states, 1, 0), jnp.moveaxis(chunk_decay, 1, 0)))
    s_in = jnp.moveaxis(s_in, 0, 1)
    y_off = jnp.einsum('bclgn,bcgepn->bclgep', Cm, s_in) * jnp.exp(acs)[..., None]
    return (y_diag + y_off).reshape(Bsz, T, G, E, P), s_final


def mamba2_branch(cols, conv_buf, s0, lp, chunk):
    Bsz, T, _ = cols.shape
    odt = cols.dtype
    z, xbc, dtr = jnp.split(cols, (SSM_W, SSM_W + CONV_CH), axis=-1)
    xpad = jnp.concatenate([conv_buf.astype(odt), xbc], axis=1)
    conv = lp['ssm_conv_b'] + sum(xpad[:, i:i + T] * lp['ssm_conv_w'][i] for i in range(CONV_W))
    conv = jax.nn.silu(conv).astype(F32)
    xs, Bm, Cm = jnp.split(conv, (SSM_W, SSM_W + SSM_GROUPS * SSM_STATE), axis=-1)
    xs = xs.reshape(Bsz, T, SSM_GROUPS, SSM_EPG, SSM_HD)
    Bm = Bm.reshape(Bsz, T, SSM_GROUPS, SSM_STATE)
    Cm = Cm.reshape(Bsz, T, SSM_GROUPS, SSM_STATE)
    dt = jax.nn.softplus((dtr + lp['ssm_dt_bias']).astype(F32)).reshape(Bsz, T, SSM_GROUPS, SSM_EPG)
    A = -jnp.exp(lp['ssm_a_log'].astype(F32)).reshape(SSM_GROUPS, SSM_EPG)
    s0g = s0.astype(F32).reshape(Bsz, SSM_GROUPS, SSM_EPG, SSM_HD, SSM_STATE)
    y, s = ssd_chunked(xs * dt[..., None], dt * A, Bm, Cm, s0g, chunk)
    y = y + xs * lp['ssm_d'].astype(F32).reshape(SSM_GROUPS, SSM_EPG)[..., None]
    y = y.reshape(Bsz, T, SSM_W)
    y = rms_norm(y * jax.nn.silu(z.astype(F32)), lp['ssm_norm_g'])
    return y.astype(odt), xpad[:, -(CONV_W - 1):], s.reshape(Bsz, SSM_HEADS, SSM_HD, SSM_STATE).astype(s0.dtype)


def peer(h, lp):
    Bsz, T, D = h.shape
    n = Bsz * T
    pad = (-n) % PK_TOK_BLOCK
    xt = jnp.pad(h.reshape(n, D), ((0, pad), (0, 0)))
    table_u, table_v = lp['pk_u'], lp['pk_v']
    def block(xb):
        q = (xb @ lp['pk_wq']).reshape(-1, PK_HEADS, 2, PK_DHALF)
        s = jnp.einsum('nhid,hikd->nhik', q, lp['pk_keys']).astype(F32)
        s_top, i_top = lax.top_k(s, PK_TOPK)
        cand_s = (s_top[:, :, 0, :, None] + s_top[:, :, 1, None, :]).reshape(-1, PK_HEADS, PK_TOPK * PK_TOPK)
        cand_i = (i_top[:, :, 0, :, None] * PK_NKEYS + i_top[:, :, 1, None, :]).reshape(-1, PK_HEADS, PK_TOPK * PK_TOPK)
        best_s, best_j = lax.top_k(cand_s, PK_TOPK)
        idx = jnp.take_along_axis(cand_i, best_j, axis=-1)
        gate = jax.nn.softmax(best_s, axis=-1)
        act = jax.nn.gelu(jnp.einsum('nhkd,nd->nhk', table_u[idx], xb).astype(F32))
        return jnp.einsum('nhk,nhkd->nd', (gate * act).astype(xb.dtype), table_v[idx])
    out = lax.map(block, xt.reshape(-1, PK_TOK_BLOCK, D))
    return out.reshape(-1, D)[:n].reshape(Bsz, T, D)


def layer_forward(x, c, lp, st, attend, pos, chunk):
    Bsz, T, D = x.shape
    mod = (jax.nn.silu(c) @ lp['ada_w'] + lp['ada_b'])[:, None, :]
    sh1, sc1, g1, sh2, sc2, g2 = jnp.split(mod, 6, axis=-1)
    h = rms_norm(x, lp['norm1_g']) * (1.0 + sc1) + sh1
    proj = h @ lp['w_in']
    rw_cols, mla_cols, ssm_cols, gate_cols = jnp.split(proj, IN_SPLITS, axis=-1)
    o_a, rw_shift, rw_state = rwkv7_branch(rw_cols, st[0], st[1], lp)
    q_nope, q_rope, k_nope, ckv, kpe, ksc = mla_project(mla_cols, pos, lp)
    o_b = attend(q_nope, q_rope, k_nope, ckv, kpe, ksc).reshape(Bsz, T, BR_W).astype(x.dtype)
    o_c, conv_buf, ssm_state = mamba2_branch(ssm_cols, st[2], st[3], lp, chunk)
    branches = jnp.stack([o_a, o_b, o_c], axis=2)
    u = jnp.einsum('btim,imd->btid', branches, lp['w_branch'])
    gates = jax.nn.sigmoid(gate_cols.reshape(Bsz, T, N_BRANCH, D))
    x = x + g1 * (jnp.sum(gates * u, axis=2) @ lp['w_out'])
    h2 = rms_norm(x, lp['norm2_g']) * (1.0 + sc2) + sh2
    x = x + g2 * peer(h2, lp)
    return x, (ckv, kpe, ksc, rw_shift, rw_state, conv_buf, ssm_state)


def setup_inputs(seed: int = 0) -> dict:
    key = jax.random.key(seed)
    ks = iter(jax.random.split(key, 64))
    def nrm(shape, s):
        return jax.random.normal(next(ks), shape, F32) * s
    def uni(shape, lo, hi):
        return jax.random.uniform(next(ks), shape, F32, lo, hi)
    L, D = DEPTH, D_MODEL
    n_pages = PAST_LEN // PAGE_SIZE
    n_used = DEC_BATCH * n_pages
    n_phys = n_used + (n_used + 3) // 4
    page_table = jax.random.permutation(next(ks), n_phys)[:n_used].reshape(DEC_BATCH, n_pages).astype(jnp.int32)
    dt0 = jnp.exp(uni((L, SSM_HEADS), math.log(1e-3), math.log(1e-1)))
    return {
        'x_prompt': nrm((BATCH, SEQ, D), 1.0),
        'x_sample': nrm((DEC_BATCH, DEC_SEQ, D), 1.0),
        'cache_ckv': nrm((L, n_phys, PAGE_SIZE, KV_LORA), 1.0),
        'cache_kpe': nrm((L, n_phys, PAGE_SIZE, ROPE_DIM), 1.0),
        'cache_kscale': uni((L, n_phys, PAGE_SIZE, MLA_HEADS), 0.7, 1.3),
        'state_rwkv_shift': nrm((L, DEC_BATCH, RW_COLS), 1.0),
        'state_rwkv_wkv': nrm((L, DEC_BATCH, RW_HEADS, RW_HD, RW_HD), 0.5),
        'state_ssm_conv': nrm((L, DEC_BATCH, CONV_W - 1, CONV_CH), 1.0),
        'state_ssm': nrm((L, DEC_BATCH, SSM_HEADS, SSM_HD, SSM_STATE), 0.5),
        'page_table': page_table,
        'c_prompt': nrm((BATCH, D), 1.0),
        'c_sample': nrm((DEC_BATCH, D), 1.0),
        'ada_w': nrm((L, D, 6 * D), 0.5 * D ** -0.5),
        'ada_b': nrm((L, 6 * D), 0.02),
        'norm1_g': 1.0 + nrm((L, D), 0.1),
        'norm2_g': 1.0 + nrm((L, D), 0.1),
        'w_in': nrm((L, D, IN_COLS), D ** -0.5),
        'rw_mu': uni((L, RW_COLS), 0.0, 1.0),
        'rw_w0': uni((L, RW_W), -4.0, 1.0),
        'rw_w2': nrm((L, RW_DECAY_LORA, RW_W), 0.1),
        'rw_a0': nrm((L, RW_W), 0.1),
        'rw_a2': nrm((L, RW_AAA_LORA, RW_W), 0.1),
        'rw_g2': nrm((L, RW_GATE_LORA, RW_W), RW_GATE_LORA ** -0.5),
        'rw_kk': 0.85 + nrm((L, RW_W), 0.05),
        'rw_ka': 1.0 + nrm((L, RW_W), 0.05),
        'rw_rk': nrm((L, RW_HEADS, RW_HD), 0.1),
        'rw_ln_w': 1.0 + nrm((L, RW_W), 0.1),
        'rw_ln_b': nrm((L, RW_W), 0.02),
        'mla_qa_g': 1.0 + nrm((L, Q_LORA), 0.1),
        'mla_wqb': nrm((L, Q_LORA, MLA_HEADS * QK_DIM), Q_LORA ** -0.5),
        'mla_kv_g': 1.0 + nrm((L, KV_LORA), 0.1),
        'mla_wkb': nrm((L, KV_LORA, MLA_HEADS, NOPE_DIM), KV_LORA ** -0.5),
        'mla_wvb': nrm((L, KV_LORA, MLA_HEADS, V_DIM), KV_LORA ** -0.5),
        'mla_q_g': 1.0 + nrm((L, QK_DIM), 0.1),
        'mla_k_g': 1.0 + nrm((L, QK_DIM), 0.1),
        'ssm_conv_w': nrm((L, CONV_W, CONV_CH), CONV_W ** -0.5),
        'ssm_conv_b': nrm((L, CONV_CH), 0.02),
        'ssm_dt_bias': dt0 + jnp.log(-jnp.expm1(-dt0)),
        'ssm_a_log': jnp.log(uni((L, SSM_HEADS), 1.0, 16.0)),
        'ssm_d': 1.0 + nrm((L, SSM_HEADS), 0.1),
        'ssm_norm_g': 1.0 + nrm((L, SSM_W), 0.1),
        'w_branch': nrm((L, N_BRANCH, BR_W, D), BR_W ** -0.5),
        'w_out': nrm((L, D, D), D ** -0.5),
        'pk_wq': nrm((L, D, PK_HEADS * PK_DKEY), D ** -0.5),
        'pk_keys': nrm((L, PK_HEADS, 2, PK_NKEYS, PK_DHALF), PK_DHALF ** -0.5),
        'pk_u': nrm((L, PK_EXPERTS, D), D ** -0.5),
        'pk_v': nrm((L, PK_EXPERTS, D), 0.25),
    }


def reference(x_prompt, x_sample, cache_ckv, cache_kpe, cache_kscale, state_rwkv_shift, state_rwkv_wkv,
              state_ssm_conv, state_ssm, page_table, c_prompt, c_sample, ada_w, ada_b, norm1_g, norm2_g, w_in,
              rw_mu, rw_w0, rw_w2, rw_a0, rw_a2, rw_g2, rw_kk, rw_ka, rw_rk, rw_ln_w, rw_ln_b,
              mla_qa_g, mla_wqb, mla_kv_g, mla_wkb, mla_wvb, mla_q_g, mla_k_g,
              ssm_conv_w, ssm_conv_b, ssm_dt_bias, ssm_a_log, ssm_d, ssm_norm_g,
              w_branch, w_out, pk_wq, pk_keys, pk_u, pk_v):
    params = dict(ada_w=ada_w, ada_b=ada_b, norm1_g=norm1_g, norm2_g=norm2_g, w_in=w_in,
                  rw_mu=rw_mu, rw_w0=rw_w0, rw_w2=rw_w2, rw_a0=rw_a0, rw_a2=rw_a2, rw_g2=rw_g2,
                  rw_kk=rw_kk, rw_ka=rw_ka, rw_rk=rw_rk, rw_ln_w=rw_ln_w, rw_ln_b=rw_ln_b,
                  mla_qa_g=mla_qa_g, mla_wqb=mla_wqb, mla_kv_g=mla_kv_g, mla_wkb=mla_wkb, mla_wvb=mla_wvb,
                  mla_q_g=mla_q_g, mla_k_g=mla_k_g,
                  ssm_conv_w=ssm_conv_w, ssm_conv_b=ssm_conv_b, ssm_dt_bias=ssm_dt_bias, ssm_a_log=ssm_a_log,
                  ssm_d=ssm_d, ssm_norm_g=ssm_norm_g,
                  w_branch=w_branch, w_out=w_out, pk_wq=pk_wq, pk_keys=pk_keys, pk_u=pk_u, pk_v=pk_v)
    Bp, S, _ = x_prompt.shape
    Bd, T, _ = x_sample.shape
    past_len = page_table.shape[1] * PAGE_SIZE
    pos_p = jnp.arange(S)
    pos_s = past_len + jnp.arange(T)
    chunk_p = SSD_CHUNK if S % SSD_CHUNK == 0 else S
    chunk_s = SSD_CHUNK if T % SSD_CHUNK == 0 else T
    odt = x_prompt.dtype
    zero_state = (jnp.zeros((Bp, RW_COLS), odt), jnp.zeros((Bp, RW_HEADS, RW_HD, RW_HD), odt),
                  jnp.zeros((Bp, CONV_W - 1, CONV_CH), odt), jnp.zeros((Bp, SSM_HEADS, SSM_HD, SSM_STATE), odt))
    hp, hs = x_prompt, x_sample
    new_p, new_s = [], []
    for l in range(DEPTH):
        lp = {name: arr[l] for name, arr in params.items()}
        attend_p = functools.partial(mla_prompt_attend, lp=lp, pos=pos_p)
        hp, st_p = layer_forward(hp, c_prompt, lp, zero_state, attend_p, pos_p, chunk_p)
        attend_s = functools.partial(
            mla_sample_attend, lp=lp, pos=pos_s,
            past_ckv=cache_ckv[l, page_table].reshape(Bd, past_len, KV_LORA),
            past_kpe=cache_kpe[l, page_table].reshape(Bd, past_len, ROPE_DIM),
            past_ksc=cache_kscale[l, page_table].reshape(Bd, past_len, MLA_HEADS))
        st_in = (state_rwkv_shift[l], state_rwkv_wkv[l], state_ssm_conv[l], state_ssm[l])
        hs, st_s = layer_forward(hs, c_sample, lp, st_in, attend_s, pos_s, chunk_s)
        new_p.append(st_p)
        new_s.append(st_s)
    return (hp, hs,
            jnp.stack([s[0] for s in new_p]), jnp.stack([s[0] for s in new_s]),
            jnp.stack([s[1] for s in new_p]), jnp.stack([s[1] for s in new_s]),
            jnp.stack([s[2] for s in new_p]), jnp.stack([s[2] for s in new_s]),
            jnp.stack([s[3] for s in new_p]), jnp.stack([s[3] for s in new_s]),
            jnp.stack([s[4] for s in new_p]), jnp.stack([s[4] for s in new_s]),
            jnp.stack([s[5] for s in new_p]), jnp.stack([s[5] for s in new_s]),
            jnp.stack([s[6] for s in new_p]), jnp.stack([s[6] for s in new_s]))
```

```python
import math, functools
import jax, jax.numpy as jnp
from jax import lax
from jax.experimental import pallas as pl
from jax.experimental.pallas import tpu as pltpu

D_MODEL = 1024
PAGE_SIZE = 128
EPS = 1e-6
F32 = jnp.float32
BF16 = jnp.bfloat16
RW_HEADS = 8
RW_HD = 64
RW_W = RW_HEADS * RW_HD
RW_DECAY_LORA = 64
RW_AAA_LORA = 64
RW_GATE_LORA = 128
RW_COLS = 3 * RW_W + RW_DECAY_LORA + RW_AAA_LORA + RW_GATE_LORA
RW_SPLITS = (RW_W, 2 * RW_W, 3 * RW_W, 3 * RW_W + RW_DECAY_LORA, 3 * RW_W + RW_DECAY_LORA + RW_AAA_LORA)
RW_GN_EPS = 64e-5
MLA_HEADS = 8
Q_LORA = 256
KV_LORA = 256
NOPE_DIM = 64
ROPE_DIM = 32
V_DIM = 64
QK_DIM = NOPE_DIM + ROPE_DIM
MLA_COLS = Q_LORA + KV_LORA + ROPE_DIM
ROPE_BASE = 10000.0
ATTN_SCALE = QK_DIM ** -0.5
Q_BLOCK = 128
SSM_HEADS = 8
SSM_HD = 64
SSM_W = SSM_HEADS * SSM_HD
SSM_GROUPS = 2
SSM_EPG = SSM_HEADS // SSM_GROUPS
SSM_STATE = 64
CONV_W = 4
CONV_CH = SSM_W + 2 * SSM_GROUPS * SSM_STATE
SSM_COLS = SSM_W + CONV_CH + SSM_HEADS
SSD_CHUNK = 128
N_BRANCH = 3
BR_W = RW_W
IN_COLS = RW_COLS + MLA_COLS + SSM_COLS + N_BRANCH * D_MODEL
IN_SPLITS = (RW_COLS, RW_COLS + MLA_COLS, RW_COLS + MLA_COLS + SSM_COLS)
PK_HEADS = 8
PK_DKEY = 256
PK_DHALF = PK_DKEY // 2
PK_NKEYS = 128
PK_TOPK = 16
PK_EXPERTS = PK_NKEYS * PK_NKEYS
PK_TOK_BLOCK = 256

LANE = 128
VMEM_LIMIT = 56 * 1024 * 1024

MLA_PAD = 640
SSM_PAD = 1408
PROJ_OFF = (0, RW_COLS, RW_COLS + MLA_PAD, RW_COLS + MLA_PAD + SSM_PAD)
PROJ_COLS = PROJ_OFF[3] + N_BRANCH * D_MODEL


def _cparams(sem):
    return pltpu.CompilerParams(dimension_semantics=sem, vmem_limit_bytes=VMEM_LIMIT)


def _pad_w_in(w_in):
    rw, mla, ssm, gate = jnp.split(w_in, IN_SPLITS, axis=-1)
    mla = jnp.pad(mla, ((0, 0), (0, MLA_PAD - MLA_COLS)))
    ssm = jnp.pad(ssm, ((0, 0), (0, SSM_PAD - SSM_COLS)))
    return jnp.concatenate([rw, mla, ssm, gate], axis=-1).astype(BF16)


def _inproj_kernel(x_ref, sc_ref, sh_ref, g_ref, w_ref, o_ref, h_scr):
    @pl.when(pl.program_id(1) == 0)
    def _():
        x = x_ref[...]
        y = x * lax.rsqrt(jnp.mean(x * x, axis=-1, keepdims=True) + EPS) * g_ref[...]
        h_scr[...] = (y * (1.0 + sc_ref[...]) + sh_ref[...]).astype(BF16)
    o_ref[...] = jnp.dot(h_scr[...], w_ref[...], preferred_element_type=F32)


def _row_spec(rows_per_group, tm, d):
    if rows_per_group >= tm:
        tiles = rows_per_group // tm
        return pl.BlockSpec((None, 1, d), lambda i, *_: (i // tiles, 0, 0))
    return pl.BlockSpec((None, tm, d), lambda i, *_: (i, 0, 0))


def _row_operand(v, seq, tm):
    if seq >= tm:
        return v[:, None, :]
    return jnp.repeat(v, seq, axis=0).reshape(-1, tm, v.shape[-1])


def in_proj(x2, sc, sh, g, w_pad, seq, tm):
    T, D = x2.shape
    tn = PROJ_COLS // 3
    return pl.pallas_call(
        _inproj_kernel,
        grid=(T // tm, PROJ_COLS // tn),
        in_specs=[
            pl.BlockSpec((tm, D), lambda i, j: (i, 0)),
            _row_spec(seq, tm, D),
            _row_spec(seq, tm, D),
            pl.BlockSpec((1, D), lambda i, j: (0, 0)),
            pl.BlockSpec((D, tn), lambda i, j: (0, j)),
        ],
        out_specs=pl.BlockSpec((tm, tn), lambda i, j: (i, j)),
        out_shape=jax.ShapeDtypeStruct((T, PROJ_COLS), F32),
        scratch_shapes=[pltpu.VMEM((tm, D), BF16)],
        compiler_params=_cparams(("parallel", "arbitrary")),
        name="in_proj",
    )(x2, _row_operand(sc, seq, tm), _row_operand(sh, seq, tm), g[None, :], w_pad)


def rms_norm(x, g):
    xf = x.astype(F32)
    y = xf * lax.rsqrt(jnp.mean(xf * xf, axis=-1, keepdims=True) + EPS)
    return (y * g.astype(F32)).astype(x.dtype)


def rope(x, pos):
    half = ROPE_DIM // 2
    inv = jnp.power(ROPE_BASE, -jnp.arange(half, dtype=F32) / half)
    ang = pos[..., None].astype(F32) * inv
    cos, sin = jnp.cos(ang), jnp.sin(ang)
    x1, x2 = x[..., :half].astype(F32), x[..., half:].astype(F32)
    return jnp.concatenate([x1 * cos - x2 * sin, x1 * sin + x2 * cos], axis=-1).astype(x.dtype)


def rwkv7_scan(r, w, k, v, a, b, s0):
    def step(s, inp):
        r_t, w_t, k_t, v_t, a_t, b_t = inp
        sa = jnp.einsum('bhvk,bhk->bhv', s, a_t)
        s = s * w_t[:, :, None, :] + sa[..., None] * b_t[:, :, None, :] + v_t[..., None] * k_t[:, :, None, :]
        return s, jnp.einsum('bhvk,bhk->bhv', s, r_t)
    xs = tuple(jnp.moveaxis(t, 1, 0) for t in (r, w, k, v, a, b))
    s, ys = lax.scan(step, s0, xs)
    return jnp.moveaxis(ys, 0, 1), s


def rwkv7_branch(cols, prev_row, s0, lp):
    Bsz, T, _ = cols.shape
    odt = cols.dtype
    xprev = jnp.concatenate([prev_row[:, None, :].astype(odt), cols[:, :-1]], axis=1)
    m = cols + (xprev - cols) * lp['rw_mu']
    r, k, v, wl, al, gl = jnp.split(m, RW_SPLITS, axis=-1)
    logw = -jax.nn.softplus(-(lp['rw_w0'] + jnp.tanh(wl) @ lp['rw_w2']).astype(F32)) - 0.5
    decay = jnp.exp(-jnp.exp(logw))
    a = jax.nn.sigmoid((lp['rw_a0'] + al @ lp['rw_a2']).astype(F32))
    g = (jax.nn.sigmoid(gl) @ lp['rw_g2']).astype(F32)
    r, k, v = r.astype(F32), k.astype(F32), v.astype(F32)
    kk = k * lp['rw_kk'].astype(F32)
    k = k * (1.0 + (a - 1.0) * lp['rw_ka'].astype(F32))
    r, k, v, a, decay, kk = tuple(t.reshape(Bsz, T, RW_HEADS, RW_HD) for t in (r, k, v, a, decay, kk))
    kk = kk / jnp.maximum(jnp.sqrt(jnp.sum(kk * kk, axis=-1, keepdims=True)), 1e-12)
    y, s = rwkv7_scan(r, decay, k, v, -kk, kk * a, s0.astype(F32))
    mu = jnp.mean(y, axis=-1, keepdims=True)
    var = jnp.mean(jnp.square(y - mu), axis=-1, keepdims=True)
    y = ((y - mu) * lax.rsqrt(var + RW_GN_EPS)).reshape(Bsz, T, RW_W)
    y = y * lp['rw_ln_w'].astype(F32) + lp['rw_ln_b'].astype(F32)
    bonus = jnp.sum(r * k * lp['rw_rk'].astype(F32), axis=-1, keepdims=True) * v
    y = y + bonus.reshape(Bsz, T, RW_W)
    return (y * g).astype(odt), cols[:, -1], s.astype(s0.dtype)


def mla_project(cols, pos, lp):
    Bsz, T, _ = cols.shape
    qa, ckv, kpe = jnp.split(cols, (Q_LORA, Q_LORA + KV_LORA), axis=-1)
    qa = rms_norm(qa, lp['mla_qa_g'])
    q = (qa @ lp['mla_wqb']).reshape(Bsz, T, MLA_HEADS, QK_DIM)
    q = rms_norm(q, lp['mla_q_g'])
    q_nope = q[..., :NOPE_DIM]
    q_rope = rope(q[..., NOPE_DIM:], pos[:, None])
    ckv = rms_norm(ckv, lp['mla_kv_g'])
    k_nope = jnp.einsum('btr,rhd->bthd', ckv, lp['mla_wkb'])
    kf, pf = k_nope.astype(F32), kpe.astype(F32)
    ms = (jnp.sum(kf * kf, axis=-1) + jnp.sum(pf * pf, axis=-1, keepdims=True)) / QK_DIM
    ksc = lax.rsqrt(ms + EPS).astype(cols.dtype)
    return q_nope, q_rope, k_nope, ckv, kpe, ksc


def blocked_causal_attention(q, k, v):
    Bsz, S, H, Dk = q.shape
    nb = S // Q_BLOCK
    qb = jnp.moveaxis(q.reshape(Bsz, nb, Q_BLOCK, H, Dk), 1, 0)
    kpos = jnp.arange(S)
    def block(args):
        i, qi = args
        s = jnp.einsum('bqhd,bkhd->bhqk', qi, k).astype(F32) * ATTN_SCALE
        qpos = i * Q_BLOCK + jnp.arange(Q_BLOCK)
        s = jnp.where(kpos[None, :] <= qpos[:, None], s, -jnp.inf)
        p = jax.nn.softmax(s, axis=-1).astype(v.dtype)
        return jnp.einsum('bhqk,bkhd->bqhd', p, v)
    o = lax.map(block, (jnp.arange(nb), qb))
    return jnp.moveaxis(o, 0, 1).reshape(Bsz, S, H, v.shape[-1])


def mla_prompt_attend(q_nope, q_rope, k_nope, ckv, kpe, ksc, lp, pos):
    gk = lp['mla_k_g']
    kr = rope(kpe * gk[NOPE_DIM:], pos)
    k = jnp.concatenate([k_nope * gk[:NOPE_DIM], jnp.broadcast_to(kr[:, :, None, :], q_rope.shape)], axis=-1)
    k = k * ksc[..., None]
    q = jnp.concatenate([q_nope, q_rope], axis=-1)
    v = jnp.einsum('btr,rhd->bthd', ckv, lp['mla_wvb'])
    return blocked_causal_attention(q, k, v)


def mla_sample_attend(q_nope, q_rope, k_nope, ckv, kpe, ksc, lp, pos, past_ckv, past_kpe, past_ksc):
    gk = lp['mla_k_g']
    T = q_nope.shape[1]
    L = past_ckv.shape[1]
    q_lat = jnp.einsum('bthd,rhd->bthr', q_nope * gk[:NOPE_DIM], lp['mla_wkb'])
    def scores(c_rows, kpe_rows, ksc_rows, kpos):
        kr = rope(kpe_rows * gk[NOPE_DIM:], kpos)
        s = jnp.einsum('bthr,blr->bhtl', q_lat, c_rows) + jnp.einsum('bthd,bld->bhtl', q_rope, kr)
        return s.astype(F32) * jnp.swapaxes(ksc_rows, 1, 2)[:, :, None, :].astype(F32) * ATTN_SCALE
    s_past = scores(past_ckv, past_kpe, past_ksc, jnp.arange(L))
    s_new = jnp.where(jnp.tril(jnp.ones((T, T), bool)), scores(ckv, kpe, ksc, pos), -jnp.inf)
    p = jax.nn.softmax(jnp.concatenate([s_past, s_new], axis=-1), axis=-1).astype(ckv.dtype)
    o_lat = jnp.einsum('bhtl,blr->bthr', p[..., :L], past_ckv) + jnp.einsum('bhtl,blr->bthr', p[..., L:], ckv)
    return jnp.einsum('bthr,rhd->bthd', o_lat, lp['mla_wvb'])


def ssd_chunked(x, a, Bm, Cm, s0, chunk):
    Bsz, T, G, E, P = x.shape
    N = Bm.shape[-1]
    nc = T // chunk
    x = x.reshape(Bsz, nc, chunk, G, E, P)
    a = a.reshape(Bsz, nc, chunk, G, E)
    Bm = Bm.reshape(Bsz, nc, chunk, G, N)
    Cm = Cm.reshape(Bsz, nc, chunk, G, N)
    acs = jnp.cumsum(a, axis=2)
    causal = jnp.tril(jnp.ones((chunk, chunk), bool))[:, :, None, None]
    seg = acs[:, :, :, None] - acs[:, :, None, :]
    lmat = jnp.exp(jnp.where(causal, seg, -jnp.inf))
    cb = jnp.einsum('bclgn,bcsgn->bclsg', Cm, Bm)
    y_diag = jnp.einsum('bclsge,bcsgep->bclgep', cb[..., None] * lmat, x)
    to_end = jnp.exp(acs[:, :, -1:] - acs)
    chunk_states = jnp.einsum('bclgn,bclgep->bcgepn', Bm, x * to_end[..., None])
    chunk_decay = jnp.exp(acs[:, :, -1])
    def step(s, inp):
        st, dec = inp
        return s * dec[..., None, None] + st, s
    s_final, s_in = lax.scan(step, s0, (jnp.moveaxis(chunk_states, 1, 0), jnp.moveaxis(chunk_decay, 1, 0)))
    s_in = jnp.moveaxis(s_in, 0, 1)
    y_off = jnp.einsum('bclgn,bcgepn->bclgep', Cm, s_in) * jnp.exp(acs)[..., None]
    return (y_diag + y_off).reshape(Bsz, T, G, E, P), s_final


def mamba2_branch(cols, conv_buf, s0, lp, chunk):
    Bsz, T, _ = cols.shape
    odt = cols.dtype
    z, xbc, dtr = jnp.split(cols, (SSM_W, SSM_W + CONV_CH), axis=-1)
    xpad = jnp.concatenate([conv_buf.astype(odt), xbc], axis=1)
    conv = lp['ssm_conv_b'] + sum(xpad[:, i:i + T] * lp['ssm_conv_w'][i] for i in range(CONV_W))
    conv = jax.nn.silu(conv).astype(F32)
    xs, Bm, Cm = jnp.split(conv, (SSM_W, SSM_W + SSM_GROUPS * SSM_STATE), axis=-1)
    xs = xs.reshape(Bsz, T, SSM_GROUPS, SSM_EPG, SSM_HD)
    Bm = Bm.reshape(Bsz, T, SSM_GROUPS, SSM_STATE)
    Cm = Cm.reshape(Bsz, T, SSM_GROUPS, SSM_STATE)
    dt = jax.nn.softplus((dtr + lp['ssm_dt_bias']).astype(F32)).reshape(Bsz, T, SSM_GROUPS, SSM_EPG)
    A = -jnp.exp(lp['ssm_a_log'].astype(F32)).reshape(SSM_GROUPS, SSM_EPG)
    s0g = s0.astype(F32).reshape(Bsz, SSM_GROUPS, SSM_EPG, SSM_HD, SSM_STATE)
    y, s = ssd_chunked(xs * dt[..., None], dt * A, Bm, Cm, s0g, chunk)
    y = y + xs * lp['ssm_d'].astype(F32).reshape(SSM_GROUPS, SSM_EPG)[..., None]
    y = y.reshape(Bsz, T, SSM_W)
    y = rms_norm(y * jax.nn.silu(z.astype(F32)), lp['ssm_norm_g'])
    return y.astype(odt), xpad[:, -(CONV_W - 1):], s.reshape(Bsz, SSM_HEADS, SSM_HD, SSM_STATE).astype(s0.dtype)


def peer(h, lp):
    Bsz, T, D = h.shape
    n = Bsz * T
    pad = (-n) % PK_TOK_BLOCK
    xt = jnp.pad(h.reshape(n, D), ((0, pad), (0, 0)))
    table_u, table_v = lp['pk_u'], lp['pk_v']
    def block(xb):
        q = (xb @ lp['pk_wq']).reshape(-1, PK_HEADS, 2, PK_DHALF)
        s = jnp.einsum('nhid,hikd->nhik', q, lp['pk_keys']).astype(F32)
        s_top, i_top = lax.top_k(s, PK_TOPK)
        cand_s = (s_top[:, :, 0, :, None] + s_top[:, :, 1, None, :]).reshape(-1, PK_HEADS, PK_TOPK * PK_TOPK)
        cand_i = (i_top[:, :, 0, :, None] * PK_NKEYS + i_top[:, :, 1, None, :]).reshape(-1, PK_HEADS, PK_TOPK * PK_TOPK)
        best_s, best_j = lax.top_k(cand_s, PK_TOPK)
        idx = jnp.take_along_axis(cand_i, best_j, axis=-1)
        gate = jax.nn.softmax(best_s, axis=-1)
        act = jax.nn.gelu(jnp.einsum('nhkd,nd->nhk', table_u[idx], xb).astype(F32))
        return jnp.einsum('nhk,nhkd->nd', (gate * act).astype(xb.dtype), table_v[idx])
    out = lax.map(block, xt.reshape(-1, PK_TOK_BLOCK, D))
    return out.reshape(-1, D)[:n].reshape(Bsz, T, D)


def layer_forward(x, c, lp, st, attend, pos, chunk):
    Bsz, T, D = x.shape
    mod = (jax.nn.silu(c) @ lp['ada_w'] + lp['ada_b'])
    sh1, sc1, g1, sh2, sc2, g2 = jnp.split(mod, 6, axis=-1)
    n_tok = Bsz * T
    tm = 512 if n_tok % 512 == 0 else n_tok
    proj = in_proj(x.reshape(n_tok, D), sc1, sh1, lp['norm1_g'], lp['w_in_pad'], T, tm).reshape(Bsz, T, PROJ_COLS)
    rw_cols = proj[..., PROJ_OFF[0]:PROJ_OFF[0] + RW_COLS]
    mla_cols = proj[..., PROJ_OFF[1]:PROJ_OFF[1] + MLA_COLS]
    ssm_cols = proj[..., PROJ_OFF[2]:PROJ_OFF[2] + SSM_COLS]
    gate_cols = proj[..., PROJ_OFF[3]:]
    g1, g2, sc2, sh2 = g1[:, None, :], g2[:, None, :], sc2[:, None, :], sh2[:, None, :]
    o_a, rw_shift, rw_state = rwkv7_branch(rw_cols, st[0], st[1], lp)
    q_nope, q_rope, k_nope, ckv, kpe, ksc = mla_project(mla_cols, pos, lp)
    o_b = attend(q_nope, q_rope, k_nope, ckv, kpe, ksc).reshape(Bsz, T, BR_W).astype(x.dtype)
    o_c, conv_buf, ssm_state = mamba2_branch(ssm_cols, st[2], st[3], lp, chunk)
    branches = jnp.stack([o_a, o_b, o_c], axis=2)
    u = jnp.einsum('btim,imd->btid', branches, lp['w_branch'])
    gates = jax.nn.sigmoid(gate_cols.reshape(Bsz, T, N_BRANCH, D))
    x = x + g1 * (jnp.sum(gates * u, axis=2) @ lp['w_out'])
    h2 = rms_norm(x, lp['norm2_g']) * (1.0 + sc2) + sh2
    x = x + g2 * peer(h2, lp)
    return x, (ckv, kpe, ksc, rw_shift, rw_state, conv_buf, ssm_state)


def kernel(x_prompt, x_sample, cache_ckv, cache_kpe, cache_kscale, state_rwkv_shift, state_rwkv_wkv,
           state_ssm_conv, state_ssm, page_table, c_prompt, c_sample, ada_w, ada_b, norm1_g, norm2_g, w_in,
           rw_mu, rw_w0, rw_w2, rw_a0, rw_a2, rw_g2, rw_kk, rw_ka, rw_rk, rw_ln_w, rw_ln_b,
           mla_qa_g, mla_wqb, mla_kv_g, mla_wkb, mla_wvb, mla_q_g, mla_k_g,
           ssm_conv_w, ssm_conv_b, ssm_dt_bias, ssm_a_log, ssm_d, ssm_norm_g,
           w_branch, w_out, pk_wq, pk_keys, pk_u, pk_v):
    params = dict(ada_w=ada_w, ada_b=ada_b, norm1_g=norm1_g, norm2_g=norm2_g,
                  w_in_pad=jax.vmap(_pad_w_in)(w_in),
                  rw_mu=rw_mu, rw_w0=rw_w0, rw_w2=rw_w2, rw_a0=rw_a0, rw_a2=rw_a2, rw_g2=rw_g2,
                  rw_kk=rw_kk, rw_ka=rw_ka, rw_rk=rw_rk, rw_ln_w=rw_ln_w, rw_ln_b=rw_ln_b,
                  mla_qa_g=mla_qa_g, mla_wqb=mla_wqb, mla_kv_g=mla_kv_g, mla_wkb=mla_wkb, mla_wvb=mla_wvb,
                  mla_q_g=mla_q_g, mla_k_g=mla_k_g,
                  ssm_conv_w=ssm_conv_w, ssm_conv_b=ssm_conv_b, ssm_dt_bias=ssm_dt_bias, ssm_a_log=ssm_a_log,
                  ssm_d=ssm_d, ssm_norm_g=ssm_norm_g,
                  w_branch=w_branch, w_out=w_out, pk_wq=pk_wq, pk_keys=pk_keys, pk_u=pk_u, pk_v=pk_v)
    depth = ada_w.shape[0]
    Bp, S, _ = x_prompt.shape
    Bd, T, _ = x_sample.shape
    past_len = page_table.shape[1] * PAGE_SIZE
    pos_p = jnp.arange(S)
    pos_s = past_len + jnp.arange(T)
    chunk_p = SSD_CHUNK if S % SSD_CHUNK == 0 else S
    chunk_s = SSD_CHUNK if T % SSD_CHUNK == 0 else T
    odt = x_prompt.dtype
    zero_state = (jnp.zeros((Bp, RW_COLS), odt), jnp.zeros((Bp, RW_HEADS, RW_HD, RW_HD), odt),
                  jnp.zeros((Bp, CONV_W - 1, CONV_CH), odt), jnp.zeros((Bp, SSM_HEADS, SSM_HD, SSM_STATE), odt))
    hp, hs = x_prompt, x_sample
    new_p, new_s = [], []
    for l in range(depth):
        lp = {name: arr[l] for name, arr in params.items()}
        attend_p = functools.partial(mla_prompt_attend, lp=lp, pos=pos_p)
        hp, st_p = layer_forward(hp, c_prompt, lp, zero_state, attend_p, pos_p, chunk_p)
        attend_s = functools.partial(
            mla_sample_attend, lp=lp, pos=pos_s,
            past_ckv=cache_ckv[l, page_table].reshape(Bd, past_len, KV_LORA),
            past_kpe=cache_kpe[l, page_table].reshape(Bd, past_len, ROPE_DIM),
            past_ksc=cache_kscale[l, page_table].reshape(Bd, past_len, MLA_HEADS))
        st_in = (state_rwkv_shift[l], state_rwkv_wkv[l], state_ssm_conv[l], state_ssm[l])
        hs, st_s = layer_forward(hs, c_sample, lp, st_in, attend_s, pos_s, chunk_s)
        new_p.append(st_p)
        new_s.append(st_s)
    return (hp, hs,
            jnp.stack([s[0] for s in new_p]), jnp.stack([s[0] for s in new_s]),
            jnp.stack([s[1] for s in new_p]), jnp.stack([s[1] for s in new_s]),
            jnp.stack([s[2] for s in new_p]), jnp.stack([s[2] for s in new_s]),
            jnp.stack([s[3] for s in new_p]), jnp.stack([s[3] for s in new_s]),
            jnp.stack([s[4] for s in new_p]), jnp.stack([s[4] for s in new_s]),
            jnp.stack([s[5] for s in new_p]), jnp.stack([s[5] for s in new_s]),
            jnp.stack([s[6] for s in new_p]), jnp.stack([s[6] for s in new_s]))
```

```python
import math, functools
import jax, jax.numpy as jnp
from jax import lax
from jax.experimental import pallas as pl
from jax.experimental.pallas import tpu as pltpu

D_MODEL = 1024
PAGE_SIZE = 128
EPS = 1e-6
F32 = jnp.float32
BF16 = jnp.bfloat16
RW_HEADS = 8
RW_HD = 64
RW_W = RW_HEADS * RW_HD
RW_DECAY_LORA = 64
RW_AAA_LORA = 64
RW_GATE_LORA = 128
RW_COLS = 3 * RW_W + RW_DECAY_LORA + RW_AAA_LORA + RW_GATE_LORA
RW_SPLITS = (RW_W, 2 * RW_W, 3 * RW_W, 3 * RW_W + RW_DECAY_LORA, 3 * RW_W + RW_DECAY_LORA + RW_AAA_LORA)
RW_GN_EPS = 64e-5
MLA_HEADS = 8
Q_LORA = 256
KV_LORA = 256
NOPE_DIM = 64
ROPE_DIM = 32
V_DIM = 64
QK_DIM = NOPE_DIM + ROPE_DIM
MLA_COLS = Q_LORA + KV_LORA + ROPE_DIM
ROPE_BASE = 10000.0
ATTN_SCALE = QK_DIM ** -0.5
Q_BLOCK = 128
SSM_HEADS = 8
SSM_HD = 64
SSM_W = SSM_HEADS * SSM_HD
SSM_GROUPS = 2
SSM_EPG = SSM_HEADS // SSM_GROUPS
SSM_STATE = 64
CONV_W = 4
CONV_CH = SSM_W + 2 * SSM_GROUPS * SSM_STATE
SSM_COLS = SSM_W + CONV_CH + SSM_HEADS
SSD_CHUNK = 128
N_BRANCH = 3
BR_W = RW_W
IN_COLS = RW_COLS + MLA_COLS + SSM_COLS + N_BRANCH * D_MODEL
IN_SPLITS = (RW_COLS, RW_COLS + MLA_COLS, RW_COLS + MLA_COLS + SSM_COLS)
PK_HEADS = 8
PK_DKEY = 256
PK_DHALF = PK_DKEY // 2
PK_NKEYS = 128
PK_TOPK = 16
PK_EXPERTS = PK_NKEYS * PK_NKEYS
PK_TOK_BLOCK = 256

LANE = 128
VMEM_LIMIT = 56 * 1024 * 1024

MLA_PAD = 640
SSM_PAD = 1408
PROJ_OFF = (0, RW_COLS, RW_COLS + MLA_PAD, RW_COLS + MLA_PAD + SSM_PAD)
PROJ_COLS = PROJ_OFF[3] + N_BRANCH * D_MODEL


def _cparams(sem):
    return pltpu.CompilerParams(dimension_semantics=sem, vmem_limit_bytes=VMEM_LIMIT)


def _pad_w_in(w_in):
    rw, mla, ssm, gate = jnp.split(w_in, IN_SPLITS, axis=-1)
    mla = jnp.pad(mla, ((0, 0), (0, MLA_PAD - MLA_COLS)))
    ssm = jnp.pad(ssm, ((0, 0), (0, SSM_PAD - SSM_COLS)))
    return jnp.concatenate([rw, mla, ssm, gate], axis=-1).astype(BF16)


PROJ_WIDTHS = (RW_COLS, MLA_PAD, SSM_PAD, N_BRANCH * D_MODEL)


def _inproj_kernel(x_ref, sc_ref, sh_ref, g_ref, w_rw, w_mla, w_ssm, w_gate, o_rw, o_mla, o_ssm, o_gate):
    x = x_ref[...]
    y = x * lax.rsqrt(jnp.mean(x * x, axis=-1, keepdims=True) + EPS) * g_ref[...]
    h = (y * (1.0 + sc_ref[...]) + sh_ref[...]).astype(BF16)
    for w_ref, o_ref in ((w_rw, o_rw), (w_mla, o_mla), (w_ssm, o_ssm), (w_gate, o_gate)):
        o_ref[...] = jnp.dot(h, w_ref[...], preferred_element_type=F32)


def _row_spec(rows_per_group, tm, d):
    if rows_per_group >= tm:
        tiles = rows_per_group // tm
        return pl.BlockSpec((None, 1, d), lambda i, *_: (i // tiles, 0, 0))
    return pl.BlockSpec((None, tm, d), lambda i, *_: (i, 0, 0))


def _row_operand(v, seq, tm):
    if seq >= tm:
        return v[:, None, :]
    return jnp.repeat(v, seq, axis=0).reshape(-1, tm, v.shape[-1])


def in_proj(x2, sc, sh, g, w_pad, seq, tm):
    T, D = x2.shape
    ws = [w_pad[:, PROJ_OFF[i]:PROJ_OFF[i] + PROJ_WIDTHS[i]] for i in range(4)]
    return pl.pallas_call(
        _inproj_kernel,
        grid=(T // tm,),
        in_specs=[_tok_spec(tm, D), _row_spec(seq, tm, D), _row_spec(seq, tm, D), _full_spec((1, D))]
        + [_full_spec((D, n)) for n in PROJ_WIDTHS],
        out_specs=[_tok_spec(tm, n) for n in PROJ_WIDTHS],
        out_shape=[jax.ShapeDtypeStruct((T, n), F32) for n in PROJ_WIDTHS],
        compiler_params=_cparams(("parallel",)),
        name="in_proj",
    )(x2, _row_operand(sc, seq, tm), _row_operand(sh, seq, tm), g[None, :], *ws)


def _seg_sum(x, seg):
    hi = x.astype(BF16)
    lo = (x - hi.astype(F32)).astype(BF16)
    return (jnp.dot(hi, seg, preferred_element_type=F32) + jnp.dot(lo, seg, preferred_element_type=F32))


def _seg_matrix(width, seg):
    i = jnp.arange(width) // seg
    return (i[:, None] == i[None, :]).astype(BF16)


def _gelu_tanh(x):
    return 0.5 * x * (1.0 + jnp.tanh(math.sqrt(2.0 / math.pi) * (x + 0.044715 * (x * x * x))))


def _tok_spec(tm, d, col_block=0):
    return pl.BlockSpec((tm, d), lambda i, *_: (i, col_block))


def _full_spec(shape):
    n = len(shape)
    return pl.BlockSpec(shape, lambda *_: (0,) * n)


def _rwkv_prep_kernel(c_ref, xp_ref, mu_ref, w0_ref, w2_ref, a0_ref, a2_ref, g2_ref, kk_ref, ka_ref, seg_ref,
                      r_ref, w_ref, k_ref, v_ref, a_ref, b_ref, g_ref):
    c = c_ref[...]
    m = c + (xp_ref[...] - c) * mu_ref[...]
    r, k, v = m[:, :RW_W], m[:, RW_W:2 * RW_W], m[:, 2 * RW_W:3 * RW_W]
    wl = m[:, RW_SPLITS[2]:RW_SPLITS[3]]
    al = m[:, RW_SPLITS[3]:RW_SPLITS[4]]
    gl = m[:, RW_SPLITS[4]:]
    nz = -(w0_ref[...] + jnp.dot(jnp.tanh(wl), w2_ref[...], preferred_element_type=F32))
    softplus = jnp.maximum(nz, 0.0) + jnp.log(1.0 + jnp.exp(-jnp.abs(nz)))
    decay = jnp.exp(-jnp.exp(-softplus - 0.5))
    a = jax.nn.sigmoid(a0_ref[...] + jnp.dot(al, a2_ref[...], preferred_element_type=F32))
    g = jnp.dot(jax.nn.sigmoid(gl), g2_ref[...], preferred_element_type=F32)
    kk = k * kk_ref[...]
    k = k * (1.0 + (a - 1.0) * ka_ref[...])
    kk = kk / jnp.maximum(jnp.sqrt(_seg_sum(kk * kk, seg_ref[...])), 1e-12)
    r_ref[...] = r
    w_ref[...] = decay
    k_ref[...] = k
    v_ref[...] = v
    a_ref[...] = -kk
    b_ref[...] = kk * a
    g_ref[...] = g


def rwkv_prep(proj, xprev, lp, tm):
    T = proj.shape[0]
    row = lambda v: v.reshape(1, -1)
    out = jax.ShapeDtypeStruct((T, RW_W), F32)
    return pl.pallas_call(
        _rwkv_prep_kernel,
        grid=(T // tm,),
        in_specs=[_tok_spec(tm, RW_COLS), _tok_spec(tm, RW_COLS), _full_spec((1, RW_COLS)),
                  _full_spec((1, RW_W)), _full_spec((RW_DECAY_LORA, RW_W)),
                  _full_spec((1, RW_W)), _full_spec((RW_AAA_LORA, RW_W)),
                  _full_spec((RW_GATE_LORA, RW_W)), _full_spec((1, RW_W)), _full_spec((1, RW_W)),
                  _full_spec((RW_W, RW_W))],
        out_specs=[_tok_spec(tm, RW_W)] * 7,
        out_shape=[out] * 7,
        compiler_params=_cparams(("parallel",)),
        name="rwkv_prep",
    )(proj, xprev, row(lp['rw_mu']), row(lp['rw_w0']), lp['rw_w2'], row(lp['rw_a0']), lp['rw_a2'], lp['rw_g2'],
      row(lp['rw_kk']), row(lp['rw_ka']), _seg_matrix(RW_W, RW_HD))


SCAN_HEADS = 32
SCAN_VM = LANE // SCAN_HEADS
SCAN_VD = RW_HD // SCAN_VM


def _scan_kernel(w_ref, a_ref, b_ref, k_ref, r_ref, v_ref, s0_ref, y_ref, sT_ref, s_scr):
    c = pl.program_id(1)

    @pl.when(c == 0)
    def _():
        s_scr[...] = s0_ref[...]

    def step(t, carry):
        A, W, Bm, K, R = a_ref[t], w_ref[t], b_ref[t], k_ref[t], r_ref[t]
        V = v_ref[t]
        ys = []
        for vd in range(SCAN_VD):
            S = s_scr[vd]
            sa = jnp.sum(S * A, axis=0, keepdims=True)
            S = S * W + sa * Bm + V[vd:vd + 1, :] * K
            s_scr[vd] = S
            ys.append(jnp.sum(S * R, axis=0, keepdims=True))
        y_ref[t] = jnp.concatenate(ys, axis=0)
        return carry

    lax.fori_loop(0, y_ref.shape[0], step, 0)

    @pl.when(c == pl.num_programs(1) - 1)
    def _():
        sT_ref[...] = s_scr[...]


def _to_scan_kh(x, n_groups):
    B, T, _ = x.shape
    x = x.reshape(B, T, RW_HEADS, RW_HD).transpose(1, 3, 0, 2).reshape(T, RW_HD, n_groups, SCAN_HEADS)
    x = jnp.broadcast_to(x[..., None], (T, RW_HD, n_groups, SCAN_HEADS, SCAN_VM))
    return x.transpose(2, 0, 1, 3, 4).reshape(n_groups, T, RW_HD, LANE)


def _to_scan_v(x, n_groups):
    B, T, _ = x.shape
    x = x.reshape(B, T, RW_HEADS, SCAN_VD, SCAN_VM).transpose(1, 3, 0, 2, 4)
    x = x.reshape(T, SCAN_VD, n_groups, SCAN_HEADS * SCAN_VM)
    return x.transpose(2, 0, 1, 3)


def _from_scan_v(y, B):
    G, T = y.shape[:2]
    y = y.transpose(1, 2, 0, 3).reshape(T, SCAN_VD, B, RW_HEADS, SCAN_VM)
    return y.transpose(2, 0, 3, 1, 4).reshape(B, T, RW_W)


def _state_to_scan(s, n_groups):
    B = s.shape[0]
    s = s.reshape(B, RW_HEADS, SCAN_VD, SCAN_VM, RW_HD).transpose(2, 4, 0, 1, 3)
    s = s.reshape(SCAN_VD, RW_HD, n_groups, SCAN_HEADS * SCAN_VM)
    return s.transpose(2, 0, 1, 3)


def _state_from_scan(s, B):
    s = s.transpose(1, 2, 0, 3).reshape(SCAN_VD, RW_HD, B, RW_HEADS, SCAN_VM)
    return s.transpose(2, 3, 0, 4, 1).reshape(B, RW_HEADS, RW_HD, RW_HD)


def state_scan(w, a, b, k, r, v, s0, tc):
    B, T, _ = w.shape
    G = (B * RW_HEADS) // SCAN_HEADS
    kh = [_to_scan_kh(t, G) for t in (w, a, b, k, r)]
    vv = _to_scan_v(v, G)
    ss = _state_to_scan(s0, G)
    kh_spec = pl.BlockSpec((None, tc, RW_HD, LANE), lambda g, c: (g, c, 0, 0))
    v_spec = pl.BlockSpec((None, tc, SCAN_VD, LANE), lambda g, c: (g, c, 0, 0))
    s_spec = pl.BlockSpec((None, SCAN_VD, RW_HD, LANE), lambda g, c: (g, 0, 0, 0))
    y, sT = pl.pallas_call(
        _scan_kernel,
        grid=(G, T // tc),
        in_specs=[kh_spec] * 5 + [v_spec, s_spec],
        out_specs=[v_spec, s_spec],
        out_shape=[jax.ShapeDtypeStruct((G, T, SCAN_VD, LANE), F32),
                   jax.ShapeDtypeStruct((G, SCAN_VD, RW_HD, LANE), F32)],
        scratch_shapes=[pltpu.VMEM((SCAN_VD, RW_HD, LANE), F32)],
        compiler_params=_cparams(("parallel", "arbitrary")),
        name="state_scan",
    )(*kh, vv, ss)
    return _from_scan_v(y, B), _state_from_scan(sT, B)


def _rwkv_post_kernel(y_ref, r_ref, k_ref, v_ref, g_ref, lnw_ref, lnb_ref, rk_ref, seg_ref, o_ref):
    seg = seg_ref[...]
    y = y_ref[...]
    d = y - _seg_sum(y, seg) * (1.0 / RW_HD)
    var = _seg_sum(d * d, seg) * (1.0 / RW_HD)
    yn = d * lax.rsqrt(var + RW_GN_EPS) * lnw_ref[...] + lnb_ref[...]
    bonus = _seg_sum(r_ref[...] * k_ref[...] * rk_ref[...], seg) * v_ref[...]
    o_ref[...] = (yn + bonus) * g_ref[...]


def rwkv_post(y, r, k, v, g, lp, tm):
    T = y.shape[0]
    row = lambda t: t.reshape(1, -1)
    return pl.pallas_call(
        _rwkv_post_kernel,
        grid=(T // tm,),
        in_specs=[_tok_spec(tm, RW_W)] * 5 + [_full_spec((1, RW_W))] * 3 + [_full_spec((RW_W, RW_W))],
        out_specs=_tok_spec(tm, RW_W),
        out_shape=jax.ShapeDtypeStruct((T, RW_W), F32),
        compiler_params=_cparams(("parallel",)),
        name="rwkv_post",
    )(y, r, k, v, g, row(lp['rw_ln_w']), row(lp['rw_ln_b']), row(lp['rw_rk']), _seg_matrix(RW_W, RW_HD))


def rwkv_branch(rw2, B, T, prev_row, s0, lp, tm, tc):
    rw_cols = rw2.reshape(B, T, RW_COLS)
    xprev = jnp.concatenate([prev_row[:, None, :], rw_cols[:, :-1]], axis=1).reshape(B * T, RW_COLS)
    r, w, k, v, a, b, g = rwkv_prep(rw2, xprev, lp, tm)
    sh = lambda t: t.reshape(B, T, RW_W)
    y, sT = state_scan(sh(w), sh(a), sh(b), sh(k), sh(r), sh(v), s0, tc)
    o = rwkv_post(y.reshape(B * T, RW_W), r, k, v, g, lp, tm)
    return o, rw_cols[:, -1], sT


def _topk_rows(s, rows, k, payload=None):
    n = s.shape[0]
    vals, outs = [], []
    for _ in range(k):
        m = jnp.max(s, axis=0, keepdims=True)
        pos = jnp.min(jnp.where(s == m, rows, n), axis=0, keepdims=True)
        sel = rows == pos
        vals.append(m)
        outs.append(pos if payload is None else jnp.sum(jnp.where(sel, payload, 0), axis=0, keepdims=True))
        s = jnp.where(sel, -jnp.inf, s)
    return jnp.concatenate(vals, axis=0), jnp.concatenate(outs, axis=0)


def _peer_route_kernel(x_ref, sc_ref, sh_ref, g_ref, wq_ref, keys_ref, h_ref, eid_ref, gate_ref):
    x = x_ref[...]
    tm = x.shape[0]
    y = x * lax.rsqrt(jnp.mean(x * x, axis=-1, keepdims=True) + EPS) * g_ref[...]
    h = y * (1.0 + sc_ref[...]) + sh_ref[...]
    h_ref[...] = h
    q = jnp.dot(h.astype(BF16), wq_ref[...], preferred_element_type=F32)
    rows_k = lax.broadcasted_iota(jnp.int32, (PK_NKEYS, tm), 0)
    rows_c = lax.broadcasted_iota(jnp.int32, (PK_TOPK * PK_TOPK, tm), 0)
    eids, gates = [], []
    for hh in range(PK_HEADS):
        tops = []
        for i in range(2):
            c0 = (hh * 2 + i) * PK_DHALF
            s = lax.dot_general(keys_ref[hh * 2 + i], q[:, c0:c0 + PK_DHALF].astype(BF16),
                                (((1,), (1,)), ((), ())), preferred_element_type=F32)
            tops.append(_topk_rows(s, rows_k, PK_TOPK))
        (s0, i0), (s1, i1) = tops
        cand = (s0[:, None, :] + s1[None, :, :]).reshape(PK_TOPK * PK_TOPK, tm)
        cid = (i0[:, None, :] * PK_NKEYS + i1[None, :, :]).reshape(PK_TOPK * PK_TOPK, tm)
        best, eid = _topk_rows(cand, rows_c, PK_TOPK, payload=cid)
        e = jnp.exp(best - best[0:1])
        gates.append(e / jnp.sum(e, axis=0, keepdims=True))
        eids.append(eid)
    eid_ref[...] = jnp.concatenate(eids, axis=0).T
    gate_ref[...] = jnp.concatenate(gates, axis=0).T


def peer_route(x2, sc, sh, norm_g, wq_bf, keys_bf, seq, tm):
    T, D = x2.shape
    nq = PK_HEADS * PK_DKEY
    nsel = PK_HEADS * PK_TOPK
    return pl.pallas_call(
        _peer_route_kernel,
        grid=(T // tm,),
        in_specs=[_tok_spec(tm, D), _row_spec(seq, tm, D), _row_spec(seq, tm, D), _full_spec((1, D)),
                  _full_spec((D, nq)), _full_spec((2 * PK_HEADS, PK_NKEYS, PK_DHALF))],
        out_specs=[_tok_spec(tm, D), _tok_spec(tm, nsel), _tok_spec(tm, nsel)],
        out_shape=[jax.ShapeDtypeStruct((T, D), F32), jax.ShapeDtypeStruct((T, nsel), jnp.int32),
                   jax.ShapeDtypeStruct((T, nsel), F32)],
        compiler_params=_cparams(("parallel",)),
        name="peer_route",
    )(x2, _row_operand(sc, seq, tm), _row_operand(sh, seq, tm), norm_g[None, :], wq_bf, keys_bf)


PEER_TOK = 32
PEER_SEL = PK_HEADS * PK_TOPK


def _peer_gather_kernel(layer, eid_ref, h_ref, gate_ref, x_ref, g2_ref, u_all, v_all, o_ref,
                        eid_smem, ubuf, vbuf, sem_idx, sem_u, sem_v):
    u_hbm, v_hbm = u_all.at[layer], v_all.at[layer]
    idx_copy = pltpu.make_async_copy(eid_ref, eid_smem, sem_idx)
    idx_copy.start()
    idx_copy.wait()

    def issue(t, slot):
        def body(kk, carry):
            e = eid_smem[t, kk]
            pltpu.make_async_copy(u_hbm.at[pl.ds(e, 1)], ubuf.at[slot, pl.ds(kk, 1)], sem_u.at[slot]).start()
            pltpu.make_async_copy(v_hbm.at[pl.ds(e, 1)], vbuf.at[slot, pl.ds(kk, 1)], sem_v.at[slot]).start()
            return carry
        lax.fori_loop(0, PEER_SEL, body, 0, unroll=8)

    def wait(slot):
        pltpu.make_async_copy(u_hbm.at[pl.ds(0, PEER_SEL)], ubuf.at[slot], sem_u.at[slot]).wait()
        pltpu.make_async_copy(v_hbm.at[pl.ds(0, PEER_SEL)], vbuf.at[slot], sem_v.at[slot]).wait()

    eye = (lax.broadcasted_iota(jnp.int32, (PEER_SEL, PEER_SEL), 0)
           == lax.broadcasted_iota(jnp.int32, (PEER_SEL, PEER_SEL), 1))
    n_tok = h_ref.shape[0]
    issue(0, 0)

    def token(t, carry):
        slot = t % 2

        @pl.when(t + 1 < n_tok)
        def _():
            issue(t + 1, 1 - slot)

        wait(slot)
        h = h_ref[pl.ds(t, 1), :]
        act = jnp.sum(ubuf[slot] * h, axis=-1, keepdims=True)
        gate = jnp.sum(jnp.where(eye, gate_ref[pl.ds(t, 1), :], 0.0), axis=-1, keepdims=True)
        wgt = gate * _gelu_tanh(act)
        y = jnp.sum(vbuf[slot] * wgt, axis=0, keepdims=True)
        g2 = g2_ref[...] if g2_ref.shape[0] == 1 else g2_ref[pl.ds(t, 1), :]
        o_ref[pl.ds(t, 1), :] = x_ref[pl.ds(t, 1), :] + g2 * y
        return carry

    lax.fori_loop(0, n_tok, token, 0)


def peer_gather(eid, h, gate, x2, g2, seq, pk_u, pk_v, layer):
    T, D = h.shape
    tb = PEER_TOK
    return pl.pallas_call(
        functools.partial(_peer_gather_kernel, layer),
        grid=(T // tb,),
        in_specs=[_tok_spec(tb, PEER_SEL), _tok_spec(tb, D), _tok_spec(tb, PEER_SEL), _tok_spec(tb, D),
                  _row_spec(seq, tb, D),
                  pl.BlockSpec(memory_space=pl.ANY), pl.BlockSpec(memory_space=pl.ANY)],
        out_specs=_tok_spec(tb, D),
        out_shape=jax.ShapeDtypeStruct((T, D), F32),
        scratch_shapes=[pltpu.SMEM((tb, PEER_SEL), jnp.int32),
                        pltpu.VMEM((2, PEER_SEL, D), F32), pltpu.VMEM((2, PEER_SEL, D), F32),
                        pltpu.SemaphoreType.DMA, pltpu.SemaphoreType.DMA((2,)), pltpu.SemaphoreType.DMA((2,))],
        compiler_params=_cparams(("arbitrary",)),
        name="peer_gather",
    )(eid, h, gate, x2, _row_operand(g2, seq, tb), pk_u, pk_v)


def _ada_kernel(c_ref, w_ref, b_ref, o_ref):
    c = c_ref[...]
    o_ref[...] = jnp.dot(c * jax.nn.sigmoid(c), w_ref[...], preferred_element_type=F32) + b_ref[...]


def ada_mod(c, ada_w, ada_b, layer):
    B, D = c.shape
    N = ada_w.shape[2]
    tn = D
    return pl.pallas_call(
        _ada_kernel,
        grid=(N // tn,),
        in_specs=[_full_spec((B, D)), pl.BlockSpec((None, D, tn), lambda j: (layer, 0, j)),
                  pl.BlockSpec((None, 1, tn), lambda j: (layer, 0, j))],
        out_specs=pl.BlockSpec((B, tn), lambda j: (0, j)),
        out_shape=jax.ShapeDtypeStruct((B, N), F32),
        compiler_params=_cparams(("parallel",)),
        name="ada_mod",
    )(c, ada_w, ada_b[:, None, :])


MLA_HP = LANE


def _rope_tables(pos):
    half = ROPE_DIM // 2
    inv = jnp.power(ROPE_BASE, -jnp.arange(half, dtype=F32) / half)
    ang = pos[:, None].astype(F32) * inv
    cos, sin = jnp.cos(ang), jnp.sin(ang)
    n = pos.shape[0]
    one, zero = jnp.ones((n, NOPE_DIM), F32), jnp.zeros((n, NOPE_DIM), F32)
    tail = jnp.zeros((n, MLA_HP - QK_DIM), F32)
    c = jnp.concatenate([one, cos, cos, tail], axis=-1)
    s_up = jnp.concatenate([zero, jnp.zeros_like(sin), sin, tail], axis=-1)
    s_dn = jnp.concatenate([zero, -sin, jnp.zeros_like(sin), tail], axis=-1)
    return c, s_up, s_dn


def _rope_apply(x, c, s_up, s_dn):
    half = ROPE_DIM // 2
    return x * c + pltpu.roll(x, half, 1) * s_up + pltpu.roll(x, MLA_HP - half, 1) * s_dn


def _head_ind(width, seg, n):
    return (jnp.arange(width)[:, None] // seg == jnp.arange(n)[None, :]).astype(BF16)


def _mla_common(c_ref, qag_ref, wqb_ref, qg_ref, kvg_ref, wkb_ref, kgn_ref, kgr_ref, ind_ref,
                cos_ref, sup_ref, sdn_ref):
    c = c_ref[...]
    cos, sup, sdn = cos_ref[...], sup_ref[...], sdn_ref[...]
    qa, ckv, kpe = c[:, :Q_LORA], c[:, Q_LORA:Q_LORA + KV_LORA], c[:, Q_LORA + KV_LORA:Q_LORA + KV_LORA + LANE]
    qa = qa * lax.rsqrt(jnp.mean(qa * qa, axis=-1, keepdims=True) + EPS) * qag_ref[...]
    q = jnp.dot(qa.astype(BF16), wqb_ref[...], preferred_element_type=F32)
    ind = ind_ref[...]
    qss = _seg_sum(q * q, ind) * (1.0 / QK_DIM)
    qs = []
    for h in range(MLA_HEADS):
        qh = q[:, h * MLA_HP:(h + 1) * MLA_HP] * lax.rsqrt(qss[:, h:h + 1] + EPS) * qg_ref[...]
        qs.append(_rope_apply(qh, cos, sup, sdn))
    ckv = ckv * lax.rsqrt(jnp.mean(ckv * ckv, axis=-1, keepdims=True) + EPS) * kvg_ref[...]
    kn = jnp.dot(ckv.astype(BF16), wkb_ref[...], preferred_element_type=F32)
    pe2 = jnp.sum(kpe * kpe, axis=-1, keepdims=True)
    ksc = lax.rsqrt((_seg_sum(kn * kn, ind) + pe2) * (1.0 / QK_DIM) + EPS)
    kr = _rope_apply(pltpu.roll(kpe, NOPE_DIM, 1) * kgr_ref[...], cos, sup, sdn)
    return qs, ckv, kn, kr, ksc


def _mla_prompt_kernel(c_ref, qag_ref, wqb_ref, qg_ref, kvg_ref, wkb_ref, kgn_ref, kgr_ref, ind_ref,
                       cos_ref, sup_ref, sdn_ref, wvb_ref, q_ref, k_ref, v_ref, ckv_ref, ksc_ref):
    qs, ckv, kn, kr, ksc = _mla_common(c_ref, qag_ref, wqb_ref, qg_ref, kvg_ref, wkb_ref, kgn_ref, kgr_ref,
                                       ind_ref, cos_ref, sup_ref, sdn_ref)
    for h in range(MLA_HEADS):
        q_ref[h] = (qs[h] * ATTN_SCALE).astype(BF16)
        kh = (kn[:, h * MLA_HP:(h + 1) * MLA_HP] * kgn_ref[...] + kr) * ksc[:, h:h + 1]
        k_ref[h] = kh.astype(BF16)
    v_ref[...] = jnp.dot(ckv.astype(BF16), wvb_ref[...], preferred_element_type=F32).astype(BF16)
    ckv_ref[...] = ckv
    ksc_ref[...] = ksc


def _mla_weights(lp):
    pad_h = lambda w, d: jnp.pad(w.reshape(w.shape[0], MLA_HEADS, d), ((0, 0), (0, 0), (0, MLA_HP - d))
                                 ).reshape(w.shape[0], MLA_HEADS * MLA_HP).astype(BF16)
    gk = lp['mla_k_g']
    return dict(
        qag=lp['mla_qa_g'][None, :], wqb=pad_h(lp['mla_wqb'], QK_DIM),
        qg=jnp.pad(lp['mla_q_g'], (0, MLA_HP - QK_DIM))[None, :], kvg=lp['mla_kv_g'][None, :],
        wkb=pad_h(lp['mla_wkb'].reshape(KV_LORA, -1), NOPE_DIM),
        kgn=jnp.pad(gk[:NOPE_DIM], (0, MLA_HP - NOPE_DIM))[None, :],
        kgr=jnp.pad(gk[NOPE_DIM:], (NOPE_DIM, MLA_HP - QK_DIM))[None, :],
        ind=_head_ind(MLA_HEADS * MLA_HP, MLA_HP, MLA_HEADS),
        wvb=lp['mla_wvb'].reshape(KV_LORA, MLA_HEADS * V_DIM).astype(BF16))


def _mla_in_specs(tm, seq):
    tiles = max(seq // tm, 1)
    tab = pl.BlockSpec((tm, MLA_HP), lambda i: (i % tiles, 0))
    return [_tok_spec(tm, MLA_PAD), _full_spec((1, Q_LORA)), _full_spec((Q_LORA, MLA_HEADS * MLA_HP)),
            _full_spec((1, MLA_HP)), _full_spec((1, KV_LORA)), _full_spec((KV_LORA, MLA_HEADS * MLA_HP)),
            _full_spec((1, MLA_HP)), _full_spec((1, MLA_HP)), _full_spec((MLA_HEADS * MLA_HP, MLA_HEADS)),
            tab, tab, tab]


def mla_prompt_proj(mla_cols, pos, lp, B, seq, tm):
    T = B * seq
    w = _mla_weights(lp)
    tiles = seq // tm
    hspec = pl.BlockSpec((None, MLA_HEADS, tm, MLA_HP), lambda i: (i // tiles, 0, i % tiles, 0))
    q, k, v, ckv, ksc = pl.pallas_call(
        _mla_prompt_kernel,
        grid=(T // tm,),
        in_specs=_mla_in_specs(tm, seq) + [_full_spec((KV_LORA, MLA_HEADS * V_DIM))],
        out_specs=[hspec, hspec, _tok_spec(tm, MLA_HEADS * V_DIM), _tok_spec(tm, KV_LORA), _tok_spec(tm, MLA_HEADS)],
        out_shape=[jax.ShapeDtypeStruct((B, MLA_HEADS, seq, MLA_HP), BF16)] * 2
        + [jax.ShapeDtypeStruct((T, MLA_HEADS * V_DIM), BF16), jax.ShapeDtypeStruct((T, KV_LORA), F32),
           jax.ShapeDtypeStruct((T, MLA_HEADS), F32)],
        compiler_params=_cparams(("parallel",)),
        name="mla_prompt_proj",
    )(mla_cols, w['qag'], w['wqb'], w['qg'], w['kvg'], w['wkb'], w['kgn'], w['kgr'], w['ind'],
      *_rope_tables(pos), w['wvb'])
    return q, k, v.reshape(B, seq, -1), ckv, ksc


def _flash_kernel(q_ref, k_ref, v_ref, o_ref, m_scr, l_scr, acc_scr):
    qi, ki = pl.program_id(2), pl.program_id(3)
    tq, tk = q_ref.shape[1], k_ref.shape[1]

    @pl.when(ki == 0)
    def _():
        m_scr[...] = jnp.full(m_scr.shape, -jnp.inf, F32)
        l_scr[...] = jnp.zeros(l_scr.shape, F32)
        acc_scr[...] = jnp.zeros(acc_scr.shape, F32)

    @pl.when(ki <= qi)
    def _():
        rows = qi * tq + lax.broadcasted_iota(jnp.int32, (tq, tk), 0)
        cols = ki * tk + lax.broadcasted_iota(jnp.int32, (tq, tk), 1)
        keep = cols <= rows
        for h in range(2):
            s = lax.dot_general(q_ref[h], k_ref[h], (((1,), (1,)), ((), ())), preferred_element_type=F32)
            s = jnp.where(keep, s, -jnp.inf)
            m_old = m_scr[h]
            m_new = jnp.maximum(m_old, jnp.max(s, axis=-1, keepdims=True))
            alpha = jnp.exp(m_old - m_new)
            p = jnp.exp(s - m_new)
            l_scr[h] = alpha * l_scr[h] + jnp.sum(p, axis=-1, keepdims=True)
            acc_scr[h] = alpha * acc_scr[h] + jnp.dot(p.astype(BF16), v_ref[:, h * V_DIM:(h + 1) * V_DIM],
                                                      preferred_element_type=F32)
            m_scr[h] = m_new

    @pl.when(ki == qi)
    def _():
        o_ref[...] = jnp.concatenate([acc_scr[h] / l_scr[h] for h in range(2)], axis=-1)


def flash_attention(q, k, v, tq):
    B, H, S, _ = q.shape
    n = S // tq
    return pl.pallas_call(
        _flash_kernel,
        grid=(B, H // 2, n, n),
        in_specs=[pl.BlockSpec((None, 2, tq, MLA_HP), lambda b, h, qi, ki: (b, h, qi, 0)),
                  pl.BlockSpec((None, 2, tq, MLA_HP), lambda b, h, qi, ki: (b, h, jnp.minimum(ki, qi), 0)),
                  pl.BlockSpec((None, tq, 2 * V_DIM), lambda b, h, qi, ki: (b, jnp.minimum(ki, qi), h))],
        out_specs=pl.BlockSpec((None, tq, 2 * V_DIM), lambda b, h, qi, ki: (b, qi, h)),
        out_shape=jax.ShapeDtypeStruct((B, S, H * V_DIM), F32),
        scratch_shapes=[pltpu.VMEM((2, tq, 1), F32), pltpu.VMEM((2, tq, 1), F32), pltpu.VMEM((2, tq, V_DIM), F32)],
        compiler_params=_cparams(("parallel", "parallel", "parallel", "arbitrary")),
        name="flash_attention",
    )(q, k, v)


MLA_QCAT = KV_LORA + MLA_HP


def _mla_sample_kernel(c_ref, qag_ref, wqb_ref, qg_ref, kvg_ref, wkb_ref, kgn_ref, kgr_ref, ind_ref,
                       cos_ref, sup_ref, sdn_ref, q_ref, knew_ref, ckv_ref, ksc_ref):
    qs, ckv, kn, kr, ksc = _mla_common(c_ref, qag_ref, wqb_ref, qg_ref, kvg_ref, wkb_ref, kgn_ref, kgr_ref,
                                       ind_ref, cos_ref, sup_ref, sdn_ref)
    tm = ckv.shape[0]
    rope_lanes = lax.broadcasted_iota(jnp.int32, (tm, MLA_HP), 1) >= NOPE_DIM
    wkb = wkb_ref[...]
    for h in range(MLA_HEADS):
        qh = qs[h] * ATTN_SCALE
        q_lat = lax.dot_general((qh * kgn_ref[...]).astype(BF16), wkb[:, h * MLA_HP:(h + 1) * MLA_HP],
                                (((1,), (1,)), ((), ())), preferred_element_type=F32)
        q_ref[:, h * MLA_QCAT:h * MLA_QCAT + KV_LORA] = q_lat.astype(BF16)
        q_ref[:, h * MLA_QCAT + KV_LORA:(h + 1) * MLA_QCAT] = jnp.where(rope_lanes, qh, 0.0).astype(BF16)
    knew_ref[:, :KV_LORA] = ckv.astype(BF16)
    knew_ref[:, KV_LORA:] = kr.astype(BF16)
    ckv_ref[...] = ckv
    ksc_ref[...] = ksc


def mla_sample_proj(mla_cols, pos, lp, seq, tm):
    T = mla_cols.shape[0]
    w = _mla_weights(lp)
    pos_tok = jnp.tile(pos, tm // seq)
    tables = _rope_tables(pos_tok)
    specs = _mla_in_specs(tm, tm)
    return pl.pallas_call(
        _mla_sample_kernel,
        grid=(T // tm,),
        in_specs=specs,
        out_specs=[_tok_spec(tm, MLA_HEADS * MLA_QCAT), _tok_spec(tm, MLA_QCAT), _tok_spec(tm, KV_LORA),
                   _tok_spec(tm, MLA_HEADS)],
        out_shape=[jax.ShapeDtypeStruct((T, MLA_HEADS * MLA_QCAT), BF16), jax.ShapeDtypeStruct((T, MLA_QCAT), BF16),
                   jax.ShapeDtypeStruct((T, KV_LORA), F32), jax.ShapeDtypeStruct((T, MLA_HEADS), F32)],
        compiler_params=_cparams(("parallel",)),
        name="mla_sample_proj",
    )(mla_cols, w['qag'], w['wqb'], w['qg'], w['kvg'], w['wkb'], w['kgn'], w['kgr'], w['ind'], *tables)


PAGES_PER_STEP = 8


def _paged_attn_kernel(pt_ref, q_ref, knew_ref, kscn_ref, cos_ref, sin_ref, kgr_ref, *rest):
    n = PAGES_PER_STEP
    c_refs, pe_refs, sc_refs = rest[:n], rest[n:2 * n], rest[2 * n:3 * n]
    o_ref, m_scr, l_scr, acc_scr = rest[3 * n:]
    j = pl.program_id(1)
    L = n * PAGE_SIZE
    rows = q_ref.shape[0]
    T = rows // MLA_HEADS
    half = ROPE_DIM // 2
    eye = (lax.broadcasted_iota(jnp.int32, (MLA_HEADS, MLA_HEADS), 0)
           == lax.broadcasted_iota(jnp.int32, (MLA_HEADS, MLA_HEADS), 1)).astype(F32)
    nt = (((1,), (1,)), ((), ()))

    @pl.when(j == 0)
    def _():
        m_scr[...] = jnp.full(m_scr.shape, -jnp.inf, F32)
        l_scr[...] = jnp.zeros(l_scr.shape, F32)
        acc_scr[...] = jnp.zeros(acc_scr.shape, F32)

    def update(s, value_fn):
        m_old = m_scr[...]
        m_new = jnp.maximum(m_old, jnp.max(s, axis=-1, keepdims=True))
        alpha = jnp.exp(m_old - m_new)
        p = jnp.exp(s - m_new)
        l_scr[...] = alpha * l_scr[...] + jnp.sum(p, axis=-1, keepdims=True)
        acc_scr[...] = alpha * acc_scr[...] + value_fn(p)
        m_scr[...] = m_new

    c = jnp.concatenate([r[...] for r in c_refs], axis=0).astype(BF16)
    pe = jnp.concatenate([r[...] for r in pe_refs], axis=0)
    sc = jnp.concatenate([r[...] for r in sc_refs], axis=0)
    base = pl.multiple_of(j * L, L)
    x = pe * kgr_ref[...]
    kr = x * cos_ref[pl.ds(base, L), :] + jnp.concatenate([x[:, half:], x[:, :half]], axis=1) * sin_ref[pl.ds(base, L), :]
    kr = jnp.concatenate([jnp.zeros((L, NOPE_DIM), F32), kr, jnp.zeros((L, MLA_HP - QK_DIM), F32)], axis=1)
    kcat = jnp.concatenate([c, kr.astype(BF16)], axis=1)
    s = lax.dot_general(q_ref[...], kcat, nt, preferred_element_type=F32)
    ksc_t = lax.dot_general(eye, sc, nt, precision=lax.Precision.HIGHEST, preferred_element_type=F32)
    s = (s.reshape(T, MLA_HEADS, L) * ksc_t[None]).reshape(rows, L)
    update(s, lambda p: jnp.dot(p.astype(BF16), c, preferred_element_type=F32))

    @pl.when(j == pl.num_programs(1) - 1)
    def _():
        knew = knew_ref[...]
        s_new = lax.dot_general(q_ref[...], knew, nt, preferred_element_type=F32)
        kscn_t = lax.dot_general(eye, kscn_ref[...], nt, precision=lax.Precision.HIGHEST,
                                 preferred_element_type=F32)
        s_new = (s_new.reshape(T, MLA_HEADS, T) * kscn_t[None]).reshape(rows, T)
        q_tok = lax.broadcasted_iota(jnp.int32, (rows, T), 0) // MLA_HEADS
        s_new = jnp.where(lax.broadcasted_iota(jnp.int32, (rows, T), 1) <= q_tok, s_new, -jnp.inf)
        c_new = knew[:, :KV_LORA].astype(F32)

        def new_values(p):
            pb = p.astype(BF16).astype(F32)
            return sum(pb[:, t:t + 1] * c_new[t:t + 1, :] for t in range(T))

        update(s_new, new_values)
        o_ref[...] = acc_scr[...] / l_scr[...]


def paged_attention(q, knew, ksc_new, cache_ckv, cache_kpe, cache_ksc, page_table, layer, gk_rope):
    B, rows, _ = q.shape
    T = rows // MLA_HEADS
    n_pages = page_table.shape[1]
    n = PAGES_PER_STEP
    past = n_pages * PAGE_SIZE
    half = ROPE_DIM // 2
    inv = jnp.power(ROPE_BASE, -jnp.arange(half, dtype=F32) / half)
    ang = jnp.arange(past)[:, None].astype(F32) * inv
    cos32 = jnp.concatenate([jnp.cos(ang), jnp.cos(ang)], axis=-1)
    sin32 = jnp.concatenate([-jnp.sin(ang), jnp.sin(ang)], axis=-1)

    def page_spec(width, i):
        return pl.BlockSpec((None, None, PAGE_SIZE, width), lambda b, j, pt: (layer, pt[b, j * n + i], 0, 0))

    grid_spec = pltpu.PrefetchScalarGridSpec(
        num_scalar_prefetch=1,
        grid=(B, n_pages // n),
        in_specs=[pl.BlockSpec((None, rows, MLA_QCAT), lambda b, j, pt: (b, 0, 0)),
                  pl.BlockSpec((None, T, MLA_QCAT), lambda b, j, pt: (b, 0, 0)),
                  pl.BlockSpec((None, T, MLA_HEADS), lambda b, j, pt: (b, 0, 0)),
                  pl.BlockSpec((past, ROPE_DIM), lambda b, j, pt: (0, 0)),
                  pl.BlockSpec((past, ROPE_DIM), lambda b, j, pt: (0, 0)),
                  pl.BlockSpec((1, ROPE_DIM), lambda b, j, pt: (0, 0))]
        + [page_spec(KV_LORA, i) for i in range(n)]
        + [page_spec(ROPE_DIM, i) for i in range(n)]
        + [page_spec(MLA_HEADS, i) for i in range(n)],
        out_specs=pl.BlockSpec((None, rows, KV_LORA), lambda b, j, pt: (b, 0, 0)),
        scratch_shapes=[pltpu.VMEM((rows, 1), F32), pltpu.VMEM((rows, 1), F32), pltpu.VMEM((rows, KV_LORA), F32)],
    )
    return pl.pallas_call(
        _paged_attn_kernel,
        grid_spec=grid_spec,
        out_shape=jax.ShapeDtypeStruct((B, rows, KV_LORA), F32),
        compiler_params=_cparams(("parallel", "arbitrary")),
        name="paged_attention",
    )(page_table, q, knew, ksc_new, cos32, sin32, gk_rope[None, :],
      *([cache_ckv] * n), *([cache_kpe] * n), *([cache_ksc] * n))


def _vproj_kernel(o_ref, w_ref, out_ref):
    out_ref[...] = jnp.concatenate(
        [jnp.dot(o_ref[:, h * KV_LORA:(h + 1) * KV_LORA].astype(BF16), w_ref[h], preferred_element_type=F32)
         for h in range(MLA_HEADS)], axis=-1)


def mla_value_proj(o_lat, wvb):
    T = o_lat.shape[0]
    w = wvb.transpose(1, 0, 2).astype(BF16)
    return pl.pallas_call(
        _vproj_kernel,
        grid=(1,),
        in_specs=[_full_spec((T, MLA_HEADS * KV_LORA)), _full_spec((MLA_HEADS, KV_LORA, V_DIM))],
        out_specs=_full_spec((T, MLA_HEADS * V_DIM)),
        out_shape=jax.ShapeDtypeStruct((T, MLA_HEADS * V_DIM), F32),
        compiler_params=_cparams(("arbitrary",)),
        name="mla_value_proj",
    )(o_lat, w)


SSM_BC = SSM_GROUPS * SSM_STATE


def _softplus(x):
    return jnp.maximum(x, 0.0) + jnp.log(1.0 + jnp.exp(-jnp.abs(x)))


def _silu(x):
    return x * jax.nn.sigmoid(x)


def _gated_norm(y, z, g):
    y = y * _silu(z)
    return y * lax.rsqrt(jnp.mean(y * y, axis=-1, keepdims=True) + EPS) * g


def _ssd_prompt_kernel(c_ref, buf_ref, s0_ref, cw_ref, cb_ref, dtb_ref, a_ref, d_ref, ng_ref,
                       o_ref, conv_ref, sT_ref, carry_scr, s_scr):
    ci = pl.program_id(1)
    Lc = c_ref.shape[0]

    @pl.when(ci == 0)
    def _():
        carry_scr[...] = buf_ref[...]
        s_scr[...] = s0_ref[...]

    cols = c_ref[...]
    z = cols[:, :SSM_W]
    xbc = cols[:, SSM_W:SSM_W + CONV_CH]
    dtr = cols[:, SSM_W + CONV_CH:SSM_W + CONV_CH + LANE]
    ext = jnp.concatenate([carry_scr[...], xbc], axis=0)
    conv = cb_ref[...]
    for i in range(CONV_W):
        lo = 8 - (CONV_W - 1) + i
        conv = conv + ext[lo:lo + Lc, :] * cw_ref[i:i + 1, :]
    carry_scr[...] = xbc[Lc - 8:, :]
    conv = _silu(conv)
    xs, Bm, Cm = conv[:, :SSM_W], conv[:, SSM_W:SSM_W + SSM_BC], conv[:, SSM_W + SSM_BC:]
    dt = _softplus(dtr + dtb_ref[...])
    a = dt * a_ref[...]
    ri = lax.broadcasted_iota(jnp.int32, (Lc, Lc), 0)
    cj = lax.broadcasted_iota(jnp.int32, (Lc, Lc), 1)
    causal = cj <= ri
    acs = jnp.dot(causal.astype(F32), a, precision=lax.Precision.HIGHEST, preferred_element_type=F32)
    acs_t = acs.T
    nt = (((1,), (1,)), ((), ()))
    ys = []
    for g in range(SSM_GROUPS):
        Bg = Bm[:, g * SSM_STATE:(g + 1) * SSM_STATE].astype(BF16)
        Cg = Cm[:, g * SSM_STATE:(g + 1) * SSM_STATE].astype(BF16)
        cb = lax.dot_general(Cg, Bg, nt, preferred_element_type=F32)
        for e in range(SSM_EPG):
            hd = g * SSM_EPG + e
            col = acs[:, hd:hd + 1]
            lmat = jnp.exp(jnp.where(causal, col - acs_t[hd:hd + 1, :], -jnp.inf))
            x_e = xs[:, hd * SSM_HD:(hd + 1) * SSM_HD]
            xdt = x_e * dt[:, hd:hd + 1]
            y = jnp.dot((cb * lmat).astype(BF16), xdt.astype(BF16), preferred_element_type=F32)
            last = acs[Lc - 1:Lc, hd:hd + 1]
            s_in = s_scr[hd]
            y = y + lax.dot_general(Cg, s_in.astype(BF16), nt, preferred_element_type=F32) * jnp.exp(col)
            st = lax.dot_general((xdt * jnp.exp(last - col)).astype(BF16), Bg, (((0,), (0,)), ((), ())),
                                 preferred_element_type=F32)
            s_scr[hd] = s_in * jnp.exp(last) + st
            ys.append(y + x_e * d_ref[:, hd:hd + 1])
    o_ref[...] = _gated_norm(jnp.concatenate(ys, axis=-1), z, ng_ref[...])

    @pl.when(ci == pl.num_programs(1) - 1)
    def _():
        conv_ref[...] = xbc[Lc - (CONV_W - 1):, :]
        sT_ref[...] = s_scr[...]


def _ssm_params(lp):
    pad = lambda v: jnp.pad(v, (0, LANE - SSM_HEADS))[None, :]
    return dict(cw=lp['ssm_conv_w'], cb=lp['ssm_conv_b'][None, :], dtb=pad(lp['ssm_dt_bias']),
                a=pad(-jnp.exp(lp['ssm_a_log'])), d=pad(lp['ssm_d']), ng=lp['ssm_norm_g'][None, :])


def ssd_prompt(ssm_cols, conv_buf, s0, lp, B, seq):
    Lc = SSD_CHUNK
    nc = seq // Lc
    p = _ssm_params(lp)
    buf8 = jnp.pad(conv_buf, ((0, 0), (8 - (CONV_W - 1), 0), (0, 0)))
    st_spec = pl.BlockSpec((None, SSM_HEADS, SSM_HD, SSM_STATE), lambda b, c: (b, 0, 0, 0))
    return pl.pallas_call(
        _ssd_prompt_kernel,
        grid=(B, nc),
        in_specs=[pl.BlockSpec((Lc, SSM_PAD), lambda b, c: (b * nc + c, 0)),
                  pl.BlockSpec((None, 8, CONV_CH), lambda b, c: (b, 0, 0)), st_spec,
                  _full_spec((CONV_W, CONV_CH)), _full_spec((1, CONV_CH)), _full_spec((1, LANE)),
                  _full_spec((1, LANE)), _full_spec((1, LANE)), _full_spec((1, SSM_W))],
        out_specs=[pl.BlockSpec((Lc, SSM_W), lambda b, c: (b * nc + c, 0)),
                   pl.BlockSpec((None, CONV_W - 1, CONV_CH), lambda b, c: (b, 0, 0)), st_spec],
        out_shape=[jax.ShapeDtypeStruct((B * seq, SSM_W), F32), jax.ShapeDtypeStruct((B, CONV_W - 1, CONV_CH), F32),
                   jax.ShapeDtypeStruct((B, SSM_HEADS, SSM_HD, SSM_STATE), F32)],
        scratch_shapes=[pltpu.VMEM((8, CONV_CH), F32), pltpu.VMEM((SSM_HEADS, SSM_HD, SSM_STATE), F32)],
        compiler_params=_cparams(("parallel", "arbitrary")),
        name="ssd_prompt",
    )(ssm_cols, buf8, s0, p['cw'], p['cb'], p['dtb'], p['a'], p['d'], p['ng'])


def _ssd_step_prep_kernel(c_ref, x0_ref, x1_ref, x2_ref, cw_ref, cb_ref, dtb_ref, a_ref, exp_ref, expg_ref,
                          w_ref, k_ref, r_ref, v_ref, xs_ref):
    cols = c_ref[...]
    xbc = cols[:, SSM_W:SSM_W + CONV_CH]
    dtr = cols[:, SSM_W + CONV_CH:SSM_W + CONV_CH + LANE]
    conv = cb_ref[...] + x0_ref[...] * cw_ref[0:1, :] + x1_ref[...] * cw_ref[1:2, :] + x2_ref[...] * cw_ref[2:3, :] \
        + xbc * cw_ref[3:4, :]
    conv = _silu(conv)
    xs, Bm, Cm = conv[:, :SSM_W], conv[:, SSM_W:SSM_W + SSM_BC], conv[:, SSM_W + SSM_BC:]
    dt = _softplus(dtr + dtb_ref[...])
    hi = lax.Precision.HIGHEST
    dt_h = jnp.dot(dt, exp_ref[...], precision=hi, preferred_element_type=F32)
    a_h = jnp.dot(dt * a_ref[...], exp_ref[...], precision=hi, preferred_element_type=F32)
    w_ref[...] = jnp.exp(a_h)
    k_ref[...] = jnp.dot(Bm, expg_ref[...], precision=hi, preferred_element_type=F32)
    r_ref[...] = jnp.dot(Cm, expg_ref[...], precision=hi, preferred_element_type=F32)
    v_ref[...] = xs * dt_h
    xs_ref[...] = xs


def _ssd_step_post_kernel(y_ref, xs_ref, c_ref, dh_ref, ng_ref, o_ref):
    y = y_ref[...] + xs_ref[...] * dh_ref[...]
    o_ref[...] = _gated_norm(y, c_ref[:, :SSM_W], ng_ref[...])


def ssd_decode(ssm_cols, conv_buf, s0, lp, B, seq, tm):
    T = B * seq
    p = _ssm_params(lp)
    xbc = ssm_cols[:, SSM_W:SSM_W + CONV_CH].reshape(B, seq, CONV_CH)
    xpad = jnp.concatenate([conv_buf, xbc], axis=1)
    shifted = [xpad[:, i:i + seq].reshape(T, CONV_CH) for i in range(CONV_W - 1)]
    head_of_lane = jnp.arange(SSM_W) // SSM_HD
    expand = (jnp.arange(LANE)[:, None] == head_of_lane[None, :]).astype(F32)
    src = (head_of_lane // SSM_EPG) * SSM_STATE + jnp.arange(SSM_W) % SSM_STATE
    expand_g = (jnp.arange(SSM_BC)[:, None] == src[None, :]).astype(F32)
    out = jax.ShapeDtypeStruct((T, SSM_W), F32)
    w, k, r, v, xs = pl.pallas_call(
        _ssd_step_prep_kernel,
        grid=(T // tm,),
        in_specs=[_tok_spec(tm, SSM_PAD)] + [_tok_spec(tm, CONV_CH)] * 3
        + [_full_spec((CONV_W, CONV_CH)), _full_spec((1, CONV_CH)), _full_spec((1, LANE)), _full_spec((1, LANE)),
           _full_spec((LANE, SSM_W)), _full_spec((SSM_BC, SSM_W))],
        out_specs=[_tok_spec(tm, SSM_W)] * 5,
        out_shape=[out] * 5,
        compiler_params=_cparams(("parallel",)),
        name="ssd_step_prep",
    )(ssm_cols, *shifted, p['cw'], p['cb'], p['dtb'], p['a'], expand, expand_g)
    sh = lambda t: t.reshape(B, seq, SSM_W)
    zeros = jnp.zeros((B, seq, SSM_W), F32)
    y, sT = state_scan(sh(w), zeros, zeros, sh(k), sh(r), sh(v), s0, seq)
    d_h = jnp.repeat(lp['ssm_d'], SSM_HD)[None, :]
    o = pl.pallas_call(
        _ssd_step_post_kernel,
        grid=(T // tm,),
        in_specs=[_tok_spec(tm, SSM_W), _tok_spec(tm, SSM_W), _tok_spec(tm, SSM_PAD), _full_spec((1, SSM_W)),
                  _full_spec((1, SSM_W))],
        out_specs=_tok_spec(tm, SSM_W),
        out_shape=out,
        compiler_params=_cparams(("parallel",)),
        name="ssd_step_post",
    )(y.reshape(T, SSM_W), xs, ssm_cols, d_h, p['ng'])
    return o, xpad[:, -(CONV_W - 1):], sT


def _merge_kernel(oa_ref, ob_ref, oc_ref, gate_ref, x_ref, g1_ref, wb_ref, wo_ref, o_ref):
    D = x_ref.shape[1]
    acc = jnp.zeros(x_ref.shape, F32)
    for i, ref in enumerate((oa_ref, ob_ref, oc_ref)):
        u = jnp.dot(ref[...].astype(BF16), wb_ref[i], preferred_element_type=F32)
        acc = acc + jax.nn.sigmoid(gate_ref[:, i * D:(i + 1) * D]) * u
    y = jnp.dot(acc.astype(BF16), wo_ref[...], preferred_element_type=F32)
    o_ref[...] = x_ref[...] + g1_ref[...] * y


def branch_merge(o_a, o_b, o_c, gates, x2, g1, wb_bf, wo_bf, seq, tm):
    T, D = x2.shape
    return pl.pallas_call(
        _merge_kernel,
        grid=(T // tm,),
        in_specs=[_tok_spec(tm, BR_W)] * 3 + [_tok_spec(tm, N_BRANCH * D), _tok_spec(tm, D), _row_spec(seq, tm, D),
                                              _full_spec((N_BRANCH, BR_W, D)), _full_spec((D, D))],
        out_specs=_tok_spec(tm, D),
        out_shape=jax.ShapeDtypeStruct((T, D), F32),
        compiler_params=_cparams(("parallel",)),
        name="branch_merge",
    )(o_a, o_b, o_c, gates, x2, _row_operand(g1, seq, tm), wb_bf, wo_bf)


TM_TOKENS = 512
TM_INPROJ = 256
TM_ROUTE = 256
SCAN_CHUNK = 32
FLASH_BLOCK = 512


def _layer(x2, B, seq, c, lw, st, pos, layer, decode_ctx):
    T = B * seq
    tm = min(TM_TOKENS, T)
    sh1, sc1, g1, sh2, sc2, g2 = jnp.split(ada_mod(c, lw['ada_w'], lw['ada_b'], layer), 6, axis=-1)
    lp = lw['lp']
    rw, mla, ssm, gates = in_proj(x2, sc1, sh1, lp['norm1_g'], lw['w_in_pad'], seq, min(TM_INPROJ, T))
    o_a, rw_shift, rw_state = rwkv_branch(rw, B, seq, st[0], st[1], lp, tm, min(SCAN_CHUNK, seq))
    if decode_ctx is None:
        q, k, v, ckv, ksc = mla_prompt_proj(mla, pos, lp, B, seq, min(tm, seq))
        o_b = flash_attention(q, k, v, min(FLASH_BLOCK, seq)).reshape(T, BR_W)
        o_c, conv_buf, ssm_state = ssd_prompt(ssm, st[2], st[3], lp, B, seq)
    else:
        cache_ckv, cache_kpe, cache_ksc, page_table = decode_ctx
        q, knew, ckv, ksc = mla_sample_proj(mla, pos, lp, seq, tm)
        o_lat = paged_attention(q.reshape(B, seq * MLA_HEADS, MLA_QCAT), knew.reshape(B, seq, MLA_QCAT),
                                ksc.reshape(B, seq, MLA_HEADS), cache_ckv, cache_kpe, cache_ksc, page_table,
                                layer, lp['mla_k_g'][NOPE_DIM:])
        o_b = mla_value_proj(o_lat.reshape(T, MLA_HEADS * KV_LORA), lp['mla_wvb'])
        o_c, conv_buf, ssm_state = ssd_decode(ssm, st[2], st[3], lp, B, seq, tm)
    x2 = branch_merge(o_a, o_b, o_c, gates, x2, g1, lw['wb_bf'], lw['wo_bf'], seq, tm)
    h, eid, gate = peer_route(x2, sc2, sh2, lp['norm2_g'], lw['wq_bf'], lw['keys_bf'], seq, min(TM_ROUTE, T))
    x2 = peer_gather(eid, h, gate, x2, g2, seq, lw['pk_u'], lw['pk_v'], layer)
    kpe = mla[:, Q_LORA + KV_LORA:MLA_COLS]
    return x2, (ckv.reshape(B, seq, KV_LORA), kpe.reshape(B, seq, ROPE_DIM), ksc.reshape(B, seq, MLA_HEADS),
                rw_shift, rw_state, conv_buf, ssm_state)


def kernel(x_prompt, x_sample, cache_ckv, cache_kpe, cache_kscale, state_rwkv_shift, state_rwkv_wkv,
           state_ssm_conv, state_ssm, page_table, c_prompt, c_sample, ada_w, ada_b, norm1_g, norm2_g, w_in,
           rw_mu, rw_w0, rw_w2, rw_a0, rw_a2, rw_g2, rw_kk, rw_ka, rw_rk, rw_ln_w, rw_ln_b,
           mla_qa_g, mla_wqb, mla_kv_g, mla_wkb, mla_wvb, mla_q_g, mla_k_g,
           ssm_conv_w, ssm_conv_b, ssm_dt_bias, ssm_a_log, ssm_d, ssm_norm_g,
           w_branch, w_out, pk_wq, pk_keys, pk_u, pk_v):
    small = dict(norm1_g=norm1_g, norm2_g=norm2_g,
                 rw_mu=rw_mu, rw_w0=rw_w0, rw_w2=rw_w2, rw_a0=rw_a0, rw_a2=rw_a2, rw_g2=rw_g2,
                 rw_kk=rw_kk, rw_ka=rw_ka, rw_rk=rw_rk, rw_ln_w=rw_ln_w, rw_ln_b=rw_ln_b,
                 mla_qa_g=mla_qa_g, mla_wqb=mla_wqb, mla_kv_g=mla_kv_g, mla_wkb=mla_wkb, mla_wvb=mla_wvb,
                 mla_q_g=mla_q_g, mla_k_g=mla_k_g,
                 ssm_conv_w=ssm_conv_w, ssm_conv_b=ssm_conv_b, ssm_dt_bias=ssm_dt_bias, ssm_a_log=ssm_a_log,
                 ssm_d=ssm_d, ssm_norm_g=ssm_norm_g)
    depth = ada_w.shape[0]
    Bp, S, D = x_prompt.shape
    Bd, T, _ = x_sample.shape
    assert SSD_CHUNK % 8 == 0 and S % SSD_CHUNK == 0 and T < SSD_CHUNK
    past_len = page_table.shape[1] * PAGE_SIZE
    pos_p = jnp.arange(S)
    pos_s = past_len + jnp.arange(T)
    zero_state = (jnp.zeros((Bp, RW_COLS), F32), jnp.zeros((Bp, RW_HEADS, RW_HD, RW_HD), F32),
                  jnp.zeros((Bp, CONV_W - 1, CONV_CH), F32), jnp.zeros((Bp, SSM_HEADS, SSM_HD, SSM_STATE), F32))
    hp, hs = x_prompt.reshape(Bp * S, D), x_sample.reshape(Bd * T, D)
    decode_ctx = (cache_ckv, cache_kpe, cache_kscale, page_table)
    new_p, new_s = [], []
    for l in range(depth):
        lw = dict(lp={name: arr[l] for name, arr in small.items()}, ada_w=ada_w, ada_b=ada_b,
                  w_in_pad=_pad_w_in(w_in[l]), wb_bf=w_branch[l].astype(BF16), wo_bf=w_out[l].astype(BF16),
                  wq_bf=pk_wq[l].astype(BF16),
                  keys_bf=pk_keys[l].reshape(2 * PK_HEADS, PK_NKEYS, PK_DHALF).astype(BF16), pk_u=pk_u, pk_v=pk_v)
        hp, st_p = _layer(hp, Bp, S, c_prompt, lw, zero_state, pos_p, l, None)
        st_in = (state_rwkv_shift[l], state_rwkv_wkv[l], state_ssm_conv[l], state_ssm[l])
        hs, st_s = _layer(hs, Bd, T, c_sample, lw, st_in, pos_s, l, decode_ctx)
        new_p.append(st_p)
        new_s.append(st_s)
    hp, hs = hp.reshape(Bp, S, D), hs.reshape(Bd, T, D)
    return (hp, hs,
            jnp.stack([s[0] for s in new_p]), jnp.stack([s[0] for s in new_s]),
            jnp.stack([s[1] for s in new_p]), jnp.stack([s[1] for s in new_s]),
            jnp.stack([s[2] for s in new_p]), jnp.stack([s[2] for s in new_s]),
            jnp.stack([s[3] for s in new_p]), jnp.stack([s[3] for s in new_s]),
            jnp.stack([s[4] for s in new_p]), jnp.stack([s[4] for s in new_s]),
            jnp.stack([s[5] for s in new_p]), jnp.stack([s[5] for s in new_s]),
            jnp.stack([s[6] for s in new_p]), jnp.stack([s[6] for s in new_s]))
```

```python
import math, functools
import jax, jax.numpy as jnp
from jax import lax
from jax.experimental import pallas as pl
from jax.experimental.pallas import tpu as pltpu

D_MODEL = 1024
PAGE_SIZE = 128
EPS = 1e-6
F32 = jnp.float32
BF16 = jnp.bfloat16
RW_HEADS = 8
RW_HD = 64
RW_W = RW_HEADS * RW_HD
RW_DECAY_LORA = 64
RW_AAA_LORA = 64
RW_GATE_LORA = 128
RW_COLS = 3 * RW_W + RW_DECAY_LORA + RW_AAA_LORA + RW_GATE_LORA
RW_SPLITS = (RW_W, 2 * RW_W, 3 * RW_W, 3 * RW_W + RW_DECAY_LORA, 3 * RW_W + RW_DECAY_LORA + RW_AAA_LORA)
RW_GN_EPS = 64e-5
MLA_HEADS = 8
Q_LORA = 256
KV_LORA = 256
NOPE_DIM = 64
ROPE_DIM = 32
V_DIM = 64
QK_DIM = NOPE_DIM + ROPE_DIM
MLA_COLS = Q_LORA + KV_LORA + ROPE_DIM
ROPE_BASE = 10000.0
ATTN_SCALE = QK_DIM ** -0.5
Q_BLOCK = 128
SSM_HEADS = 8
SSM_HD = 64
SSM_W = SSM_HEADS * SSM_HD
SSM_GROUPS = 2
SSM_EPG = SSM_HEADS // SSM_GROUPS
SSM_STATE = 64
CONV_W = 4
CONV_CH = SSM_W + 2 * SSM_GROUPS * SSM_STATE
SSM_COLS = SSM_W + CONV_CH + SSM_HEADS
SSD_CHUNK = 128
N_BRANCH = 3
BR_W = RW_W
IN_COLS = RW_COLS + MLA_COLS + SSM_COLS + N_BRANCH * D_MODEL
IN_SPLITS = (RW_COLS, RW_COLS + MLA_COLS, RW_COLS + MLA_COLS + SSM_COLS)
PK_HEADS = 8
PK_DKEY = 256
PK_DHALF = PK_DKEY // 2
PK_NKEYS = 128
PK_TOPK = 16
PK_EXPERTS = PK_NKEYS * PK_NKEYS
PK_TOK_BLOCK = 256

LANE = 128
VMEM_LIMIT = 56 * 1024 * 1024

MLA_PAD = 640
SSM_PAD = 1408
PROJ_OFF = (0, RW_COLS, RW_COLS + MLA_PAD, RW_COLS + MLA_PAD + SSM_PAD)
PROJ_COLS = PROJ_OFF[3] + N_BRANCH * D_MODEL


def _cparams(sem):
    return pltpu.CompilerParams(dimension_semantics=sem, vmem_limit_bytes=VMEM_LIMIT)


def _pad_w_in(w_in):
    rw, mla, ssm, gate = jnp.split(w_in, IN_SPLITS, axis=-1)
    mla = jnp.pad(mla, ((0, 0), (0, MLA_PAD - MLA_COLS)))
    ssm = jnp.pad(ssm, ((0, 0), (0, SSM_PAD - SSM_COLS)))
    return jnp.concatenate([rw, mla, ssm, gate], axis=-1).astype(BF16)


PROJ_WIDTHS = (RW_COLS, MLA_PAD, SSM_PAD, N_BRANCH * D_MODEL)


def _inproj_kernel(x_ref, sc_ref, sh_ref, g_ref, w_rw, w_mla, w_ssm, w_gate, o_rw, o_mla, o_ssm, o_gate):
    x = x_ref[...]
    y = x * lax.rsqrt(jnp.mean(x * x, axis=-1, keepdims=True) + EPS) * g_ref[...]
    h = (y * (1.0 + sc_ref[...]) + sh_ref[...]).astype(BF16)
    for w_ref, o_ref in ((w_rw, o_rw), (w_mla, o_mla), (w_ssm, o_ssm), (w_gate, o_gate)):
        o_ref[...] = jnp.dot(h, w_ref[...], preferred_element_type=F32)


def _row_spec(rows_per_group, tm, d):
    if rows_per_group >= tm:
        tiles = rows_per_group // tm
        return pl.BlockSpec((None, 1, d), lambda i, *_: (i // tiles, 0, 0))
    return pl.BlockSpec((None, tm, d), lambda i, *_: (i, 0, 0))


def _row_operand(v, seq, tm):
    if seq >= tm:
        return v[:, None, :]
    return jnp.repeat(v, seq, axis=0).reshape(-1, tm, v.shape[-1])


def in_proj(x2, sc, sh, g, w_pad, seq, tm):
    T, D = x2.shape
    ws = [w_pad[:, PROJ_OFF[i]:PROJ_OFF[i] + PROJ_WIDTHS[i]] for i in range(4)]
    return pl.pallas_call(
        _inproj_kernel,
        grid=(T // tm,),
        in_specs=[_tok_spec(tm, D), _row_spec(seq, tm, D), _row_spec(seq, tm, D), _full_spec((1, D))]
        + [_full_spec((D, n)) for n in PROJ_WIDTHS],
        out_specs=[_tok_spec(tm, n) for n in PROJ_WIDTHS],
        out_shape=[jax.ShapeDtypeStruct((T, n), F32) for n in PROJ_WIDTHS],
        compiler_params=_cparams(("parallel",)),
        name="in_proj",
    )(x2, _row_operand(sc, seq, tm), _row_operand(sh, seq, tm), g[None, :], *ws)


def _seg_sum(x, seg):
    hi = x.astype(BF16)
    lo = (x - hi.astype(F32)).astype(BF16)
    return (jnp.dot(hi, seg, preferred_element_type=F32) + jnp.dot(lo, seg, preferred_element_type=F32))


def _seg_matrix(width, seg):
    i = jnp.arange(width) // seg
    return (i[:, None] == i[None, :]).astype(BF16)


def _gelu_tanh(x):
    return 0.5 * x * (1.0 + jnp.tanh(math.sqrt(2.0 / math.pi) * (x + 0.044715 * (x * x * x))))


def _tok_spec(tm, d, col_block=0):
    return pl.BlockSpec((tm, d), lambda i, *_: (i, col_block))


def _full_spec(shape):
    n = len(shape)
    return pl.BlockSpec(shape, lambda *_: (0,) * n)


def _rwkv_prep_kernel(c_ref, xp_ref, mu_ref, w0_ref, w2_ref, a0_ref, a2_ref, g2_ref, kk_ref, ka_ref, seg_ref,
                      r_ref, w_ref, k_ref, v_ref, a_ref, b_ref, g_ref):
    c = c_ref[...]
    m = c + (xp_ref[...] - c) * mu_ref[...]
    r, k, v = m[:, :RW_W], m[:, RW_W:2 * RW_W], m[:, 2 * RW_W:3 * RW_W]
    wl = m[:, RW_SPLITS[2]:RW_SPLITS[3]]
    al = m[:, RW_SPLITS[3]:RW_SPLITS[4]]
    gl = m[:, RW_SPLITS[4]:]
    nz = -(w0_ref[...] + jnp.dot(jnp.tanh(wl), w2_ref[...], preferred_element_type=F32))
    softplus = jnp.maximum(nz, 0.0) + jnp.log(1.0 + jnp.exp(-jnp.abs(nz)))
    decay = jnp.exp(-jnp.exp(-softplus - 0.5))
    a = jax.nn.sigmoid(a0_ref[...] + jnp.dot(al, a2_ref[...], preferred_element_type=F32))
    g = jnp.dot(jax.nn.sigmoid(gl), g2_ref[...], preferred_element_type=F32)
    kk = k * kk_ref[...]
    k = k * (1.0 + (a - 1.0) * ka_ref[...])
    kk = kk / jnp.maximum(jnp.sqrt(_seg_sum(kk * kk, seg_ref[...])), 1e-12)
    r_ref[...] = r
    w_ref[...] = decay
    k_ref[...] = k
    v_ref[...] = v
    a_ref[...] = -kk
    b_ref[...] = kk * a
    g_ref[...] = g


def rwkv_prep(proj, xprev, lp, tm):
    T = proj.shape[0]
    row = lambda v: v.reshape(1, -1)
    out = jax.ShapeDtypeStruct((T, RW_W), F32)
    return pl.pallas_call(
        _rwkv_prep_kernel,
        grid=(T // tm,),
        in_specs=[_tok_spec(tm, RW_COLS), _tok_spec(tm, RW_COLS), _full_spec((1, RW_COLS)),
                  _full_spec((1, RW_W)), _full_spec((RW_DECAY_LORA, RW_W)),
                  _full_spec((1, RW_W)), _full_spec((RW_AAA_LORA, RW_W)),
                  _full_spec((RW_GATE_LORA, RW_W)), _full_spec((1, RW_W)), _full_spec((1, RW_W)),
                  _full_spec((RW_W, RW_W))],
        out_specs=[_tok_spec(tm, RW_W)] * 7,
        out_shape=[out] * 7,
        compiler_params=_cparams(("parallel",)),
        name="rwkv_prep",
    )(proj, xprev, row(lp['rw_mu']), row(lp['rw_w0']), lp['rw_w2'], row(lp['rw_a0']), lp['rw_a2'], lp['rw_g2'],
      row(lp['rw_kk']), row(lp['rw_ka']), _seg_matrix(RW_W, RW_HD))


SCAN_HEADS = 32
SCAN_VM = LANE // SCAN_HEADS
SCAN_VD = RW_HD // SCAN_VM


def _scan_kernel(w_ref, a_ref, b_ref, k_ref, r_ref, v_ref, s0_ref, y_ref, sT_ref, s_scr):
    c = pl.program_id(1)

    @pl.when(c == 0)
    def _():
        s_scr[...] = s0_ref[...]

    def step(t, carry):
        A, W, Bm, K, R = a_ref[t], w_ref[t], b_ref[t], k_ref[t], r_ref[t]
        V = v_ref[t]
        ys = []
        for vd in range(SCAN_VD):
            S = s_scr[vd]
            sa = jnp.sum(S * A, axis=0, keepdims=True)
            S = S * W + sa * Bm + V[vd:vd + 1, :] * K
            s_scr[vd] = S
            ys.append(jnp.sum(S * R, axis=0, keepdims=True))
        y_ref[t] = jnp.concatenate(ys, axis=0)
        return carry

    lax.fori_loop(0, y_ref.shape[0], step, 0)

    @pl.when(c == pl.num_programs(1) - 1)
    def _():
        sT_ref[...] = s_scr[...]


def _to_scan_kh(x, n_groups):
    B, T, _ = x.shape
    x = x.reshape(B, T, RW_HEADS, RW_HD).transpose(1, 3, 0, 2).reshape(T, RW_HD, n_groups, SCAN_HEADS)
    x = jnp.broadcast_to(x[..., None], (T, RW_HD, n_groups, SCAN_HEADS, SCAN_VM))
    return x.transpose(2, 0, 1, 3, 4).reshape(n_groups, T, RW_HD, LANE)


def _to_scan_v(x, n_groups):
    B, T, _ = x.shape
    x = x.reshape(B, T, RW_HEADS, SCAN_VD, SCAN_VM).transpose(1, 3, 0, 2, 4)
    x = x.reshape(T, SCAN_VD, n_groups, SCAN_HEADS * SCAN_VM)
    return x.transpose(2, 0, 1, 3)


def _from_scan_v(y, B):
    G, T = y.shape[:2]
    y = y.transpose(1, 2, 0, 3).reshape(T, SCAN_VD, B, RW_HEADS, SCAN_VM)
    return y.transpose(2, 0, 3, 1, 4).reshape(B, T, RW_W)


def _state_to_scan(s, n_groups):
    B = s.shape[0]
    s = s.reshape(B, RW_HEADS, SCAN_VD, SCAN_VM, RW_HD).transpose(2, 4, 0, 1, 3)
    s = s.reshape(SCAN_VD, RW_HD, n_groups, SCAN_HEADS * SCAN_VM)
    return s.transpose(2, 0, 1, 3)


def _state_from_scan(s, B):
    s = s.transpose(1, 2, 0, 3).reshape(SCAN_VD, RW_HD, B, RW_HEADS, SCAN_VM)
    return s.transpose(2, 3, 0, 4, 1).reshape(B, RW_HEADS, RW_HD, RW_HD)


def state_scan(w, a, b, k, r, v, s0, tc):
    B, T, _ = w.shape
    G = (B * RW_HEADS) // SCAN_HEADS
    kh = [_to_scan_kh(t, G) for t in (w, a, b, k, r)]
    vv = _to_scan_v(v, G)
    ss = _state_to_scan(s0, G)
    kh_spec = pl.BlockSpec((None, tc, RW_HD, LANE), lambda g, c: (g, c, 0, 0))
    v_spec = pl.BlockSpec((None, tc, SCAN_VD, LANE), lambda g, c: (g, c, 0, 0))
    s_spec = pl.BlockSpec((None, SCAN_VD, RW_HD, LANE), lambda g, c: (g, 0, 0, 0))
    y, sT = pl.pallas_call(
        _scan_kernel,
        grid=(G, T // tc),
        in_specs=[kh_spec] * 5 + [v_spec, s_spec],
        out_specs=[v_spec, s_spec],
        out_shape=[jax.ShapeDtypeStruct((G, T, SCAN_VD, LANE), F32),
                   jax.ShapeDtypeStruct((G, SCAN_VD, RW_HD, LANE), F32)],
        scratch_shapes=[pltpu.VMEM((SCAN_VD, RW_HD, LANE), F32)],
        compiler_params=_cparams(("parallel", "arbitrary")),
        name="state_scan",
    )(*kh, vv, ss)
    return _from_scan_v(y, B), _state_from_scan(sT, B)


def _rwkv_post_kernel(y_ref, r_ref, k_ref, v_ref, g_ref, lnw_ref, lnb_ref, rk_ref, seg_ref, o_ref):
    seg = seg_ref[...]
    y = y_ref[...]
    d = y - _seg_sum(y, seg) * (1.0 / RW_HD)
    var = _seg_sum(d * d, seg) * (1.0 / RW_HD)
    yn = d * lax.rsqrt(var + RW_GN_EPS) * lnw_ref[...] + lnb_ref[...]
    bonus = _seg_sum(r_ref[...] * k_ref[...] * rk_ref[...], seg) * v_ref[...]
    o_ref[...] = (yn + bonus) * g_ref[...]


def rwkv_post(y, r, k, v, g, lp, tm):
    T = y.shape[0]
    row = lambda t: t.reshape(1, -1)
    return pl.pallas_call(
        _rwkv_post_kernel,
        grid=(T // tm,),
        in_specs=[_tok_spec(tm, RW_W)] * 5 + [_full_spec((1, RW_W))] * 3 + [_full_spec((RW_W, RW_W))],
        out_specs=_tok_spec(tm, RW_W),
        out_shape=jax.ShapeDtypeStruct((T, RW_W), F32),
        compiler_params=_cparams(("parallel",)),
        name="rwkv_post",
    )(y, r, k, v, g, row(lp['rw_ln_w']), row(lp['rw_ln_b']), row(lp['rw_rk']), _seg_matrix(RW_W, RW_HD))


def rwkv_branch(rw2, B, T, prev_row, s0, lp, tm, tc):
    rw_cols = rw2.reshape(B, T, RW_COLS)
    xprev = jnp.concatenate([prev_row[:, None, :], rw_cols[:, :-1]], axis=1).reshape(B * T, RW_COLS)
    r, w, k, v, a, b, g = rwkv_prep(rw2, xprev, lp, tm)
    sh = lambda t: t.reshape(B, T, RW_W)
    y, sT = state_scan(sh(w), sh(a), sh(b), sh(k), sh(r), sh(v), s0, tc)
    o = rwkv_post(y.reshape(B * T, RW_W), r, k, v, g, lp, tm)
    return o, rw_cols[:, -1], sT


def _topk_rows(s, rows, k, payload=None):
    n = s.shape[0]
    vals, outs = [], []
    for _ in range(k):
        m = jnp.max(s, axis=0, keepdims=True)
        pos = jnp.min(jnp.where(s == m, rows, n), axis=0, keepdims=True)
        sel = rows == pos
        vals.append(m)
        outs.append(pos if payload is None else jnp.sum(jnp.where(sel, payload, 0), axis=0, keepdims=True))
        s = jnp.where(sel, -jnp.inf, s)
    return jnp.concatenate(vals, axis=0), jnp.concatenate(outs, axis=0)


_PAIR_BLOCKS = ((0, 1, PK_TOPK, PK_TOPK),) + tuple((a, 1, 8, PK_TOPK // (a + 1)) for a in range(1, 8)) + ((8, 8, 1, 8),)
_PAIR_ROWS = sum(max(na, nb) for _, na, nb, _ in _PAIR_BLOCKS)


def _peer_route_kernel(x_ref, sc_ref, sh_ref, g_ref, wq_ref, keys_ref, h_ref, eid_ref, gate_ref):
    x = x_ref[...]
    tm = x.shape[0]
    y = x * lax.rsqrt(jnp.mean(x * x, axis=-1, keepdims=True) + EPS) * g_ref[...]
    h = y * (1.0 + sc_ref[...]) + sh_ref[...]
    h_ref[...] = h
    q = jnp.dot(h.astype(BF16), wq_ref[...], preferred_element_type=F32)
    rows_k = lax.broadcasted_iota(jnp.int32, (PK_NKEYS, tm), 0)
    rows_c = lax.broadcasted_iota(jnp.int32, (_PAIR_ROWS, tm), 0)
    eids, gates = [], []
    for hh in range(PK_HEADS):
        tops = []
        for i in range(2):
            c0 = (hh * 2 + i) * PK_DHALF
            s = lax.dot_general(keys_ref[hh * 2 + i], q[:, c0:c0 + PK_DHALF].astype(BF16),
                                (((1,), (1,)), ((), ())), preferred_element_type=F32)
            tops.append(_topk_rows(s, rows_k, PK_TOPK))
        (s0, i0), (s1, i1) = tops
        cs, ci = [], []
        for a0, na, nb, keep_n in _PAIR_BLOCKS:
            blk = s0[a0:a0 + na] + s1[0:nb]
            if keep_n < blk.shape[0]:
                blk = jnp.where(lax.broadcasted_iota(jnp.int32, blk.shape, 0) < keep_n, blk, -jnp.inf)
            cs.append(blk)
            ci.append(i0[a0:a0 + na] * PK_NKEYS + i1[0:nb])
        cand = jnp.concatenate(cs, axis=0)
        cid = jnp.concatenate(ci, axis=0)
        best, eid = _topk_rows(cand, rows_c, PK_TOPK, payload=cid)
        e = jnp.exp(best - best[0:1])
        gates.append(e / jnp.sum(e, axis=0, keepdims=True))
        eids.append(eid)
    eid_ref[...] = jnp.concatenate(eids, axis=0).T
    gate_ref[...] = jnp.concatenate(gates, axis=0).T


def peer_route(x2, sc, sh, norm_g, wq_bf, keys_bf, seq, tm):
    T, D = x2.shape
    nq = PK_HEADS * PK_DKEY
    nsel = PK_HEADS * PK_TOPK
    return pl.pallas_call(
        _peer_route_kernel,
        grid=(T // tm,),
        in_specs=[_tok_spec(tm, D), _row_spec(seq, tm, D), _row_spec(seq, tm, D), _full_spec((1, D)),
                  _full_spec((D, nq)), _full_spec((2 * PK_HEADS, PK_NKEYS, PK_DHALF))],
        out_specs=[_tok_spec(tm, D), _tok_spec(tm, nsel), _tok_spec(tm, nsel)],
        out_shape=[jax.ShapeDtypeStruct((T, D), F32), jax.ShapeDtypeStruct((T, nsel), jnp.int32),
                   jax.ShapeDtypeStruct((T, nsel), F32)],
        compiler_params=_cparams(("parallel",)),
        name="peer_route",
    )(x2, _row_operand(sc, seq, tm), _row_operand(sh, seq, tm), norm_g[None, :], wq_bf, keys_bf)


PEER_TOK = 128
PEER_SLOTS = 4
PEER_SEL = PK_HEADS * PK_TOPK


PEER_ROWS = 2 * D_MODEL // LANE


def _fold_pairs(x, shift):
    n = x.shape[0] // 2
    x = x.reshape(n, 2, 8, LANE)
    a, b = x[:, 0], x[:, 1]
    low = (lax.broadcasted_iota(jnp.int32, (n, 8, LANE), 1) & shift) == 0
    return jnp.where(low, a + pltpu.roll(a, 8 - shift, 1), b + pltpu.roll(b, shift, 1))


def _fold_order(r):
    s = r & 7
    return (r & ~7) | ((s & 1) << 2) | (s & 2) | ((s >> 2) & 1)


def _peer_gather_kernel(eid_ref, h_ref, gate_ref, x_ref, g2_ref, uv_hbm, o_ref,
                        eid_smem, w_scr, buf0, buf1, buf2, buf3, sem_idx, sem):
    bufs = (buf0, buf1, buf2, buf3)
    idx_copy = pltpu.make_async_copy(eid_ref, eid_smem, sem_idx)
    idx_copy.start()
    idx_copy.wait()
    ns = PEER_SLOTS
    n_tok = gate_ref.shape[0]
    n_groups = n_tok // ns
    rows = PEER_ROWS

    def issue(t, slot):
        base = t * PEER_SEL
        for kk in range(PEER_SEL):
            src = pl.multiple_of(eid_smem[base + kk] * rows, rows)
            pltpu.make_async_copy(uv_hbm.at[pl.ds(src, rows)], bufs[slot].at[pl.ds(kk * rows, rows)],
                                  sem.at[slot]).start()

    def wait(slot):
        pltpu.make_async_copy(uv_hbm.at[pl.ds(0, PEER_SEL * rows)], bufs[slot], sem.at[slot]).wait()

    rr = lax.broadcasted_iota(jnp.int32, (PEER_SEL, PEER_SEL), 0)
    pick = _fold_order(rr) == lax.broadcasted_iota(jnp.int32, (PEER_SEL, PEER_SEL), 1)

    def compute(t, slot):
        buf = bufs[slot]
        t8 = pl.multiple_of(t * 8, 8)
        h = h_ref[pl.ds(t8, 8), :]
        u = buf[...].reshape(PEER_SEL, rows, LANE)[:, :8, :]
        part = _fold_pairs(_fold_pairs(_fold_pairs(u * h[None], 4), 2), 1)
        act = jnp.sum(part.reshape(PEER_SEL, LANE), axis=-1, keepdims=True)
        gate = jnp.sum(jnp.where(pick, gate_ref[pl.ds(t, 1), :], 0.0), axis=-1, keepdims=True)
        w_scr[...] = jnp.broadcast_to(gate * _gelu_tanh(act), (PEER_SEL, LANE))
        acc = [jnp.zeros((8, LANE), F32) for _ in range(4)]
        for r in range(PEER_SEL):
            kk = _fold_order(r)
            v = buf[kk * rows + 8:(kk + 1) * rows, :]
            acc[r % 4] = acc[r % 4] + jnp.broadcast_to(w_scr[r:r + 1, :], (8, LANE)) * v
        y = (acc[0] + acc[1]) + (acc[2] + acc[3])
        g2 = g2_ref[...] if g2_ref.shape[0] == 8 else g2_ref[pl.ds(t8, 8), :]
        o_ref[pl.ds(t8, 8), :] = x_ref[pl.ds(t8, 8), :] + g2 * y

    for j in range(ns - 1):
        issue(j, j)

    def group(gi, carry):
        for j in range(ns):
            t = gi * ns + j
            wait(j)
            issue(t + ns - 1, (j + ns - 1) % ns)
            compute(t, j)
        return carry

    lax.fori_loop(0, n_groups - 1, group, 0)
    t0 = (n_groups - 1) * ns
    for j in range(ns):
        wait(j)
        if j == 0:
            issue(t0 + ns - 1, ns - 1)
        compute(t0 + j, j)


def peer_table(pk_u, pk_v):
    E, D = pk_u.shape
    t = jnp.concatenate([pk_u.reshape(E, D // LANE, LANE), pk_v.reshape(E, D // LANE, LANE)], axis=1)
    return t.reshape(E * PEER_ROWS, LANE)


def peer_gather(eid, h, gate, x2, g2, seq, uv):
    T, D = h.shape
    tb = min(PEER_TOK, T)
    sub = D // LANE
    assert T % tb == 0 and tb % PEER_SLOTS == 0 and tb // PEER_SLOTS >= 2 and sub == 8 and PEER_SLOTS == 4
    tiles = lambda a: a.reshape(-1, LANE)
    if seq >= tb:
        per_seq = seq // tb
        g2_op, g2_spec = tiles(g2), pl.BlockSpec((sub, LANE), lambda i: (i // per_seq, 0))
    else:
        g2_op, g2_spec = tiles(jnp.repeat(g2, seq, axis=0)), _tok_spec(tb * sub, LANE)
    out = pl.pallas_call(
        _peer_gather_kernel,
        grid=(T // tb,),
        in_specs=[pl.BlockSpec((tb * PEER_SEL,), lambda i: (i,)), _tok_spec(tb * sub, LANE), _tok_spec(tb, PEER_SEL),
                  _tok_spec(tb * sub, LANE), g2_spec, pl.BlockSpec(memory_space=pl.ANY)],
        out_specs=_tok_spec(tb * sub, LANE),
        out_shape=jax.ShapeDtypeStruct((T * sub, LANE), F32),
        scratch_shapes=[pltpu.SMEM((tb * PEER_SEL,), jnp.int32), pltpu.VMEM((PEER_SEL, LANE), F32)]
        + [pltpu.VMEM((PEER_SEL * PEER_ROWS, LANE), F32)] * PEER_SLOTS
        + [pltpu.SemaphoreType.DMA, pltpu.SemaphoreType.DMA((PEER_SLOTS,))],
        compiler_params=_cparams(("arbitrary",)),
        name="peer_gather",
    )(eid.reshape(T * PEER_SEL), tiles(h), gate, tiles(x2), g2_op, uv)
    return out.reshape(T, D)


def _ada_kernel(c_ref, w_ref, b_ref, o_ref):
    c = c_ref[...]
    o_ref[...] = jnp.dot(c * jax.nn.sigmoid(c), w_ref[...], preferred_element_type=F32) + b_ref[...]


def ada_mod(c, ada_w, ada_b, layer):
    B, D = c.shape
    N = ada_w.shape[2]
    tn = D
    return pl.pallas_call(
        _ada_kernel,
        grid=(N // tn,),
        in_specs=[_full_spec((B, D)), pl.BlockSpec((None, D, tn), lambda j: (layer, 0, j)),
                  pl.BlockSpec((None, 1, tn), lambda j: (layer, 0, j))],
        out_specs=pl.BlockSpec((B, tn), lambda j: (0, j)),
        out_shape=jax.ShapeDtypeStruct((B, N), F32),
        compiler_params=_cparams(("parallel",)),
        name="ada_mod",
    )(c, ada_w, ada_b[:, None, :])


MLA_HP = LANE


def _rope_tables(pos):
    half = ROPE_DIM // 2
    inv = jnp.power(ROPE_BASE, -jnp.arange(half, dtype=F32) / half)
    ang = pos[:, None].astype(F32) * inv
    cos, sin = jnp.cos(ang), jnp.sin(ang)
    n = pos.shape[0]
    one, zero = jnp.ones((n, NOPE_DIM), F32), jnp.zeros((n, NOPE_DIM), F32)
    tail = jnp.zeros((n, MLA_HP - QK_DIM), F32)
    c = jnp.concatenate([one, cos, cos, tail], axis=-1)
    s_up = jnp.concatenate([zero, jnp.zeros_like(sin), sin, tail], axis=-1)
    s_dn = jnp.concatenate([zero, -sin, jnp.zeros_like(sin), tail], axis=-1)
    return c, s_up, s_dn


def _rope_apply(x, c, s_up, s_dn):
    half = ROPE_DIM // 2
    return x * c + pltpu.roll(x, half, 1) * s_up + pltpu.roll(x, MLA_HP - half, 1) * s_dn


def _head_ind(width, seg, n):
    return (jnp.arange(width)[:, None] // seg == jnp.arange(n)[None, :]).astype(BF16)


def _mla_common(c_ref, qag_ref, wqb_ref, qg_ref, kvg_ref, wkb_ref, kgn_ref, kgr_ref, ind_ref,
                cos_ref, sup_ref, sdn_ref):
    c = c_ref[...]
    cos, sup, sdn = cos_ref[...], sup_ref[...], sdn_ref[...]
    qa, ckv, kpe = c[:, :Q_LORA], c[:, Q_LORA:Q_LORA + KV_LORA], c[:, Q_LORA + KV_LORA:Q_LORA + KV_LORA + LANE]
    qa = qa * lax.rsqrt(jnp.mean(qa * qa, axis=-1, keepdims=True) + EPS) * qag_ref[...]
    q = jnp.dot(qa.astype(BF16), wqb_ref[...], preferred_element_type=F32)
    ind = ind_ref[...]
    qss = _seg_sum(q * q, ind) * (1.0 / QK_DIM)
    qs = []
    for h in range(MLA_HEADS):
        qh = q[:, h * MLA_HP:(h + 1) * MLA_HP] * lax.rsqrt(qss[:, h:h + 1] + EPS) * qg_ref[...]
        qs.append(_rope_apply(qh, cos, sup, sdn))
    ckv = ckv * lax.rsqrt(jnp.mean(ckv * ckv, axis=-1, keepdims=True) + EPS) * kvg_ref[...]
    kn = jnp.dot(ckv.astype(BF16), wkb_ref[...], preferred_element_type=F32)
    pe2 = jnp.sum(kpe * kpe, axis=-1, keepdims=True)
    ksc = lax.rsqrt((_seg_sum(kn * kn, ind) + pe2) * (1.0 / QK_DIM) + EPS)
    kr = _rope_apply(pltpu.roll(kpe, NOPE_DIM, 1) * kgr_ref[...], cos, sup, sdn)
    return qs, ckv, kn, kr, ksc


def _mla_prompt_kernel(c_ref, qag_ref, wqb_ref, qg_ref, kvg_ref, wkb_ref, kgn_ref, kgr_ref, ind_ref,
                       cos_ref, sup_ref, sdn_ref, wvb_ref, q_ref, k_ref, v_ref, ckv_ref, ksc_ref):
    qs, ckv, kn, kr, ksc = _mla_common(c_ref, qag_ref, wqb_ref, qg_ref, kvg_ref, wkb_ref, kgn_ref, kgr_ref,
                                       ind_ref, cos_ref, sup_ref, sdn_ref)
    for h in range(MLA_HEADS):
        q_ref[h] = (qs[h] * ATTN_SCALE).astype(BF16)
        kh = (kn[:, h * MLA_HP:(h + 1) * MLA_HP] * kgn_ref[...] + kr) * ksc[:, h:h + 1]
        k_ref[h] = kh.astype(BF16)
    v_ref[...] = jnp.dot(ckv.astype(BF16), wvb_ref[...], preferred_element_type=F32).astype(BF16)
    ckv_ref[...] = ckv
    ksc_ref[...] = ksc


def _mla_weights(lp):
    pad_h = lambda w, d: jnp.pad(w.reshape(w.shape[0], MLA_HEADS, d), ((0, 0), (0, 0), (0, MLA_HP - d))
                                 ).reshape(w.shape[0], MLA_HEADS * MLA_HP).astype(BF16)
    gk = lp['mla_k_g']
    return dict(
        qag=lp['mla_qa_g'][None, :], wqb=pad_h(lp['mla_wqb'], QK_DIM),
        qg=jnp.pad(lp['mla_q_g'], (0, MLA_HP - QK_DIM))[None, :], kvg=lp['mla_kv_g'][None, :],
        wkb=pad_h(lp['mla_wkb'].reshape(KV_LORA, -1), NOPE_DIM),
        kgn=jnp.pad(gk[:NOPE_DIM], (0, MLA_HP - NOPE_DIM))[None, :],
        kgr=jnp.pad(gk[NOPE_DIM:], (NOPE_DIM, MLA_HP - QK_DIM))[None, :],
        ind=_head_ind(MLA_HEADS * MLA_HP, MLA_HP, MLA_HEADS),
        wvb=lp['mla_wvb'].reshape(KV_LORA, MLA_HEADS * V_DIM).astype(BF16))


def _mla_in_specs(tm, seq):
    tiles = max(seq // tm, 1)
    tab = pl.BlockSpec((tm, MLA_HP), lambda i: (i % tiles, 0))
    return [_tok_spec(tm, MLA_PAD), _full_spec((1, Q_LORA)), _full_spec((Q_LORA, MLA_HEADS * MLA_HP)),
            _full_spec((1, MLA_HP)), _full_spec((1, KV_LORA)), _full_spec((KV_LORA, MLA_HEADS * MLA_HP)),
            _full_spec((1, MLA_HP)), _full_spec((1, MLA_HP)), _full_spec((MLA_HEADS * MLA_HP, MLA_HEADS)),
            tab, tab, tab]


def mla_prompt_proj(mla_cols, pos, lp, B, seq, tm):
    T = B * seq
    w = _mla_weights(lp)
    tiles = seq // tm
    hspec = pl.BlockSpec((None, MLA_HEADS, tm, MLA_HP), lambda i: (i // tiles, 0, i % tiles, 0))
    q, k, v, ckv, ksc = pl.pallas_call(
        _mla_prompt_kernel,
        grid=(T // tm,),
        in_specs=_mla_in_specs(tm, seq) + [_full_spec((KV_LORA, MLA_HEADS * V_DIM))],
        out_specs=[hspec, hspec, _tok_spec(tm, MLA_HEADS * V_DIM), _tok_spec(tm, KV_LORA), _tok_spec(tm, MLA_HEADS)],
        out_shape=[jax.ShapeDtypeStruct((B, MLA_HEADS, seq, MLA_HP), BF16)] * 2
        + [jax.ShapeDtypeStruct((T, MLA_HEADS * V_DIM), BF16), jax.ShapeDtypeStruct((T, KV_LORA), F32),
           jax.ShapeDtypeStruct((T, MLA_HEADS), F32)],
        compiler_params=_cparams(("parallel",)),
        name="mla_prompt_proj",
    )(mla_cols, w['qag'], w['wqb'], w['qg'], w['kvg'], w['wkb'], w['kgn'], w['kgr'], w['ind'],
      *_rope_tables(pos), w['wvb'])
    return q, k, v.reshape(B, seq, -1), ckv, ksc


def _flash_kernel(q_ref, k_ref, v_ref, o_ref, m_scr, l_scr, acc_scr):
    qi, ki = pl.program_id(2), pl.program_id(3)
    tq, tk = q_ref.shape[1], k_ref.shape[1]

    @pl.when(ki == 0)
    def _():
        m_scr[...] = jnp.full(m_scr.shape, -jnp.inf, F32)
        l_scr[...] = jnp.zeros(l_scr.shape, F32)
        acc_scr[...] = jnp.zeros(acc_scr.shape, F32)

    def block(on_diagonal):
        for h in range(2):
            s = lax.dot_general(q_ref[h], k_ref[h], (((1,), (1,)), ((), ())), preferred_element_type=F32)
            if on_diagonal:
                keep = lax.broadcasted_iota(jnp.int32, (tq, tk), 1) <= lax.broadcasted_iota(jnp.int32, (tq, tk), 0)
                s = jnp.where(keep, s, -jnp.inf)
            m_old = m_scr[h]
            m_new = jnp.maximum(m_old, jnp.max(s, axis=-1, keepdims=True))
            alpha = jnp.exp(m_old - m_new)
            p = jnp.exp(s - m_new)
            l_scr[h] = alpha * l_scr[h] + jnp.sum(p, axis=-1, keepdims=True)
            acc_scr[h] = alpha * acc_scr[h] + jnp.dot(p.astype(BF16), v_ref[:, h * V_DIM:(h + 1) * V_DIM],
                                                      preferred_element_type=F32)
            m_scr[h] = m_new

    @pl.when(ki < qi)
    def _():
        block(False)

    @pl.when(ki == qi)
    def _():
        block(True)
        o_ref[...] = jnp.concatenate([acc_scr[h] / l_scr[h] for h in range(2)], axis=-1)


def flash_attention(q, k, v, tq):
    B, H, S, _ = q.shape
    n = S // tq
    return pl.pallas_call(
        _flash_kernel,
        grid=(B, H // 2, n, n),
        in_specs=[pl.BlockSpec((None, 2, tq, MLA_HP), lambda b, h, qi, ki: (b, h, qi, 0)),
                  pl.BlockSpec((None, 2, tq, MLA_HP), lambda b, h, qi, ki: (b, h, jnp.minimum(ki, qi), 0)),
                  pl.BlockSpec((None, tq, 2 * V_DIM), lambda b, h, qi, ki: (b, jnp.minimum(ki, qi), h))],
        out_specs=pl.BlockSpec((None, tq, 2 * V_DIM), lambda b, h, qi, ki: (b, qi, h)),
        out_shape=jax.ShapeDtypeStruct((B, S, H * V_DIM), F32),
        scratch_shapes=[pltpu.VMEM((2, tq, 1), F32), pltpu.VMEM((2, tq, 1), F32), pltpu.VMEM((2, tq, V_DIM), F32)],
        compiler_params=_cparams(("parallel", "parallel", "parallel", "arbitrary")),
        name="flash_attention",
    )(q, k, v)


MLA_QCAT = KV_LORA + MLA_HP


def _mla_sample_kernel(c_ref, qag_ref, wqb_ref, qg_ref, kvg_ref, wkb_ref, kgn_ref, kgr_ref, ind_ref,
                       cos_ref, sup_ref, sdn_ref, q_ref, knew_ref, ckv_ref, ksc_ref):
    qs, ckv, kn, kr, ksc = _mla_common(c_ref, qag_ref, wqb_ref, qg_ref, kvg_ref, wkb_ref, kgn_ref, kgr_ref,
                                       ind_ref, cos_ref, sup_ref, sdn_ref)
    tm = ckv.shape[0]
    rope_lanes = lax.broadcasted_iota(jnp.int32, (tm, MLA_HP), 1) >= NOPE_DIM
    wkb = wkb_ref[...]
    for h in range(MLA_HEADS):
        qh = qs[h] * ATTN_SCALE
        q_lat = lax.dot_general((qh * kgn_ref[...]).astype(BF16), wkb[:, h * MLA_HP:(h + 1) * MLA_HP],
                                (((1,), (1,)), ((), ())), preferred_element_type=F32)
        q_ref[:, h * MLA_QCAT:h * MLA_QCAT + KV_LORA] = q_lat.astype(BF16)
        q_ref[:, h * MLA_QCAT + KV_LORA:(h + 1) * MLA_QCAT] = jnp.where(rope_lanes, qh, 0.0).astype(BF16)
    knew_ref[:, :KV_LORA] = ckv.astype(BF16)
    knew_ref[:, KV_LORA:] = kr.astype(BF16)
    ckv_ref[...] = ckv
    ksc_ref[...] = ksc


def mla_sample_proj(mla_cols, pos, lp, seq, tm):
    T = mla_cols.shape[0]
    w = _mla_weights(lp)
    pos_tok = jnp.tile(pos, tm // seq)
    tables = _rope_tables(pos_tok)
    specs = _mla_in_specs(tm, tm)
    return pl.pallas_call(
        _mla_sample_kernel,
        grid=(T // tm,),
        in_specs=specs,
        out_specs=[_tok_spec(tm, MLA_HEADS * MLA_QCAT), _tok_spec(tm, MLA_QCAT), _tok_spec(tm, KV_LORA),
                   _tok_spec(tm, MLA_HEADS)],
        out_shape=[jax.ShapeDtypeStruct((T, MLA_HEADS * MLA_QCAT), BF16), jax.ShapeDtypeStruct((T, MLA_QCAT), BF16),
                   jax.ShapeDtypeStruct((T, KV_LORA), F32), jax.ShapeDtypeStruct((T, MLA_HEADS), F32)],
        compiler_params=_cparams(("parallel",)),
        name="mla_sample_proj",
    )(mla_cols, w['qag'], w['wqb'], w['qg'], w['kvg'], w['wkb'], w['kgn'], w['kgr'], w['ind'], *tables)


PAGES_PER_STEP = 16


def _paged_attn_kernel(pt_ref, q_ref, knew_ref, kscn_ref, cos_ref, sin_ref, kgr_ref, *rest):
    n = PAGES_PER_STEP
    c_refs, pe_refs, sc_refs = rest[:n], rest[n:2 * n], rest[2 * n:3 * n]
    o_ref, m_scr, l_scr, acc_scr = rest[3 * n:]
    j = pl.program_id(1)
    L = n * PAGE_SIZE
    rows = q_ref.shape[0]
    T = rows // MLA_HEADS
    half = ROPE_DIM // 2
    eye = (lax.broadcasted_iota(jnp.int32, (MLA_HEADS, MLA_HEADS), 0)
           == lax.broadcasted_iota(jnp.int32, (MLA_HEADS, MLA_HEADS), 1)).astype(F32)
    nt = (((1,), (1,)), ((), ()))

    @pl.when(j == 0)
    def _():
        m_scr[...] = jnp.full(m_scr.shape, -jnp.inf, F32)
        l_scr[...] = jnp.zeros(l_scr.shape, F32)
        acc_scr[...] = jnp.zeros(acc_scr.shape, F32)

    def update(s, value_fn):
        m_old = m_scr[...]
        m_new = jnp.maximum(m_old, jnp.max(s, axis=-1, keepdims=True))
        alpha = jnp.exp(m_old - m_new)
        p = jnp.exp(s - m_new)
        l_scr[...] = alpha * l_scr[...] + jnp.sum(p, axis=-1, keepdims=True)
        acc_scr[...] = alpha * acc_scr[...] + value_fn(p)
        m_scr[...] = m_new

    q_lat = q_ref[:, :KV_LORA]
    q_rope = q_ref[:, KV_LORA + NOPE_DIM:KV_LORA + QK_DIM]
    s_parts, c_parts = [], []
    for i in range(n):
        c_i = c_refs[i][...].astype(BF16)
        x = pe_refs[i][...] * kgr_ref[...]
        cs, sn = cos_ref[j * n + i], sin_ref[j * n + i]
        x1, x2 = x[:half], x[half:]
        kr = jnp.concatenate([x1 * cs - x2 * sn, x1 * sn + x2 * cs], axis=0).astype(BF16)
        s_i = (lax.dot_general(q_lat, c_i, nt, preferred_element_type=F32)
               + jnp.dot(q_rope, kr, preferred_element_type=F32))
        s_parts.append((s_i.reshape(T, MLA_HEADS, PAGE_SIZE) * sc_refs[i][...][None]).reshape(rows, PAGE_SIZE))
        c_parts.append(c_i)
    s = jnp.concatenate(s_parts, axis=1)
    update(s, lambda p: sum(jnp.dot(p[:, i * PAGE_SIZE:(i + 1) * PAGE_SIZE].astype(BF16), c_parts[i],
                                    preferred_element_type=F32) for i in range(n)))

    @pl.when(j == pl.num_programs(1) - 1)
    def _():
        knew = knew_ref[...]
        s_new = (lax.dot_general(q_lat, knew[:, :KV_LORA], nt, preferred_element_type=F32)
                 + lax.dot_general(q_rope, knew[:, KV_LORA + NOPE_DIM:KV_LORA + QK_DIM], nt,
                                   preferred_element_type=F32))
        kscn_t = lax.dot_general(eye, kscn_ref[...], nt, precision=lax.Precision.HIGHEST,
                                 preferred_element_type=F32)
        s_new = (s_new.reshape(T, MLA_HEADS, T) * kscn_t[None]).reshape(rows, T)
        q_tok = lax.broadcasted_iota(jnp.int32, (rows, T), 0) // MLA_HEADS
        s_new = jnp.where(lax.broadcasted_iota(jnp.int32, (rows, T), 1) <= q_tok, s_new, -jnp.inf)
        c_new = knew[:, :KV_LORA].astype(F32)

        def new_values(p):
            pb = p.astype(BF16).astype(F32)
            return sum(pb[:, t:t + 1] * c_new[t:t + 1, :] for t in range(T))

        update(s_new, new_values)
        o_ref[...] = acc_scr[...] / l_scr[...]


def paged_attention(q, knew, ksc_new, cache_ckv, cache_kpe_t, cache_ksc_t, page_table, layer, gk_rope):
    B, rows, _ = q.shape
    T = rows // MLA_HEADS
    n_pages = page_table.shape[1]
    n = PAGES_PER_STEP
    half = ROPE_DIM // 2
    inv = jnp.power(ROPE_BASE, -jnp.arange(half, dtype=F32) / half)
    pos = jnp.arange(n_pages * PAGE_SIZE).astype(F32).reshape(n_pages, 1, PAGE_SIZE)
    ang = pos * inv[None, :, None]
    cos3, sin3 = jnp.cos(ang), jnp.sin(ang)

    def page_spec(shape, i):
        return pl.BlockSpec((None, None) + shape, lambda b, j, pt: (layer, pt[b, j * n + i], 0, 0))

    grid_spec = pltpu.PrefetchScalarGridSpec(
        num_scalar_prefetch=1,
        grid=(B, n_pages // n),
        in_specs=[pl.BlockSpec((None, rows, MLA_QCAT), lambda b, j, pt: (b, 0, 0)),
                  pl.BlockSpec((None, T, MLA_QCAT), lambda b, j, pt: (b, 0, 0)),
                  pl.BlockSpec((None, T, MLA_HEADS), lambda b, j, pt: (b, 0, 0)),
                  pl.BlockSpec((n_pages, half, PAGE_SIZE), lambda b, j, pt: (0, 0, 0)),
                  pl.BlockSpec((n_pages, half, PAGE_SIZE), lambda b, j, pt: (0, 0, 0)),
                  pl.BlockSpec((ROPE_DIM, 1), lambda b, j, pt: (0, 0))]
        + [page_spec((PAGE_SIZE, KV_LORA), i) for i in range(n)]
        + [page_spec((ROPE_DIM, PAGE_SIZE), i) for i in range(n)]
        + [page_spec((MLA_HEADS, PAGE_SIZE), i) for i in range(n)],
        out_specs=pl.BlockSpec((None, rows, KV_LORA), lambda b, j, pt: (b, 0, 0)),
        scratch_shapes=[pltpu.VMEM((rows, 1), F32), pltpu.VMEM((rows, 1), F32), pltpu.VMEM((rows, KV_LORA), F32)],
    )
    return pl.pallas_call(
        _paged_attn_kernel,
        grid_spec=grid_spec,
        out_shape=jax.ShapeDtypeStruct((B, rows, KV_LORA), F32),
        compiler_params=_cparams(("parallel", "arbitrary")),
        name="paged_attention",
    )(page_table, q, knew, ksc_new, cos3, sin3, gk_rope[:, None],
      *([cache_ckv] * n), *([cache_kpe_t] * n), *([cache_ksc_t] * n))


def _vproj_kernel(o_ref, w_ref, out_ref):
    out_ref[...] = jnp.concatenate(
        [jnp.dot(o_ref[:, h * KV_LORA:(h + 1) * KV_LORA].astype(BF16), w_ref[h], preferred_element_type=F32)
         for h in range(MLA_HEADS)], axis=-1)


def mla_value_proj(o_lat, wvb):
    T = o_lat.shape[0]
    w = wvb.transpose(1, 0, 2).astype(BF16)
    return pl.pallas_call(
        _vproj_kernel,
        grid=(1,),
        in_specs=[_full_spec((T, MLA_HEADS * KV_LORA)), _full_spec((MLA_HEADS, KV_LORA, V_DIM))],
        out_specs=_full_spec((T, MLA_HEADS * V_DIM)),
        out_shape=jax.ShapeDtypeStruct((T, MLA_HEADS * V_DIM), F32),
        compiler_params=_cparams(("arbitrary",)),
        name="mla_value_proj",
    )(o_lat, w)


SSM_BC = SSM_GROUPS * SSM_STATE


def _softplus(x):
    return jnp.maximum(x, 0.0) + jnp.log(1.0 + jnp.exp(-jnp.abs(x)))


def _silu(x):
    return x * jax.nn.sigmoid(x)


def _gated_norm(y, z, g):
    y = y * _silu(z)
    return y * lax.rsqrt(jnp.mean(y * y, axis=-1, keepdims=True) + EPS) * g


def _ssd_prompt_kernel(c_ref, buf_ref, s0_ref, cw_ref, cb_ref, dtb_ref, a_ref, d_ref, ng_ref,
                       o_ref, conv_ref, sT_ref, carry_scr, s_scr):
    ci = pl.program_id(1)
    Lc = c_ref.shape[0]

    @pl.when(ci == 0)
    def _():
        carry_scr[...] = buf_ref[...]
        s_scr[...] = s0_ref[...]

    cols = c_ref[...]
    z = cols[:, :SSM_W]
    xbc = cols[:, SSM_W:SSM_W + CONV_CH]
    dtr = cols[:, SSM_W + CONV_CH:SSM_W + CONV_CH + LANE]
    ext = jnp.concatenate([carry_scr[...], xbc], axis=0)
    conv = cb_ref[...]
    for i in range(CONV_W):
        lo = 8 - (CONV_W - 1) + i
        conv = conv + ext[lo:lo + Lc, :] * cw_ref[i:i + 1, :]
    carry_scr[...] = xbc[Lc - 8:, :]
    conv = _silu(conv)
    xs, Bm, Cm = conv[:, :SSM_W], conv[:, SSM_W:SSM_W + SSM_BC], conv[:, SSM_W + SSM_BC:]
    dt = _softplus(dtr + dtb_ref[...])
    a = dt * a_ref[...]
    ri = lax.broadcasted_iota(jnp.int32, (Lc, Lc), 0)
    cj = lax.broadcasted_iota(jnp.int32, (Lc, Lc), 1)
    causal = cj <= ri
    acs = jnp.dot(causal.astype(F32), a, precision=lax.Precision.HIGHEST, preferred_element_type=F32)
    acs_t = acs.T
    nt = (((1,), (1,)), ((), ()))
    ys = []
    for g in range(SSM_GROUPS):
        Bg = Bm[:, g * SSM_STATE:(g + 1) * SSM_STATE].astype(BF16)
        Cg = Cm[:, g * SSM_STATE:(g + 1) * SSM_STATE].astype(BF16)
        cb = lax.dot_general(Cg, Bg, nt, preferred_element_type=F32)
        for e in range(SSM_EPG):
            hd = g * SSM_EPG + e
            col = acs[:, hd:hd + 1]
            lmat = jnp.exp(jnp.where(causal, col - acs_t[hd:hd + 1, :], -jnp.inf))
            x_e = xs[:, hd * SSM_HD:(hd + 1) * SSM_HD]
            xdt = x_e * dt[:, hd:hd + 1]
            y = jnp.dot((cb * lmat).astype(BF16), xdt.astype(BF16), preferred_element_type=F32)
            last = acs[Lc - 1:Lc, hd:hd + 1]
            s_in = s_scr[hd]
            y = y + lax.dot_general(Cg, s_in.astype(BF16), nt, preferred_element_type=F32) * jnp.exp(col)
            st = lax.dot_general((xdt * jnp.exp(last - col)).astype(BF16), Bg, (((0,), (0,)), ((), ())),
                                 preferred_element_type=F32)
            s_scr[hd] = s_in * jnp.exp(last) + st
            ys.append(y + x_e * d_ref[:, hd:hd + 1])
    o_ref[...] = _gated_norm(jnp.concatenate(ys, axis=-1), z, ng_ref[...])

    @pl.when(ci == pl.num_programs(1) - 1)
    def _():
        conv_ref[...] = xbc[Lc - (CONV_W - 1):, :]
        sT_ref[...] = s_scr[...]


def _ssm_params(lp):
    pad = lambda v: jnp.pad(v, (0, LANE - SSM_HEADS))[None, :]
    return dict(cw=lp['ssm_conv_w'], cb=lp['ssm_conv_b'][None, :], dtb=pad(lp['ssm_dt_bias']),
                a=pad(-jnp.exp(lp['ssm_a_log'])), d=pad(lp['ssm_d']), ng=lp['ssm_norm_g'][None, :])


def ssd_prompt(ssm_cols, conv_buf, s0, lp, B, seq):
    Lc = SSD_CHUNK
    nc = seq // Lc
    p = _ssm_params(lp)
    buf8 = jnp.pad(conv_buf, ((0, 0), (8 - (CONV_W - 1), 0), (0, 0)))
    st_spec = pl.BlockSpec((None, SSM_HEADS, SSM_HD, SSM_STATE), lambda b, c: (b, 0, 0, 0))
    return pl.pallas_call(
        _ssd_prompt_kernel,
        grid=(B, nc),
        in_specs=[pl.BlockSpec((Lc, SSM_PAD), lambda b, c: (b * nc + c, 0)),
                  pl.BlockSpec((None, 8, CONV_CH), lambda b, c: (b, 0, 0)), st_spec,
                  _full_spec((CONV_W, CONV_CH)), _full_spec((1, CONV_CH)), _full_spec((1, LANE)),
                  _full_spec((1, LANE)), _full_spec((1, LANE)), _full_spec((1, SSM_W))],
        out_specs=[pl.BlockSpec((Lc, SSM_W), lambda b, c: (b * nc + c, 0)),
                   pl.BlockSpec((None, CONV_W - 1, CONV_CH), lambda b, c: (b, 0, 0)), st_spec],
        out_shape=[jax.ShapeDtypeStruct((B * seq, SSM_W), F32), jax.ShapeDtypeStruct((B, CONV_W - 1, CONV_CH), F32),
                   jax.ShapeDtypeStruct((B, SSM_HEADS, SSM_HD, SSM_STATE), F32)],
        scratch_shapes=[pltpu.VMEM((8, CONV_CH), F32), pltpu.VMEM((SSM_HEADS, SSM_HD, SSM_STATE), F32)],
        compiler_params=_cparams(("parallel", "arbitrary")),
        name="ssd_prompt",
    )(ssm_cols, buf8, s0, p['cw'], p['cb'], p['dtb'], p['a'], p['d'], p['ng'])


def _ssd_step_prep_kernel(c_ref, x0_ref, x1_ref, x2_ref, cw_ref, cb_ref, dtb_ref, a_ref, exp_ref, expg_ref,
                          w_ref, k_ref, r_ref, v_ref, xs_ref):
    cols = c_ref[...]
    xbc = cols[:, SSM_W:SSM_W + CONV_CH]
    dtr = cols[:, SSM_W + CONV_CH:SSM_W + CONV_CH + LANE]
    conv = cb_ref[...] + x0_ref[...] * cw_ref[0:1, :] + x1_ref[...] * cw_ref[1:2, :] + x2_ref[...] * cw_ref[2:3, :] \
        + xbc * cw_ref[3:4, :]
    conv = _silu(conv)
    xs, Bm, Cm = conv[:, :SSM_W], conv[:, SSM_W:SSM_W + SSM_BC], conv[:, SSM_W + SSM_BC:]
    dt = _softplus(dtr + dtb_ref[...])
    hi = lax.Precision.HIGHEST
    dt_h = jnp.dot(dt, exp_ref[...], precision=hi, preferred_element_type=F32)
    a_h = jnp.dot(dt * a_ref[...], exp_ref[...], precision=hi, preferred_element_type=F32)
    w_ref[...] = jnp.exp(a_h)
    k_ref[...] = jnp.dot(Bm, expg_ref[...], precision=hi, preferred_element_type=F32)
    r_ref[...] = jnp.dot(Cm, expg_ref[...], precision=hi, preferred_element_type=F32)
    v_ref[...] = xs * dt_h
    xs_ref[...] = xs


def _ssd_step_post_kernel(y_ref, xs_ref, c_ref, dh_ref, ng_ref, o_ref):
    y = y_ref[...] + xs_ref[...] * dh_ref[...]
    o_ref[...] = _gated_norm(y, c_ref[:, :SSM_W], ng_ref[...])


def ssd_decode(ssm_cols, conv_buf, s0, lp, B, seq, tm):
    T = B * seq
    p = _ssm_params(lp)
    xbc = ssm_cols[:, SSM_W:SSM_W + CONV_CH].reshape(B, seq, CONV_CH)
    xpad = jnp.concatenate([conv_buf, xbc], axis=1)
    shifted = [xpad[:, i:i + seq].reshape(T, CONV_CH) for i in range(CONV_W - 1)]
    head_of_lane = jnp.arange(SSM_W) // SSM_HD
    expand = (jnp.arange(LANE)[:, None] == head_of_lane[None, :]).astype(F32)
    src = (head_of_lane // SSM_EPG) * SSM_STATE + jnp.arange(SSM_W) % SSM_STATE
    expand_g = (jnp.arange(SSM_BC)[:, None] == src[None, :]).astype(F32)
    out = jax.ShapeDtypeStruct((T, SSM_W), F32)
    w, k, r, v, xs = pl.pallas_call(
        _ssd_step_prep_kernel,
        grid=(T // tm,),
        in_specs=[_tok_spec(tm, SSM_PAD)] + [_tok_spec(tm, CONV_CH)] * 3
        + [_full_spec((CONV_W, CONV_CH)), _full_spec((1, CONV_CH)), _full_spec((1, LANE)), _full_spec((1, LANE)),
           _full_spec((LANE, SSM_W)), _full_spec((SSM_BC, SSM_W))],
        out_specs=[_tok_spec(tm, SSM_W)] * 5,
        out_shape=[out] * 5,
        compiler_params=_cparams(("parallel",)),
        name="ssd_step_prep",
    )(ssm_cols, *shifted, p['cw'], p['cb'], p['dtb'], p['a'], expand, expand_g)
    sh = lambda t: t.reshape(B, seq, SSM_W)
    zeros = jnp.zeros((B, seq, SSM_W), F32)
    y, sT = state_scan(sh(w), zeros, zeros, sh(k), sh(r), sh(v), s0, seq)
    d_h = jnp.repeat(lp['ssm_d'], SSM_HD)[None, :]
    o = pl.pallas_call(
        _ssd_step_post_kernel,
        grid=(T // tm,),
        in_specs=[_tok_spec(tm, SSM_W), _tok_spec(tm, SSM_W), _tok_spec(tm, SSM_PAD), _full_spec((1, SSM_W)),
                  _full_spec((1, SSM_W))],
        out_specs=_tok_spec(tm, SSM_W),
        out_shape=out,
        compiler_params=_cparams(("parallel",)),
        name="ssd_step_post",
    )(y.reshape(T, SSM_W), xs, ssm_cols, d_h, p['ng'])
    return o, xpad[:, -(CONV_W - 1):], sT


def _merge_kernel(oa_ref, ob_ref, oc_ref, gate_ref, x_ref, g1_ref, wb_ref, wo_ref, o_ref):
    D = x_ref.shape[1]
    acc = jnp.zeros(x_ref.shape, F32)
    for i, ref in enumerate((oa_ref, ob_ref, oc_ref)):
        u = jnp.dot(ref[...].astype(BF16), wb_ref[i], preferred_element_type=F32)
        acc = acc + jax.nn.sigmoid(gate_ref[:, i * D:(i + 1) * D]) * u
    y = jnp.dot(acc.astype(BF16), wo_ref[...], preferred_element_type=F32)
    o_ref[...] = x_ref[...] + g1_ref[...] * y


def branch_merge(o_a, o_b, o_c, gates, x2, g1, wb_bf, wo_bf, seq, tm):
    T, D = x2.shape
    return pl.pallas_call(
        _merge_kernel,
        grid=(T // tm,),
        in_specs=[_tok_spec(tm, BR_W)] * 3 + [_tok_spec(tm, N_BRANCH * D), _tok_spec(tm, D), _row_spec(seq, tm, D),
                                              _full_spec((N_BRANCH, BR_W, D)), _full_spec((D, D))],
        out_specs=_tok_spec(tm, D),
        out_shape=jax.ShapeDtypeStruct((T, D), F32),
        compiler_params=_cparams(("parallel",)),
        name="branch_merge",
    )(o_a, o_b, o_c, gates, x2, _row_operand(g1, seq, tm), wb_bf, wo_bf)


TM_TOKENS = 512
TM_INPROJ = 256
TM_ROUTE = 256
SCAN_CHUNK = 32
FLASH_BLOCK = 512


def _layer(x2, B, seq, c, lw, st, pos, layer, decode_ctx):
    T = B * seq
    tm = min(TM_TOKENS, T)
    sh1, sc1, g1, sh2, sc2, g2 = jnp.split(ada_mod(c, lw['ada_w'], lw['ada_b'], layer), 6, axis=-1)
    lp = lw['lp']
    rw, mla, ssm, gates = in_proj(x2, sc1, sh1, lp['norm1_g'], lw['w_in_pad'], seq, min(TM_INPROJ, T))
    o_a, rw_shift, rw_state = rwkv_branch(rw, B, seq, st[0], st[1], lp, tm, min(SCAN_CHUNK, seq))
    if decode_ctx is None:
        q, k, v, ckv, ksc = mla_prompt_proj(mla, pos, lp, B, seq, min(tm, seq))
        o_b = flash_attention(q, k, v, min(FLASH_BLOCK, seq)).reshape(T, BR_W)
        o_c, conv_buf, ssm_state = ssd_prompt(ssm, st[2], st[3], lp, B, seq)
    else:
        cache_ckv, cache_kpe, cache_ksc, page_table = decode_ctx
        q, knew, ckv, ksc = mla_sample_proj(mla, pos, lp, seq, tm)
        o_lat = paged_attention(q.reshape(B, seq * MLA_HEADS, MLA_QCAT), knew.reshape(B, seq, MLA_QCAT),
                                ksc.reshape(B, seq, MLA_HEADS), cache_ckv, cache_kpe, cache_ksc, page_table,
                                layer, lp['mla_k_g'][NOPE_DIM:])
        o_b = mla_value_proj(o_lat.reshape(T, MLA_HEADS * KV_LORA), lp['mla_wvb'])
        o_c, conv_buf, ssm_state = ssd_decode(ssm, st[2], st[3], lp, B, seq, tm)
    x2 = branch_merge(o_a, o_b, o_c, gates, x2, g1, lw['wb_bf'], lw['wo_bf'], seq, tm)
    h, eid, gate = peer_route(x2, sc2, sh2, lp['norm2_g'], lw['wq_bf'], lw['keys_bf'], seq, min(TM_ROUTE, T))
    x2 = peer_gather(eid, h, gate, x2, g2, seq, lw['uv'])
    kpe = mla[:, Q_LORA + KV_LORA:MLA_COLS]
    return x2, (ckv.reshape(B, seq, KV_LORA), kpe.reshape(B, seq, ROPE_DIM), ksc.reshape(B, seq, MLA_HEADS),
                rw_shift, rw_state, conv_buf, ssm_state)


def kernel(x_prompt, x_sample, cache_ckv, cache_kpe, cache_kscale, state_rwkv_shift, state_rwkv_wkv,
           state_ssm_conv, state_ssm, page_table, c_prompt, c_sample, ada_w, ada_b, norm1_g, norm2_g, w_in,
           rw_mu, rw_w0, rw_w2, rw_a0, rw_a2, rw_g2, rw_kk, rw_ka, rw_rk, rw_ln_w, rw_ln_b,
           mla_qa_g, mla_wqb, mla_kv_g, mla_wkb, mla_wvb, mla_q_g, mla_k_g,
           ssm_conv_w, ssm_conv_b, ssm_dt_bias, ssm_a_log, ssm_d, ssm_norm_g,
           w_branch, w_out, pk_wq, pk_keys, pk_u, pk_v):
    small = dict(norm1_g=norm1_g, norm2_g=norm2_g,
                 rw_mu=rw_mu, rw_w0=rw_w0, rw_w2=rw_w2, rw_a0=rw_a0, rw_a2=rw_a2, rw_g2=rw_g2,
                 rw_kk=rw_kk, rw_ka=rw_ka, rw_rk=rw_rk, rw_ln_w=rw_ln_w, rw_ln_b=rw_ln_b,
                 mla_qa_g=mla_qa_g, mla_wqb=mla_wqb, mla_kv_g=mla_kv_g, mla_wkb=mla_wkb, mla_wvb=mla_wvb,
                 mla_q_g=mla_q_g, mla_k_g=mla_k_g,
                 ssm_conv_w=ssm_conv_w, ssm_conv_b=ssm_conv_b, ssm_dt_bias=ssm_dt_bias, ssm_a_log=ssm_a_log,
                 ssm_d=ssm_d, ssm_norm_g=ssm_norm_g)
    depth = ada_w.shape[0]
    Bp, S, D = x_prompt.shape
    Bd, T, _ = x_sample.shape
    assert SSD_CHUNK % 8 == 0 and S % SSD_CHUNK == 0 and T < SSD_CHUNK
    past_len = page_table.shape[1] * PAGE_SIZE
    pos_p = jnp.arange(S)
    pos_s = past_len + jnp.arange(T)
    zero_state = (jnp.zeros((Bp, RW_COLS), F32), jnp.zeros((Bp, RW_HEADS, RW_HD, RW_HD), F32),
                  jnp.zeros((Bp, CONV_W - 1, CONV_CH), F32), jnp.zeros((Bp, SSM_HEADS, SSM_HD, SSM_STATE), F32))
    hp, hs = x_prompt.reshape(Bp * S, D), x_sample.reshape(Bd * T, D)
    decode_ctx = (cache_ckv, jnp.swapaxes(cache_kpe, -1, -2), jnp.swapaxes(cache_kscale, -1, -2), page_table)
    new_p, new_s = [], []
    for l in range(depth):
        lw = dict(lp={name: arr[l] for name, arr in small.items()}, ada_w=ada_w, ada_b=ada_b,
                  w_in_pad=_pad_w_in(w_in[l]), wb_bf=w_branch[l].astype(BF16), wo_bf=w_out[l].astype(BF16),
                  wq_bf=pk_wq[l].astype(BF16),
                  keys_bf=pk_keys[l].reshape(2 * PK_HEADS, PK_NKEYS, PK_DHALF).astype(BF16),
                  uv=peer_table(pk_u[l], pk_v[l]))
        hp, st_p = _layer(hp, Bp, S, c_prompt, lw, zero_state, pos_p, l, None)
        st_in = (state_rwkv_shift[l], state_rwkv_wkv[l], state_ssm_conv[l], state_ssm[l])
        hs, st_s = _layer(hs, Bd, T, c_sample, lw, st_in, pos_s, l, decode_ctx)
        new_p.append(st_p)
        new_s.append(st_s)
    hp, hs = hp.reshape(Bp, S, D), hs.reshape(Bd, T, D)
    return (hp, hs,
            jnp.stack([s[0] for s in new_p]), jnp.stack([s[0] for s in new_s]),
            jnp.stack([s[1] for s in new_p]), jnp.stack([s[1] for s in new_s]),
            jnp.stack([s[2] for s in new_p]), jnp.stack([s[2] for s in new_s]),
            jnp.stack([s[3] for s in new_p]), jnp.stack([s[3] for s in new_s]),
            jnp.stack([s[4] for s in new_p]), jnp.stack([s[4] for s in new_s]),
            jnp.stack([s[5] for s in new_p]), jnp.stack([s[5] for s in new_s]),
            jnp.stack([s[6] for s in new_p]), jnp.stack([s[6] for s in new_s]))
```

```python
import math, functools
import jax, jax.numpy as jnp
from jax import lax
from jax.experimental import pallas as pl
from jax.experimental.pallas import tpu as pltpu

D_MODEL = 1024
PAGE_SIZE = 128
EPS = 1e-6
F32 = jnp.float32
BF16 = jnp.bfloat16
RW_HEADS = 8
RW_HD = 64
RW_W = RW_HEADS * RW_HD
RW_DECAY_LORA = 64
RW_AAA_LORA = 64
RW_GATE_LORA = 128
RW_COLS = 3 * RW_W + RW_DECAY_LORA + RW_AAA_LORA + RW_GATE_LORA
RW_SPLITS = (RW_W, 2 * RW_W, 3 * RW_W, 3 * RW_W + RW_DECAY_LORA, 3 * RW_W + RW_DECAY_LORA + RW_AAA_LORA)
RW_GN_EPS = 64e-5
MLA_HEADS = 8
Q_LORA = 256
KV_LORA = 256
NOPE_DIM = 64
ROPE_DIM = 32
V_DIM = 64
QK_DIM = NOPE_DIM + ROPE_DIM
MLA_COLS = Q_LORA + KV_LORA + ROPE_DIM
ROPE_BASE = 10000.0
ATTN_SCALE = QK_DIM ** -0.5
Q_BLOCK = 128
SSM_HEADS = 8
SSM_HD = 64
SSM_W = SSM_HEADS * SSM_HD
SSM_GROUPS = 2
SSM_EPG = SSM_HEADS // SSM_GROUPS
SSM_STATE = 64
CONV_W = 4
CONV_CH = SSM_W + 2 * SSM_GROUPS * SSM_STATE
SSM_COLS = SSM_W + CONV_CH + SSM_HEADS
SSD_CHUNK = 128
N_BRANCH = 3
BR_W = RW_W
IN_COLS = RW_COLS + MLA_COLS + SSM_COLS + N_BRANCH * D_MODEL
IN_SPLITS = (RW_COLS, RW_COLS + MLA_COLS, RW_COLS + MLA_COLS + SSM_COLS)
PK_HEADS = 8
PK_DKEY = 256
PK_DHALF = PK_DKEY // 2
PK_NKEYS = 128
PK_TOPK = 16
PK_EXPERTS = PK_NKEYS * PK_NKEYS
PK_TOK_BLOCK = 256

LANE = 128
VMEM_LIMIT = 56 * 1024 * 1024

MLA_PAD = 640
SSM_PAD = 1408
PROJ_OFF = (0, RW_COLS, RW_COLS + MLA_PAD, RW_COLS + MLA_PAD + SSM_PAD)
PROJ_COLS = PROJ_OFF[3] + N_BRANCH * D_MODEL


def _cparams(sem):
    return pltpu.CompilerParams(dimension_semantics=sem, vmem_limit_bytes=VMEM_LIMIT)


def _pad_w_in(w_in):
    rw, mla, ssm, gate = jnp.split(w_in, IN_SPLITS, axis=-1)
    mla = jnp.pad(mla, ((0, 0), (0, MLA_PAD - MLA_COLS)))
    ssm = jnp.pad(ssm, ((0, 0), (0, SSM_PAD - SSM_COLS)))
    return jnp.concatenate([rw, mla, ssm, gate], axis=-1).astype(BF16)


PROJ_WIDTHS = (RW_COLS, MLA_PAD, SSM_PAD, N_BRANCH * D_MODEL)


def _inproj_kernel(x_ref, sc_ref, sh_ref, g_ref, w_rw, w_mla, w_ssm, w_gate, o_rw, o_mla, o_ssm, o_gate):
    x = x_ref[...]
    y = x * lax.rsqrt(jnp.mean(x * x, axis=-1, keepdims=True) + EPS) * g_ref[...]
    h = (y * (1.0 + sc_ref[...]) + sh_ref[...]).astype(BF16)
    for w_ref, o_ref in ((w_rw, o_rw), (w_mla, o_mla), (w_ssm, o_ssm), (w_gate, o_gate)):
        o_ref[...] = jnp.dot(h, w_ref[...], preferred_element_type=F32)


def _row_spec(rows_per_group, tm, d):
    if rows_per_group >= tm:
        tiles = rows_per_group // tm
        return pl.BlockSpec((None, 1, d), lambda i, *_: (i // tiles, 0, 0))
    return pl.BlockSpec((None, tm, d), lambda i, *_: (i, 0, 0))


def _row_operand(v, seq, tm):
    if seq >= tm:
        return v[:, None, :]
    return jnp.repeat(v, seq, axis=0).reshape(-1, tm, v.shape[-1])


def in_proj(x2, sc, sh, g, w_pad, seq, tm):
    T, D = x2.shape
    ws = [w_pad[:, PROJ_OFF[i]:PROJ_OFF[i] + PROJ_WIDTHS[i]] for i in range(4)]
    return pl.pallas_call(
        _inproj_kernel,
        grid=(T // tm,),
        in_specs=[_tok_spec(tm, D), _row_spec(seq, tm, D), _row_spec(seq, tm, D), _full_spec((1, D))]
        + [_full_spec((D, n)) for n in PROJ_WIDTHS],
        out_specs=[_tok_spec(tm, n) for n in PROJ_WIDTHS],
        out_shape=[jax.ShapeDtypeStruct((T, n), F32) for n in PROJ_WIDTHS],
        compiler_params=_cparams(("parallel",)),
        name="in_proj",
    )(x2, _row_operand(sc, seq, tm), _row_operand(sh, seq, tm), g[None, :], *ws)


def _seg_sum(x, seg):
    hi = x.astype(BF16)
    lo = (x - hi.astype(F32)).astype(BF16)
    return (jnp.dot(hi, seg, preferred_element_type=F32) + jnp.dot(lo, seg, preferred_element_type=F32))


def _seg_matrix(width, seg):
    i = jnp.arange(width) // seg
    return (i[:, None] == i[None, :]).astype(BF16)


def _gelu_tanh(x):
    return 0.5 * x * (1.0 + jnp.tanh(math.sqrt(2.0 / math.pi) * (x + 0.044715 * (x * x * x))))


def _tok_spec(tm, d, col_block=0):
    return pl.BlockSpec((tm, d), lambda i, *_: (i, col_block))


def _full_spec(shape):
    n = len(shape)
    return pl.BlockSpec(shape, lambda *_: (0,) * n)


def _rwkv_prep_kernel(c_ref, xp_ref, mu_ref, w0_ref, w2_ref, a0_ref, a2_ref, g2_ref, kk_ref, ka_ref, seg_ref,
                      r_ref, w_ref, k_ref, v_ref, a_ref, b_ref, g_ref):
    c = c_ref[...]
    m = c + (xp_ref[...] - c) * mu_ref[...]
    r, k, v = m[:, :RW_W], m[:, RW_W:2 * RW_W], m[:, 2 * RW_W:3 * RW_W]
    wl = m[:, RW_SPLITS[2]:RW_SPLITS[3]]
    al = m[:, RW_SPLITS[3]:RW_SPLITS[4]]
    gl = m[:, RW_SPLITS[4]:]
    nz = -(w0_ref[...] + jnp.dot(jnp.tanh(wl), w2_ref[...], preferred_element_type=F32))
    softplus = jnp.maximum(nz, 0.0) + jnp.log(1.0 + jnp.exp(-jnp.abs(nz)))
    decay = jnp.exp(-jnp.exp(-softplus - 0.5))
    a = jax.nn.sigmoid(a0_ref[...] + jnp.dot(al, a2_ref[...], preferred_element_type=F32))
    g = jnp.dot(jax.nn.sigmoid(gl), g2_ref[...], preferred_element_type=F32)
    kk = k * kk_ref[...]
    k = k * (1.0 + (a - 1.0) * ka_ref[...])
    kk = kk / jnp.maximum(jnp.sqrt(_seg_sum(kk * kk, seg_ref[...])), 1e-12)
    r_ref[...] = r
    w_ref[...] = decay
    k_ref[...] = k
    v_ref[...] = v
    a_ref[...] = -kk
    b_ref[...] = kk * a
    g_ref[...] = g


def rwkv_prep(proj, xprev, lp, tm):
    T = proj.shape[0]
    row = lambda v: v.reshape(1, -1)
    out = jax.ShapeDtypeStruct((T, RW_W), F32)
    return pl.pallas_call(
        _rwkv_prep_kernel,
        grid=(T // tm,),
        in_specs=[_tok_spec(tm, RW_COLS), _tok_spec(tm, RW_COLS), _full_spec((1, RW_COLS)),
                  _full_spec((1, RW_W)), _full_spec((RW_DECAY_LORA, RW_W)),
                  _full_spec((1, RW_W)), _full_spec((RW_AAA_LORA, RW_W)),
                  _full_spec((RW_GATE_LORA, RW_W)), _full_spec((1, RW_W)), _full_spec((1, RW_W)),
                  _full_spec((RW_W, RW_W))],
        out_specs=[_tok_spec(tm, RW_W)] * 7,
        out_shape=[out] * 7,
        compiler_params=_cparams(("parallel",)),
        name="rwkv_prep",
    )(proj, xprev, row(lp['rw_mu']), row(lp['rw_w0']), lp['rw_w2'], row(lp['rw_a0']), lp['rw_a2'], lp['rw_g2'],
      row(lp['rw_kk']), row(lp['rw_ka']), _seg_matrix(RW_W, RW_HD))


SCAN_HEADS = 32
SCAN_VM = LANE // SCAN_HEADS
SCAN_VD = RW_HD // SCAN_VM


def _scan_kernel(w_ref, a_ref, b_ref, k_ref, r_ref, v_ref, s0_ref, y_ref, sT_ref, s_scr):
    c = pl.program_id(1)

    @pl.when(c == 0)
    def _():
        s_scr[...] = s0_ref[...]

    def step(t, carry):
        A, W, Bm, K, R = a_ref[t], w_ref[t], b_ref[t], k_ref[t], r_ref[t]
        V = v_ref[t]
        ys = []
        for vd in range(SCAN_VD):
            S = s_scr[vd]
            sa = jnp.sum(S * A, axis=0, keepdims=True)
            S = S * W + sa * Bm + V[vd:vd + 1, :] * K
            s_scr[vd] = S
            ys.append(jnp.sum(S * R, axis=0, keepdims=True))
        y_ref[t] = jnp.concatenate(ys, axis=0)
        return carry

    lax.fori_loop(0, y_ref.shape[0], step, 0)

    @pl.when(c == pl.num_programs(1) - 1)
    def _():
        sT_ref[...] = s_scr[...]


def _to_scan_kh(x, n_groups):
    B, T, _ = x.shape
    x = x.reshape(B, T, RW_HEADS, RW_HD).transpose(1, 3, 0, 2).reshape(T, RW_HD, n_groups, SCAN_HEADS)
    x = jnp.broadcast_to(x[..., None], (T, RW_HD, n_groups, SCAN_HEADS, SCAN_VM))
    return x.transpose(2, 0, 1, 3, 4).reshape(n_groups, T, RW_HD, LANE)


def _to_scan_v(x, n_groups):
    B, T, _ = x.shape
    x = x.reshape(B, T, RW_HEADS, SCAN_VD, SCAN_VM).transpose(1, 3, 0, 2, 4)
    x = x.reshape(T, SCAN_VD, n_groups, SCAN_HEADS * SCAN_VM)
    return x.transpose(2, 0, 1, 3)


def _from_scan_v(y, B):
    G, T = y.shape[:2]
    y = y.transpose(1, 2, 0, 3).reshape(T, SCAN_VD, B, RW_HEADS, SCAN_VM)
    return y.transpose(2, 0, 3, 1, 4).reshape(B, T, RW_W)


def _state_to_scan(s, n_groups):
    B = s.shape[0]
    s = s.reshape(B, RW_HEADS, SCAN_VD, SCAN_VM, RW_HD).transpose(2, 4, 0, 1, 3)
    s = s.reshape(SCAN_VD, RW_HD, n_groups, SCAN_HEADS * SCAN_VM)
    return s.transpose(2, 0, 1, 3)


def _state_from_scan(s, B):
    s = s.transpose(1, 2, 0, 3).reshape(SCAN_VD, RW_HD, B, RW_HEADS, SCAN_VM)
    return s.transpose(2, 3, 0, 4, 1).reshape(B, RW_HEADS, RW_HD, RW_HD)


def state_scan(w, a, b, k, r, v, s0, tc):
    B, T, _ = w.shape
    G = (B * RW_HEADS) // SCAN_HEADS
    kh = [_to_scan_kh(t, G) for t in (w, a, b, k, r)]
    vv = _to_scan_v(v, G)
    ss = _state_to_scan(s0, G)
    kh_spec = pl.BlockSpec((None, tc, RW_HD, LANE), lambda g, c: (g, c, 0, 0))
    v_spec = pl.BlockSpec((None, tc, SCAN_VD, LANE), lambda g, c: (g, c, 0, 0))
    s_spec = pl.BlockSpec((None, SCAN_VD, RW_HD, LANE), lambda g, c: (g, 0, 0, 0))
    y, sT = pl.pallas_call(
        _scan_kernel,
        grid=(G, T // tc),
        in_specs=[kh_spec] * 5 + [v_spec, s_spec],
        out_specs=[v_spec, s_spec],
        out_shape=[jax.ShapeDtypeStruct((G, T, SCAN_VD, LANE), F32),
                   jax.ShapeDtypeStruct((G, SCAN_VD, RW_HD, LANE), F32)],
        scratch_shapes=[pltpu.VMEM((SCAN_VD, RW_HD, LANE), F32)],
        compiler_params=_cparams(("parallel", "arbitrary")),
        name="state_scan",
    )(*kh, vv, ss)
    return _from_scan_v(y, B), _state_from_scan(sT, B)


def _rwkv_post_kernel(y_ref, r_ref, k_ref, v_ref, g_ref, lnw_ref, lnb_ref, rk_ref, seg_ref, o_ref):
    seg = seg_ref[...]
    y = y_ref[...]
    d = y - _seg_sum(y, seg) * (1.0 / RW_HD)
    var = _seg_sum(d * d, seg) * (1.0 / RW_HD)
    yn = d * lax.rsqrt(var + RW_GN_EPS) * lnw_ref[...] + lnb_ref[...]
    bonus = _seg_sum(r_ref[...] * k_ref[...] * rk_ref[...], seg) * v_ref[...]
    o_ref[...] = (yn + bonus) * g_ref[...]


def rwkv_post(y, r, k, v, g, lp, tm):
    T = y.shape[0]
    row = lambda t: t.reshape(1, -1)
    return pl.pallas_call(
        _rwkv_post_kernel,
        grid=(T // tm,),
        in_specs=[_tok_spec(tm, RW_W)] * 5 + [_full_spec((1, RW_W))] * 3 + [_full_spec((RW_W, RW_W))],
        out_specs=_tok_spec(tm, RW_W),
        out_shape=jax.ShapeDtypeStruct((T, RW_W), F32),
        compiler_params=_cparams(("parallel",)),
        name="rwkv_post",
    )(y, r, k, v, g, row(lp['rw_ln_w']), row(lp['rw_ln_b']), row(lp['rw_rk']), _seg_matrix(RW_W, RW_HD))


def rwkv_branch(rw2, B, T, prev_row, s0, lp, tm, tc):
    rw_cols = rw2.reshape(B, T, RW_COLS)
    xprev = jnp.concatenate([prev_row[:, None, :], rw_cols[:, :-1]], axis=1).reshape(B * T, RW_COLS)
    r, w, k, v, a, b, g = rwkv_prep(rw2, xprev, lp, tm)
    sh = lambda t: t.reshape(B, T, RW_W)
    y, sT = state_scan(sh(w), sh(a), sh(b), sh(k), sh(r), sh(v), s0, tc)
    o = rwkv_post(y.reshape(B * T, RW_W), r, k, v, g, lp, tm)
    return o, rw_cols[:, -1], sT


def _topk_rows(s, rows, k, payload=None):
    n = s.shape[0]
    vals, outs = [], []
    for _ in range(k):
        m = jnp.max(s, axis=0, keepdims=True)
        pos = jnp.min(jnp.where(s == m, rows, n), axis=0, keepdims=True)
        sel = rows == pos
        vals.append(m)
        outs.append(pos if payload is None else jnp.sum(jnp.where(sel, payload, 0), axis=0, keepdims=True))
        s = jnp.where(sel, -jnp.inf, s)
    return jnp.concatenate(vals, axis=0), jnp.concatenate(outs, axis=0)


_PAIR_BLOCKS = ((0, 1, PK_TOPK, PK_TOPK),) + tuple((a, 1, 8, PK_TOPK // (a + 1)) for a in range(1, 8)) + ((8, 8, 1, 8),)
_PAIR_ROWS = sum(max(na, nb) for _, na, nb, _ in _PAIR_BLOCKS)


def _peer_route_kernel(x_ref, sc_ref, sh_ref, g_ref, wq_ref, keys_ref, h_ref, eid_ref, gate_ref):
    x = x_ref[...]
    tm = x.shape[0]
    y = x * lax.rsqrt(jnp.mean(x * x, axis=-1, keepdims=True) + EPS) * g_ref[...]
    h = y * (1.0 + sc_ref[...]) + sh_ref[...]
    h_ref[...] = h
    q = jnp.dot(h.astype(BF16), wq_ref[...], preferred_element_type=F32)
    rows_k = lax.broadcasted_iota(jnp.int32, (PK_NKEYS, tm), 0)
    rows_c = lax.broadcasted_iota(jnp.int32, (_PAIR_ROWS, tm), 0)
    eids, gates = [], []
    for hh in range(PK_HEADS):
        tops = []
        for i in range(2):
            c0 = (hh * 2 + i) * PK_DHALF
            s = lax.dot_general(keys_ref[hh * 2 + i], q[:, c0:c0 + PK_DHALF].astype(BF16),
                                (((1,), (1,)), ((), ())), preferred_element_type=F32)
            tops.append(_topk_rows(s, rows_k, PK_TOPK))
        (s0, i0), (s1, i1) = tops
        cs, ci = [], []
        for a0, na, nb, keep_n in _PAIR_BLOCKS:
            blk = s0[a0:a0 + na] + s1[0:nb]
            if keep_n < blk.shape[0]:
                blk = jnp.where(lax.broadcasted_iota(jnp.int32, blk.shape, 0) < keep_n, blk, -jnp.inf)
            cs.append(blk)
            ci.append(i0[a0:a0 + na] * PK_NKEYS + i1[0:nb])
        cand = jnp.concatenate(cs, axis=0)
        cid = jnp.concatenate(ci, axis=0)
        best, eid = _topk_rows(cand, rows_c, PK_TOPK, payload=cid)
        e = jnp.exp(best - best[0:1])
        gates.append(e / jnp.sum(e, axis=0, keepdims=True))
        eids.append(eid)
    eid_ref[...] = jnp.concatenate(eids, axis=0).T
    gate_ref[...] = jnp.concatenate(gates, axis=0).T


def peer_route(x2, sc, sh, norm_g, wq_bf, keys_bf, seq, tm):
    T, D = x2.shape
    nq = PK_HEADS * PK_DKEY
    nsel = PK_HEADS * PK_TOPK
    return pl.pallas_call(
        _peer_route_kernel,
        grid=(T // tm,),
        in_specs=[_tok_spec(tm, D), _row_spec(seq, tm, D), _row_spec(seq, tm, D), _full_spec((1, D)),
                  _full_spec((D, nq)), _full_spec((2 * PK_HEADS, PK_NKEYS, PK_DHALF))],
        out_specs=[_tok_spec(tm, D), _tok_spec(tm, nsel), _tok_spec(tm, nsel)],
        out_shape=[jax.ShapeDtypeStruct((T, D), F32), jax.ShapeDtypeStruct((T, nsel), jnp.int32),
                   jax.ShapeDtypeStruct((T, nsel), F32)],
        compiler_params=_cparams(("parallel",)),
        name="peer_route",
    )(x2, _row_operand(sc, seq, tm), _row_operand(sh, seq, tm), norm_g[None, :], wq_bf, keys_bf)


PEER_TOK = 256
PEER_SLOTS = 4
PEER_SEL = PK_HEADS * PK_TOPK


PEER_ROWS = 2 * D_MODEL // LANE


def _fold_pairs(x, shift):
    n = x.shape[0] // 2
    x = x.reshape(n, 2, 8, LANE)
    a, b = x[:, 0], x[:, 1]
    low = (lax.broadcasted_iota(jnp.int32, (n, 8, LANE), 1) & shift) == 0
    return jnp.where(low, a + pltpu.roll(a, 8 - shift, 1), b + pltpu.roll(b, shift, 1))


def _fold_order(r):
    s = r & 7
    return (r & ~7) | ((s & 1) << 2) | (s & 2) | ((s >> 2) & 1)


def _peer_gather_kernel(eid_ref, h_ref, gate_ref, x_ref, g2_ref, uv_hbm, o_ref,
                        eid_smem, w_scr, buf0, buf1, buf2, buf3, sem_idx, sem):
    bufs = (buf0, buf1, buf2, buf3)
    idx_copy = pltpu.make_async_copy(eid_ref, eid_smem, sem_idx)
    idx_copy.start()
    idx_copy.wait()
    ns = PEER_SLOTS
    n_tok = gate_ref.shape[0]
    n_groups = n_tok // ns
    rows = PEER_ROWS

    def issue(t, slot):
        base = t * PEER_SEL
        for kk in range(PEER_SEL):
            src = pl.multiple_of(eid_smem[base + kk] * rows, rows)
            pltpu.make_async_copy(uv_hbm.at[pl.ds(src, rows)], bufs[slot].at[pl.ds(kk * rows, rows)],
                                  sem.at[slot]).start(priority=kk % 2)

    def wait(slot):
        pltpu.make_async_copy(uv_hbm.at[pl.ds(0, PEER_SEL * rows)], bufs[slot], sem.at[slot]).wait()

    rr = lax.broadcasted_iota(jnp.int32, (PEER_SEL, PEER_SEL), 0)
    pick = _fold_order(rr) == lax.broadcasted_iota(jnp.int32, (PEER_SEL, PEER_SEL), 1)

    def compute(t, slot):
        buf = bufs[slot]
        t8 = pl.multiple_of(t * 8, 8)
        h = h_ref[pl.ds(t8, 8), :]
        u = buf[...].reshape(PEER_SEL, rows, LANE)[:, :8, :]
        part = _fold_pairs(_fold_pairs(_fold_pairs(u * h[None], 4), 2), 1)
        act = jnp.sum(part.reshape(PEER_SEL, LANE), axis=-1, keepdims=True)
        gate = jnp.sum(jnp.where(pick, gate_ref[pl.ds(t, 1), :], 0.0), axis=-1, keepdims=True)
        w_scr[...] = jnp.broadcast_to(gate * _gelu_tanh(act), (PEER_SEL, LANE))
        acc = [jnp.zeros((8, LANE), F32) for _ in range(4)]
        for r in range(PEER_SEL):
            kk = _fold_order(r)
            v = buf[kk * rows + 8:(kk + 1) * rows, :]
            acc[r % 4] = acc[r % 4] + jnp.broadcast_to(w_scr[r:r + 1, :], (8, LANE)) * v
        y = (acc[0] + acc[1]) + (acc[2] + acc[3])
        g2 = g2_ref[...] if g2_ref.shape[0] == 8 else g2_ref[pl.ds(t8, 8), :]
        o_ref[pl.ds(t8, 8), :] = x_ref[pl.ds(t8, 8), :] + g2 * y

    for j in range(ns - 1):
        issue(j, j)

    def group(gi, carry):
        for j in range(ns):
            t = gi * ns + j
            wait(j)
            issue(t + ns - 1, (j + ns - 1) % ns)
            compute(t, j)
        return carry

    lax.fori_loop(0, n_groups - 1, group, 0)
    t0 = (n_groups - 1) * ns
    for j in range(ns):
        wait(j)
        if j == 0:
            issue(t0 + ns - 1, ns - 1)
        compute(t0 + j, j)


def peer_table(pk_u, pk_v):
    E, D = pk_u.shape
    t = jnp.concatenate([pk_u.reshape(E, D // LANE, LANE), pk_v.reshape(E, D // LANE, LANE)], axis=1)
    return t.reshape(E * PEER_ROWS, LANE)


def peer_gather(eid, h, gate, x2, g2, seq, uv):
    T, D = h.shape
    tb = min(PEER_TOK, T)
    sub = D // LANE
    assert T % tb == 0 and tb % PEER_SLOTS == 0 and tb // PEER_SLOTS >= 2 and sub == 8 and PEER_SLOTS == 4
    tiles = lambda a: a.reshape(-1, LANE)
    if seq >= tb:
        per_seq = seq // tb
        g2_op, g2_spec = tiles(g2), pl.BlockSpec((sub, LANE), lambda i: (i // per_seq, 0))
    else:
        g2_op, g2_spec = tiles(jnp.repeat(g2, seq, axis=0)), _tok_spec(tb * sub, LANE)
    out = pl.pallas_call(
        _peer_gather_kernel,
        grid=(T // tb,),
        in_specs=[pl.BlockSpec((tb * PEER_SEL,), lambda i: (i,)), _tok_spec(tb * sub, LANE), _tok_spec(tb, PEER_SEL),
                  _tok_spec(tb * sub, LANE), g2_spec, pl.BlockSpec(memory_space=pl.ANY)],
        out_specs=_tok_spec(tb * sub, LANE),
        out_shape=jax.ShapeDtypeStruct((T * sub, LANE), F32),
        scratch_shapes=[pltpu.SMEM((tb * PEER_SEL,), jnp.int32), pltpu.VMEM((PEER_SEL, LANE), F32)]
        + [pltpu.VMEM((PEER_SEL * PEER_ROWS, LANE), F32)] * PEER_SLOTS
        + [pltpu.SemaphoreType.DMA, pltpu.SemaphoreType.DMA((PEER_SLOTS,))],
        compiler_params=_cparams(("arbitrary",)),
        name="peer_gather",
    )(eid.reshape(T * PEER_SEL), tiles(h), gate, tiles(x2), g2_op, uv)
    return out.reshape(T, D)


def _ada_kernel(c_ref, w_ref, b_ref, o_ref):
    c = c_ref[...]
    o_ref[...] = jnp.dot(c * jax.nn.sigmoid(c), w_ref[...], preferred_element_type=F32) + b_ref[...]


def ada_mod(c, ada_w, ada_b, layer):
    B, D = c.shape
    N = ada_w.shape[2]
    tn = D
    return pl.pallas_call(
        _ada_kernel,
        grid=(N // tn,),
        in_specs=[_full_spec((B, D)), pl.BlockSpec((None, D, tn), lambda j: (layer, 0, j)),
                  pl.BlockSpec((None, 1, tn), lambda j: (layer, 0, j))],
        out_specs=pl.BlockSpec((B, tn), lambda j: (0, j)),
        out_shape=jax.ShapeDtypeStruct((B, N), F32),
        compiler_params=_cparams(("parallel",)),
        name="ada_mod",
    )(c, ada_w, ada_b[:, None, :])


MLA_HP = LANE


def _rope_tables(pos):
    half = ROPE_DIM // 2
    inv = jnp.power(ROPE_BASE, -jnp.arange(half, dtype=F32) / half)
    ang = pos[:, None].astype(F32) * inv
    cos, sin = jnp.cos(ang), jnp.sin(ang)
    n = pos.shape[0]
    one, zero = jnp.ones((n, NOPE_DIM), F32), jnp.zeros((n, NOPE_DIM), F32)
    tail = jnp.zeros((n, MLA_HP - QK_DIM), F32)
    c = jnp.concatenate([one, cos, cos, tail], axis=-1)
    s_up = jnp.concatenate([zero, jnp.zeros_like(sin), sin, tail], axis=-1)
    s_dn = jnp.concatenate([zero, -sin, jnp.zeros_like(sin), tail], axis=-1)
    return c, s_up, s_dn


def _rope_apply(x, c, s_up, s_dn):
    half = ROPE_DIM // 2
    return x * c + pltpu.roll(x, half, 1) * s_up + pltpu.roll(x, MLA_HP - half, 1) * s_dn


def _head_ind(width, seg, n):
    return (jnp.arange(width)[:, None] // seg == jnp.arange(n)[None, :]).astype(BF16)


def _mla_common(c_ref, qag_ref, wqb_ref, qg_ref, kvg_ref, wkb_ref, kgn_ref, kgr_ref, ind_ref,
                cos_ref, sup_ref, sdn_ref):
    c = c_ref[...]
    cos, sup, sdn = cos_ref[...], sup_ref[...], sdn_ref[...]
    qa, ckv, kpe = c[:, :Q_LORA], c[:, Q_LORA:Q_LORA + KV_LORA], c[:, Q_LORA + KV_LORA:Q_LORA + KV_LORA + LANE]
    qa = qa * lax.rsqrt(jnp.mean(qa * qa, axis=-1, keepdims=True) + EPS) * qag_ref[...]
    q = jnp.dot(qa.astype(BF16), wqb_ref[...], preferred_element_type=F32)
    ind = ind_ref[...]
    qss = _seg_sum(q * q, ind) * (1.0 / QK_DIM)
    qs = []
    for h in range(MLA_HEADS):
        qh = q[:, h * MLA_HP:(h + 1) * MLA_HP] * lax.rsqrt(qss[:, h:h + 1] + EPS) * qg_ref[...]
        qs.append(_rope_apply(qh, cos, sup, sdn))
    ckv = ckv * lax.rsqrt(jnp.mean(ckv * ckv, axis=-1, keepdims=True) + EPS) * kvg_ref[...]
    kn = jnp.dot(ckv.astype(BF16), wkb_ref[...], preferred_element_type=F32)
    pe2 = jnp.sum(kpe * kpe, axis=-1, keepdims=True)
    ksc = lax.rsqrt((_seg_sum(kn * kn, ind) + pe2) * (1.0 / QK_DIM) + EPS)
    kr = _rope_apply(pltpu.roll(kpe, NOPE_DIM, 1) * kgr_ref[...], cos, sup, sdn)
    return qs, ckv, kn, kr, ksc


def _mla_prompt_kernel(c_ref, qag_ref, wqb_ref, qg_ref, kvg_ref, wkb_ref, kgn_ref, kgr_ref, ind_ref,
                       cos_ref, sup_ref, sdn_ref, wvb_ref, q_ref, k_ref, v_ref, ckv_ref, ksc_ref):
    qs, ckv, kn, kr, ksc = _mla_common(c_ref, qag_ref, wqb_ref, qg_ref, kvg_ref, wkb_ref, kgn_ref, kgr_ref,
                                       ind_ref, cos_ref, sup_ref, sdn_ref)
    for h in range(MLA_HEADS):
        q_ref[h] = (qs[h] * ATTN_SCALE).astype(BF16)
        kh = (kn[:, h * MLA_HP:(h + 1) * MLA_HP] * kgn_ref[...] + kr) * ksc[:, h:h + 1]
        k_ref[h] = kh.astype(BF16)
    v_ref[...] = jnp.dot(ckv.astype(BF16), wvb_ref[...], preferred_element_type=F32).astype(BF16)
    ckv_ref[...] = ckv
    ksc_ref[...] = ksc


def _mla_weights(lp):
    pad_h = lambda w, d: jnp.pad(w.reshape(w.shape[0], MLA_HEADS, d), ((0, 0), (0, 0), (0, MLA_HP - d))
                                 ).reshape(w.shape[0], MLA_HEADS * MLA_HP).astype(BF16)
    gk = lp['mla_k_g']
    return dict(
        qag=lp['mla_qa_g'][None, :], wqb=pad_h(lp['mla_wqb'], QK_DIM),
        qg=jnp.pad(lp['mla_q_g'], (0, MLA_HP - QK_DIM))[None, :], kvg=lp['mla_kv_g'][None, :],
        wkb=pad_h(lp['mla_wkb'].reshape(KV_LORA, -1), NOPE_DIM),
        kgn=jnp.pad(gk[:NOPE_DIM], (0, MLA_HP - NOPE_DIM))[None, :],
        kgr=jnp.pad(gk[NOPE_DIM:], (NOPE_DIM, MLA_HP - QK_DIM))[None, :],
        ind=_head_ind(MLA_HEADS * MLA_HP, MLA_HP, MLA_HEADS),
        wvb=lp['mla_wvb'].reshape(KV_LORA, MLA_HEADS * V_DIM).astype(BF16))


def _mla_in_specs(tm, seq):
    tiles = max(seq // tm, 1)
    tab = pl.BlockSpec((tm, MLA_HP), lambda i: (i % tiles, 0))
    return [_tok_spec(tm, MLA_PAD), _full_spec((1, Q_LORA)), _full_spec((Q_LORA, MLA_HEADS * MLA_HP)),
            _full_spec((1, MLA_HP)), _full_spec((1, KV_LORA)), _full_spec((KV_LORA, MLA_HEADS * MLA_HP)),
            _full_spec((1, MLA_HP)), _full_spec((1, MLA_HP)), _full_spec((MLA_HEADS * MLA_HP, MLA_HEADS)),
            tab, tab, tab]


def mla_prompt_proj(mla_cols, pos, lp, B, seq, tm):
    T = B * seq
    w = _mla_weights(lp)
    tiles = seq // tm
    hspec = pl.BlockSpec((None, MLA_HEADS, tm, MLA_HP), lambda i: (i // tiles, 0, i % tiles, 0))
    q, k, v, ckv, ksc = pl.pallas_call(
        _mla_prompt_kernel,
        grid=(T // tm,),
        in_specs=_mla_in_specs(tm, seq) + [_full_spec((KV_LORA, MLA_HEADS * V_DIM))],
        out_specs=[hspec, hspec, _tok_spec(tm, MLA_HEADS * V_DIM), _tok_spec(tm, KV_LORA), _tok_spec(tm, MLA_HEADS)],
        out_shape=[jax.ShapeDtypeStruct((B, MLA_HEADS, seq, MLA_HP), BF16)] * 2
        + [jax.ShapeDtypeStruct((T, MLA_HEADS * V_DIM), BF16), jax.ShapeDtypeStruct((T, KV_LORA), F32),
           jax.ShapeDtypeStruct((T, MLA_HEADS), F32)],
        compiler_params=_cparams(("parallel",)),
        name="mla_prompt_proj",
    )(mla_cols, w['qag'], w['wqb'], w['qg'], w['kvg'], w['wkb'], w['kgn'], w['kgr'], w['ind'],
      *_rope_tables(pos), w['wvb'])
    return q, k, v.reshape(B, seq, -1), ckv, ksc


def _flash_kernel(q_ref, k_ref, v_ref, o_ref, m_scr, l_scr, acc_scr):
    qi, ki = pl.program_id(2), pl.program_id(3)
    tq, tk = q_ref.shape[1], k_ref.shape[1]

    @pl.when(ki == 0)
    def _():
        m_scr[...] = jnp.full(m_scr.shape, -jnp.inf, F32)
        l_scr[...] = jnp.zeros(l_scr.shape, F32)
        acc_scr[...] = jnp.zeros(acc_scr.shape, F32)

    def block(on_diagonal):
        for h in range(2):
            s = lax.dot_general(q_ref[h], k_ref[h], (((1,), (1,)), ((), ())), preferred_element_type=F32)
            if on_diagonal:
                keep = lax.broadcasted_iota(jnp.int32, (tq, tk), 1) <= lax.broadcasted_iota(jnp.int32, (tq, tk), 0)
                s = jnp.where(keep, s, -jnp.inf)
            m_old = m_scr[h]
            m_new = jnp.maximum(m_old, jnp.max(s, axis=-1, keepdims=True))
            alpha = jnp.exp(m_old - m_new)
            p = jnp.exp(s - m_new)
            l_scr[h] = alpha * l_scr[h] + jnp.sum(p, axis=-1, keepdims=True)
            acc_scr[h] = alpha * acc_scr[h] + jnp.dot(p.astype(BF16), v_ref[:, h * V_DIM:(h + 1) * V_DIM],
                                                      preferred_element_type=F32)
            m_scr[h] = m_new

    @pl.when(ki < qi)
    def _():
        block(False)

    @pl.when(ki == qi)
    def _():
        block(True)
        o_ref[...] = jnp.concatenate([acc_scr[h] / l_scr[h] for h in range(2)], axis=-1)


def flash_attention(q, k, v, tq):
    B, H, S, _ = q.shape
    n = S // tq
    return pl.pallas_call(
        _flash_kernel,
        grid=(B, H // 2, n, n),
        in_specs=[pl.BlockSpec((None, 2, tq, MLA_HP), lambda b, h, qi, ki: (b, h, qi, 0)),
                  pl.BlockSpec((None, 2, tq, MLA_HP), lambda b, h, qi, ki: (b, h, jnp.minimum(ki, qi), 0)),
                  pl.BlockSpec((None, tq, 2 * V_DIM), lambda b, h, qi, ki: (b, jnp.minimum(ki, qi), h))],
        out_specs=pl.BlockSpec((None, tq, 2 * V_DIM), lambda b, h, qi, ki: (b, qi, h)),
        out_shape=jax.ShapeDtypeStruct((B, S, H * V_DIM), F32),
        scratch_shapes=[pltpu.VMEM((2, tq, 1), F32), pltpu.VMEM((2, tq, 1), F32), pltpu.VMEM((2, tq, V_DIM), F32)],
        compiler_params=_cparams(("parallel", "parallel", "parallel", "arbitrary")),
        name="flash_attention",
    )(q, k, v)


MLA_QCAT = KV_LORA + MLA_HP


def _mla_sample_kernel(c_ref, qag_ref, wqb_ref, qg_ref, kvg_ref, wkb_ref, kgn_ref, kgr_ref, ind_ref,
                       cos_ref, sup_ref, sdn_ref, q_ref, knew_ref, ckv_ref, ksc_ref):
    qs, ckv, kn, kr, ksc = _mla_common(c_ref, qag_ref, wqb_ref, qg_ref, kvg_ref, wkb_ref, kgn_ref, kgr_ref,
                                       ind_ref, cos_ref, sup_ref, sdn_ref)
    tm = ckv.shape[0]
    rope_lanes = lax.broadcasted_iota(jnp.int32, (tm, MLA_HP), 1) >= NOPE_DIM
    wkb = wkb_ref[...]
    for h in range(MLA_HEADS):
        qh = qs[h] * ATTN_SCALE
        q_lat = lax.dot_general((qh * kgn_ref[...]).astype(BF16), wkb[:, h * MLA_HP:(h + 1) * MLA_HP],
                                (((1,), (1,)), ((), ())), preferred_element_type=F32)
        q_ref[:, h * MLA_QCAT:h * MLA_QCAT + KV_LORA] = q_lat.astype(BF16)
        q_ref[:, h * MLA_QCAT + KV_LORA:(h + 1) * MLA_QCAT] = jnp.where(rope_lanes, qh, 0.0).astype(BF16)
    knew_ref[:, :KV_LORA] = ckv.astype(BF16)
    knew_ref[:, KV_LORA:] = kr.astype(BF16)
    ckv_ref[...] = ckv
    ksc_ref[...] = ksc


def mla_sample_proj(mla_cols, pos, lp, seq, tm):
    T = mla_cols.shape[0]
    w = _mla_weights(lp)
    pos_tok = jnp.tile(pos, tm // seq)
    tables = _rope_tables(pos_tok)
    specs = _mla_in_specs(tm, tm)
    return pl.pallas_call(
        _mla_sample_kernel,
        grid=(T // tm,),
        in_specs=specs,
        out_specs=[_tok_spec(tm, MLA_HEADS * MLA_QCAT), _tok_spec(tm, MLA_QCAT), _tok_spec(tm, KV_LORA),
                   _tok_spec(tm, MLA_HEADS)],
        out_shape=[jax.ShapeDtypeStruct((T, MLA_HEADS * MLA_QCAT), BF16), jax.ShapeDtypeStruct((T, MLA_QCAT), BF16),
                   jax.ShapeDtypeStruct((T, KV_LORA), F32), jax.ShapeDtypeStruct((T, MLA_HEADS), F32)],
        compiler_params=_cparams(("parallel",)),
        name="mla_sample_proj",
    )(mla_cols, w['qag'], w['wqb'], w['qg'], w['kvg'], w['wkb'], w['kgn'], w['kgr'], w['ind'], *tables)


PAGES_PER_STEP = 16


def _paged_attn_kernel(pt_ref, q_ref, knew_ref, kscn_ref, cos_ref, sin_ref, kgr_ref, *rest):
    n = PAGES_PER_STEP
    c_refs, pe_refs, sc_refs = rest[:n], rest[n:2 * n], rest[2 * n:3 * n]
    o_ref, m_scr, l_scr, acc_scr = rest[3 * n:]
    j = pl.program_id(1)
    L = n * PAGE_SIZE
    rows = q_ref.shape[0]
    T = rows // MLA_HEADS
    half = ROPE_DIM // 2
    eye = (lax.broadcasted_iota(jnp.int32, (MLA_HEADS, MLA_HEADS), 0)
           == lax.broadcasted_iota(jnp.int32, (MLA_HEADS, MLA_HEADS), 1)).astype(F32)
    nt = (((1,), (1,)), ((), ()))

    @pl.when(j == 0)
    def _():
        m_scr[...] = jnp.full(m_scr.shape, -jnp.inf, F32)
        l_scr[...] = jnp.zeros(l_scr.shape, F32)
        acc_scr[...] = jnp.zeros(acc_scr.shape, F32)

    def update(s, value_fn):
        m_old = m_scr[...]
        m_new = jnp.maximum(m_old, jnp.max(s, axis=-1, keepdims=True))
        alpha = jnp.exp(m_old - m_new)
        p = jnp.exp(s - m_new)
        l_scr[...] = alpha * l_scr[...] + jnp.sum(p, axis=-1, keepdims=True)
        acc_scr[...] = alpha * acc_scr[...] + value_fn(p)
        m_scr[...] = m_new

    q_lat = q_ref[:, :KV_LORA]
    q_rope = q_ref[:, KV_LORA + NOPE_DIM:KV_LORA + QK_DIM]
    s_parts, c_parts = [], []
    for i in range(n):
        c_i = c_refs[i][...].astype(BF16)
        x = pe_refs[i][...] * kgr_ref[...]
        cs, sn = cos_ref[j * n + i], sin_ref[j * n + i]
        x1, x2 = x[:half], x[half:]
        kr = jnp.concatenate([x1 * cs - x2 * sn, x1 * sn + x2 * cs], axis=0).astype(BF16)
        s_i = (lax.dot_general(q_lat, c_i, nt, preferred_element_type=F32)
               + jnp.dot(q_rope, kr, preferred_element_type=F32))
        s_parts.append((s_i.reshape(T, MLA_HEADS, PAGE_SIZE) * sc_refs[i][...][None]).reshape(rows, PAGE_SIZE))
        c_parts.append(c_i)
    s = jnp.concatenate(s_parts, axis=1)
    update(s, lambda p: sum(jnp.dot(p[:, i * PAGE_SIZE:(i + 1) * PAGE_SIZE].astype(BF16), c_parts[i],
                                    preferred_element_type=F32) for i in range(n)))

    @pl.when(j == pl.num_programs(1) - 1)
    def _():
        knew = knew_ref[...]
        s_new = (lax.dot_general(q_lat, knew[:, :KV_LORA], nt, preferred_element_type=F32)
                 + lax.dot_general(q_rope, knew[:, KV_LORA + NOPE_DIM:KV_LORA + QK_DIM], nt,
                                   preferred_element_type=F32))
        kscn_t = lax.dot_general(eye, kscn_ref[...], nt, precision=lax.Precision.HIGHEST,
                                 preferred_element_type=F32)
        s_new = (s_new.reshape(T, MLA_HEADS, T) * kscn_t[None]).reshape(rows, T)
        q_tok = lax.broadcasted_iota(jnp.int32, (rows, T), 0) // MLA_HEADS
        s_new = jnp.where(lax.broadcasted_iota(jnp.int32, (rows, T), 1) <= q_tok, s_new, -jnp.inf)
        c_new = knew[:, :KV_LORA].astype(F32)

        def new_values(p):
            pb = p.astype(BF16).astype(F32)
            return sum(pb[:, t:t + 1] * c_new[t:t + 1, :] for t in range(T))

        update(s_new, new_values)
        o_ref[...] = acc_scr[...] / l_scr[...]


def paged_attention(q, knew, ksc_new, cache_ckv, cache_kpe_t, cache_ksc_t, page_table, layer, gk_rope):
    B, rows, _ = q.shape
    T = rows // MLA_HEADS
    n_pages = page_table.shape[1]
    n = PAGES_PER_STEP
    half = ROPE_DIM // 2
    inv = jnp.power(ROPE_BASE, -jnp.arange(half, dtype=F32) / half)
    pos = jnp.arange(n_pages * PAGE_SIZE).astype(F32).reshape(n_pages, 1, PAGE_SIZE)
    ang = pos * inv[None, :, None]
    cos3, sin3 = jnp.cos(ang), jnp.sin(ang)

    def page_spec(shape, i):
        return pl.BlockSpec((None, None) + shape, lambda b, j, pt: (layer, pt[b, j * n + i], 0, 0))

    grid_spec = pltpu.PrefetchScalarGridSpec(
        num_scalar_prefetch=1,
        grid=(B, n_pages // n),
        in_specs=[pl.BlockSpec((None, rows, MLA_QCAT), lambda b, j, pt: (b, 0, 0)),
                  pl.BlockSpec((None, T, MLA_QCAT), lambda b, j, pt: (b, 0, 0)),
                  pl.BlockSpec((None, T, MLA_HEADS), lambda b, j, pt: (b, 0, 0)),
                  pl.BlockSpec((n_pages, half, PAGE_SIZE), lambda b, j, pt: (0, 0, 0)),
                  pl.BlockSpec((n_pages, half, PAGE_SIZE), lambda b, j, pt: (0, 0, 0)),
                  pl.BlockSpec((ROPE_DIM, 1), lambda b, j, pt: (0, 0))]
        + [page_spec((PAGE_SIZE, KV_LORA), i) for i in range(n)]
        + [page_spec((ROPE_DIM, PAGE_SIZE), i) for i in range(n)]
        + [page_spec((MLA_HEADS, PAGE_SIZE), i) for i in range(n)],
        out_specs=pl.BlockSpec((None, rows, KV_LORA), lambda b, j, pt: (b, 0, 0)),
        scratch_shapes=[pltpu.VMEM((rows, 1), F32), pltpu.VMEM((rows, 1), F32), pltpu.VMEM((rows, KV_LORA), F32)],
    )
    return pl.pallas_call(
        _paged_attn_kernel,
        grid_spec=grid_spec,
        out_shape=jax.ShapeDtypeStruct((B, rows, KV_LORA), F32),
        compiler_params=_cparams(("parallel", "arbitrary")),
        name="paged_attention",
    )(page_table, q, knew, ksc_new, cos3, sin3, gk_rope[:, None],
      *([cache_ckv] * n), *([cache_kpe_t] * n), *([cache_ksc_t] * n))


def _vproj_kernel(o_ref, w_ref, out_ref):
    out_ref[...] = jnp.concatenate(
        [jnp.dot(o_ref[:, h * KV_LORA:(h + 1) * KV_LORA].astype(BF16), w_ref[h], preferred_element_type=F32)
         for h in range(MLA_HEADS)], axis=-1)


def mla_value_proj(o_lat, wvb):
    T = o_lat.shape[0]
    w = wvb.transpose(1, 0, 2).astype(BF16)
    return pl.pallas_call(
        _vproj_kernel,
        grid=(1,),
        in_specs=[_full_spec((T, MLA_HEADS * KV_LORA)), _full_spec((MLA_HEADS, KV_LORA, V_DIM))],
        out_specs=_full_spec((T, MLA_HEADS * V_DIM)),
        out_shape=jax.ShapeDtypeStruct((T, MLA_HEADS * V_DIM), F32),
        compiler_params=_cparams(("arbitrary",)),
        name="mla_value_proj",
    )(o_lat, w)


SSM_BC = SSM_GROUPS * SSM_STATE


def _softplus(x):
    return jnp.maximum(x, 0.0) + jnp.log(1.0 + jnp.exp(-jnp.abs(x)))


def _silu(x):
    return x * jax.nn.sigmoid(x)


def _gated_norm(y, z, g):
    y = y * _silu(z)
    return y * lax.rsqrt(jnp.mean(y * y, axis=-1, keepdims=True) + EPS) * g


def _ssd_prompt_kernel(c_ref, buf_ref, s0_ref, cw_ref, cb_ref, dtb_ref, a_ref, d_ref, ng_ref,
                       o_ref, conv_ref, sT_ref, carry_scr, s_scr):
    ci = pl.program_id(1)
    Lc = c_ref.shape[0]

    @pl.when(ci == 0)
    def _():
        carry_scr[...] = buf_ref[...]
        s_scr[...] = s0_ref[...]

    cols = c_ref[...]
    z = cols[:, :SSM_W]
    xbc = cols[:, SSM_W:SSM_W + CONV_CH]
    dtr = cols[:, SSM_W + CONV_CH:SSM_W + CONV_CH + LANE]
    ext = jnp.concatenate([carry_scr[...], xbc], axis=0)
    conv = cb_ref[...]
    for i in range(CONV_W):
        lo = 8 - (CONV_W - 1) + i
        conv = conv + ext[lo:lo + Lc, :] * cw_ref[i:i + 1, :]
    carry_scr[...] = xbc[Lc - 8:, :]
    conv = _silu(conv)
    xs, Bm, Cm = conv[:, :SSM_W], conv[:, SSM_W:SSM_W + SSM_BC], conv[:, SSM_W + SSM_BC:]
    dt = _softplus(dtr + dtb_ref[...])
    a = dt * a_ref[...]
    ri = lax.broadcasted_iota(jnp.int32, (Lc, Lc), 0)
    cj = lax.broadcasted_iota(jnp.int32, (Lc, Lc), 1)
    causal = cj <= ri
    acs = jnp.dot(causal.astype(F32), a, precision=lax.Precision.HIGHEST, preferred_element_type=F32)
    acs_t = acs.T
    nt = (((1,), (1,)), ((), ()))
    ys = []
    for g in range(SSM_GROUPS):
        Bg = Bm[:, g * SSM_STATE:(g + 1) * SSM_STATE].astype(BF16)
        Cg = Cm[:, g * SSM_STATE:(g + 1) * SSM_STATE].astype(BF16)
        cb = lax.dot_general(Cg, Bg, nt, preferred_element_type=F32)
        for e in range(SSM_EPG):
            hd = g * SSM_EPG + e
            col = acs[:, hd:hd + 1]
            lmat = jnp.exp(jnp.where(causal, col - acs_t[hd:hd + 1, :], -jnp.inf))
            x_e = xs[:, hd * SSM_HD:(hd + 1) * SSM_HD]
            xdt = x_e * dt[:, hd:hd + 1]
            y = jnp.dot((cb * lmat).astype(BF16), xdt.astype(BF16), preferred_element_type=F32)
            last = acs[Lc - 1:Lc, hd:hd + 1]
            s_in = s_scr[hd]
            y = y + lax.dot_general(Cg, s_in.astype(BF16), nt, preferred_element_type=F32) * jnp.exp(col)
            st = lax.dot_general((xdt * jnp.exp(last - col)).astype(BF16), Bg, (((0,), (0,)), ((), ())),
                                 preferred_element_type=F32)
            s_scr[hd] = s_in * jnp.exp(last) + st
            ys.append(y + x_e * d_ref[:, hd:hd + 1])
    o_ref[...] = _gated_norm(jnp.concatenate(ys, axis=-1), z, ng_ref[...])

    @pl.when(ci == pl.num_programs(1) - 1)
    def _():
        conv_ref[...] = xbc[Lc - (CONV_W - 1):, :]
        sT_ref[...] = s_scr[...]


def _ssm_params(lp):
    pad = lambda v: jnp.pad(v, (0, LANE - SSM_HEADS))[None, :]
    return dict(cw=lp['ssm_conv_w'], cb=lp['ssm_conv_b'][None, :], dtb=pad(lp['ssm_dt_bias']),
                a=pad(-jnp.exp(lp['ssm_a_log'])), d=pad(lp['ssm_d']), ng=lp['ssm_norm_g'][None, :])


def ssd_prompt(ssm_cols, conv_buf, s0, lp, B, seq):
    Lc = SSD_CHUNK
    nc = seq // Lc
    p = _ssm_params(lp)
    buf8 = jnp.pad(conv_buf, ((0, 0), (8 - (CONV_W - 1), 0), (0, 0)))
    st_spec = pl.BlockSpec((None, SSM_HEADS, SSM_HD, SSM_STATE), lambda b, c: (b, 0, 0, 0))
    return pl.pallas_call(
        _ssd_prompt_kernel,
        grid=(B, nc),
        in_specs=[pl.BlockSpec((Lc, SSM_PAD), lambda b, c: (b * nc + c, 0)),
                  pl.BlockSpec((None, 8, CONV_CH), lambda b, c: (b, 0, 0)), st_spec,
                  _full_spec((CONV_W, CONV_CH)), _full_spec((1, CONV_CH)), _full_spec((1, LANE)),
                  _full_spec((1, LANE)), _full_spec((1, LANE)), _full_spec((1, SSM_W))],
        out_specs=[pl.BlockSpec((Lc, SSM_W), lambda b, c: (b * nc + c, 0)),
                   pl.BlockSpec((None, CONV_W - 1, CONV_CH), lambda b, c: (b, 0, 0)), st_spec],
        out_shape=[jax.ShapeDtypeStruct((B * seq, SSM_W), F32), jax.ShapeDtypeStruct((B, CONV_W - 1, CONV_CH), F32),
                   jax.ShapeDtypeStruct((B, SSM_HEADS, SSM_HD, SSM_STATE), F32)],
        scratch_shapes=[pltpu.VMEM((8, CONV_CH), F32), pltpu.VMEM((SSM_HEADS, SSM_HD, SSM_STATE), F32)],
        compiler_params=_cparams(("parallel", "arbitrary")),
        name="ssd_prompt",
    )(ssm_cols, buf8, s0, p['cw'], p['cb'], p['dtb'], p['a'], p['d'], p['ng'])


def _ssd_step_prep_kernel(c_ref, x0_ref, x1_ref, x2_ref, cw_ref, cb_ref, dtb_ref, a_ref, exp_ref, expg_ref,
                          w_ref, k_ref, r_ref, v_ref, xs_ref):
    cols = c_ref[...]
    xbc = cols[:, SSM_W:SSM_W + CONV_CH]
    dtr = cols[:, SSM_W + CONV_CH:SSM_W + CONV_CH + LANE]
    conv = cb_ref[...] + x0_ref[...] * cw_ref[0:1, :] + x1_ref[...] * cw_ref[1:2, :] + x2_ref[...] * cw_ref[2:3, :] \
        + xbc * cw_ref[3:4, :]
    conv = _silu(conv)
    xs, Bm, Cm = conv[:, :SSM_W], conv[:, SSM_W:SSM_W + SSM_BC], conv[:, SSM_W + SSM_BC:]
    dt = _softplus(dtr + dtb_ref[...])
    hi = lax.Precision.HIGHEST
    dt_h = jnp.dot(dt, exp_ref[...], precision=hi, preferred_element_type=F32)
    a_h = jnp.dot(dt * a_ref[...], exp_ref[...], precision=hi, preferred_element_type=F32)
    w_ref[...] = jnp.exp(a_h)
    k_ref[...] = jnp.dot(Bm, expg_ref[...], precision=hi, preferred_element_type=F32)
    r_ref[...] = jnp.dot(Cm, expg_ref[...], precision=hi, preferred_element_type=F32)
    v_ref[...] = xs * dt_h
    xs_ref[...] = xs


def _ssd_step_post_kernel(y_ref, xs_ref, c_ref, dh_ref, ng_ref, o_ref):
    y = y_ref[...] + xs_ref[...] * dh_ref[...]
    o_ref[...] = _gated_norm(y, c_ref[:, :SSM_W], ng_ref[...])


def ssd_decode(ssm_cols, conv_buf, s0, lp, B, seq, tm):
    T = B * seq
    p = _ssm_params(lp)
    xbc = ssm_cols[:, SSM_W:SSM_W + CONV_CH].reshape(B, seq, CONV_CH)
    xpad = jnp.concatenate([conv_buf, xbc], axis=1)
    shifted = [xpad[:, i:i + seq].reshape(T, CONV_CH) for i in range(CONV_W - 1)]
    head_of_lane = jnp.arange(SSM_W) // SSM_HD
    expand = (jnp.arange(LANE)[:, None] == head_of_lane[None, :]).astype(F32)
    src = (head_of_lane // SSM_EPG) * SSM_STATE + jnp.arange(SSM_W) % SSM_STATE
    expand_g = (jnp.arange(SSM_BC)[:, None] == src[None, :]).astype(F32)
    out = jax.ShapeDtypeStruct((T, SSM_W), F32)
    w, k, r, v, xs = pl.pallas_call(
        _ssd_step_prep_kernel,
        grid=(T // tm,),
        in_specs=[_tok_spec(tm, SSM_PAD)] + [_tok_spec(tm, CONV_CH)] * 3
        + [_full_spec((CONV_W, CONV_CH)), _full_spec((1, CONV_CH)), _full_spec((1, LANE)), _full_spec((1, LANE)),
           _full_spec((LANE, SSM_W)), _full_spec((SSM_BC, SSM_W))],
        out_specs=[_tok_spec(tm, SSM_W)] * 5,
        out_shape=[out] * 5,
        compiler_params=_cparams(("parallel",)),
        name="ssd_step_prep",
    )(ssm_cols, *shifted, p['cw'], p['cb'], p['dtb'], p['a'], expand, expand_g)
    sh = lambda t: t.reshape(B, seq, SSM_W)
    zeros = jnp.zeros((B, seq, SSM_W), F32)
    y, sT = state_scan(sh(w), zeros, zeros, sh(k), sh(r), sh(v), s0, seq)
    d_h = jnp.repeat(lp['ssm_d'], SSM_HD)[None, :]
    o = pl.pallas_call(
        _ssd_step_post_kernel,
        grid=(T // tm,),
        in_specs=[_tok_spec(tm, SSM_W), _tok_spec(tm, SSM_W), _tok_spec(tm, SSM_PAD), _full_spec((1, SSM_W)),
                  _full_spec((1, SSM_W))],
        out_specs=_tok_spec(tm, SSM_W),
        out_shape=out,
        compiler_params=_cparams(("parallel",)),
        name="ssd_step_post",
    )(y.reshape(T, SSM_W), xs, ssm_cols, d_h, p['ng'])
    return o, xpad[:, -(CONV_W - 1):], sT


def _merge_kernel(oa_ref, ob_ref, oc_ref, gate_ref, x_ref, g1_ref, wb_ref, wo_ref, o_ref):
    D = x_ref.shape[1]
    acc = jnp.zeros(x_ref.shape, F32)
    for i, ref in enumerate((oa_ref, ob_ref, oc_ref)):
        u = jnp.dot(ref[...].astype(BF16), wb_ref[i], preferred_element_type=F32)
        acc = acc + jax.nn.sigmoid(gate_ref[:, i * D:(i + 1) * D]) * u
    y = jnp.dot(acc.astype(BF16), wo_ref[...], preferred_element_type=F32)
    o_ref[...] = x_ref[...] + g1_ref[...] * y


def branch_merge(o_a, o_b, o_c, gates, x2, g1, wb_bf, wo_bf, seq, tm):
    T, D = x2.shape
    return pl.pallas_call(
        _merge_kernel,
        grid=(T // tm,),
        in_specs=[_tok_spec(tm, BR_W)] * 3 + [_tok_spec(tm, N_BRANCH * D), _tok_spec(tm, D), _row_spec(seq, tm, D),
                                              _full_spec((N_BRANCH, BR_W, D)), _full_spec((D, D))],
        out_specs=_tok_spec(tm, D),
        out_shape=jax.ShapeDtypeStruct((T, D), F32),
        compiler_params=_cparams(("parallel",)),
        name="branch_merge",
    )(o_a, o_b, o_c, gates, x2, _row_operand(g1, seq, tm), wb_bf, wo_bf)


TM_TOKENS = 512
TM_INPROJ = 256
TM_ROUTE = 256
SCAN_CHUNK = 32
FLASH_BLOCK = 1024


def _layer(x2, B, seq, c, lw, st, pos, layer, decode_ctx):
    T = B * seq
    tm = min(TM_TOKENS, T)
    sh1, sc1, g1, sh2, sc2, g2 = jnp.split(ada_mod(c, lw['ada_w'], lw['ada_b'], layer), 6, axis=-1)
    lp = lw['lp']
    rw, mla, ssm, gates = in_proj(x2, sc1, sh1, lp['norm1_g'], lw['w_in_pad'], seq, min(TM_INPROJ, T))
    o_a, rw_shift, rw_state = rwkv_branch(rw, B, seq, st[0], st[1], lp, tm, min(SCAN_CHUNK, seq))
    if decode_ctx is None:
        q, k, v, ckv, ksc = mla_prompt_proj(mla, pos, lp, B, seq, min(tm, seq))
        o_b = flash_attention(q, k, v, min(FLASH_BLOCK, seq)).reshape(T, BR_W)
        o_c, conv_buf, ssm_state = ssd_prompt(ssm, st[2], st[3], lp, B, seq)
    else:
        cache_ckv, cache_kpe, cache_ksc, page_table = decode_ctx
        q, knew, ckv, ksc = mla_sample_proj(mla, pos, lp, seq, tm)
        o_lat = paged_attention(q.reshape(B, seq * MLA_HEADS, MLA_QCAT), knew.reshape(B, seq, MLA_QCAT),
                                ksc.reshape(B, seq, MLA_HEADS), cache_ckv, cache_kpe, cache_ksc, page_table,
                                layer, lp['mla_k_g'][NOPE_DIM:])
        o_b = mla_value_proj(o_lat.reshape(T, MLA_HEADS * KV_LORA), lp['mla_wvb'])
        o_c, conv_buf, ssm_state = ssd_decode(ssm, st[2], st[3], lp, B, seq, tm)
    x2 = branch_merge(o_a, o_b, o_c, gates, x2, g1, lw['wb_bf'], lw['wo_bf'], seq, tm)
    h, eid, gate = peer_route(x2, sc2, sh2, lp['norm2_g'], lw['wq_bf'], lw['keys_bf'], seq, min(TM_ROUTE, T))
    x2 = peer_gather(eid, h, gate, x2, g2, seq, lw['uv'])
    kpe = mla[:, Q_LORA + KV_LORA:MLA_COLS]
    return x2, (ckv.reshape(B, seq, KV_LORA), kpe.reshape(B, seq, ROPE_DIM), ksc.reshape(B, seq, MLA_HEADS),
                rw_shift, rw_state, conv_buf, ssm_state)


def kernel(x_prompt, x_sample, cache_ckv, cache_kpe, cache_kscale, state_rwkv_shift, state_rwkv_wkv,
           state_ssm_conv, state_ssm, page_table, c_prompt, c_sample, ada_w, ada_b, norm1_g, norm2_g, w_in,
           rw_mu, rw_w0, rw_w2, rw_a0, rw_a2, rw_g2, rw_kk, rw_ka, rw_rk, rw_ln_w, rw_ln_b,
           mla_qa_g, mla_wqb, mla_kv_g, mla_wkb, mla_wvb, mla_q_g, mla_k_g,
           ssm_conv_w, ssm_conv_b, ssm_dt_bias, ssm_a_log, ssm_d, ssm_norm_g,
           w_branch, w_out, pk_wq, pk_keys, pk_u, pk_v):
    small = dict(norm1_g=norm1_g, norm2_g=norm2_g,
                 rw_mu=rw_mu, rw_w0=rw_w0, rw_w2=rw_w2, rw_a0=rw_a0, rw_a2=rw_a2, rw_g2=rw_g2,
                 rw_kk=rw_kk, rw_ka=rw_ka, rw_rk=rw_rk, rw_ln_w=rw_ln_w, rw_ln_b=rw_ln_b,
                 mla_qa_g=mla_qa_g, mla_wqb=mla_wqb, mla_kv_g=mla_kv_g, mla_wkb=mla_wkb, mla_wvb=mla_wvb,
                 mla_q_g=mla_q_g, mla_k_g=mla_k_g,
                 ssm_conv_w=ssm_conv_w, ssm_conv_b=ssm_conv_b, ssm_dt_bias=ssm_dt_bias, ssm_a_log=ssm_a_log,
                 ssm_d=ssm_d, ssm_norm_g=ssm_norm_g)
    depth = ada_w.shape[0]
    Bp, S, D = x_prompt.shape
    Bd, T, _ = x_sample.shape
    assert SSD_CHUNK % 8 == 0 and S % SSD_CHUNK == 0 and T < SSD_CHUNK
    past_len = page_table.shape[1] * PAGE_SIZE
    pos_p = jnp.arange(S)
    pos_s = past_len + jnp.arange(T)
    zero_state = (jnp.zeros((Bp, RW_COLS), F32), jnp.zeros((Bp, RW_HEADS, RW_HD, RW_HD), F32),
                  jnp.zeros((Bp, CONV_W - 1, CONV_CH), F32), jnp.zeros((Bp, SSM_HEADS, SSM_HD, SSM_STATE), F32))
    hp, hs = x_prompt.reshape(Bp * S, D), x_sample.reshape(Bd * T, D)
    decode_ctx = (cache_ckv, jnp.swapaxes(cache_kpe, -1, -2), jnp.swapaxes(cache_kscale, -1, -2), page_table)
    new_p, new_s = [], []
    for l in range(depth):
        lw = dict(lp={name: arr[l] for name, arr in small.items()}, ada_w=ada_w, ada_b=ada_b,
                  w_in_pad=_pad_w_in(w_in[l]), wb_bf=w_branch[l].astype(BF16), wo_bf=w_out[l].astype(BF16),
                  wq_bf=pk_wq[l].astype(BF16),
                  keys_bf=pk_keys[l].reshape(2 * PK_HEADS, PK_NKEYS, PK_DHALF).astype(BF16),
                  uv=peer_table(pk_u[l], pk_v[l]))
        hp, st_p = _layer(hp, Bp, S, c_prompt, lw, zero_state, pos_p, l, None)
        st_in = (state_rwkv_shift[l], state_rwkv_wkv[l], state_ssm_conv[l], state_ssm[l])
        hs, st_s = _layer(hs, Bd, T, c_sample, lw, st_in, pos_s, l, decode_ctx)
        new_p.append(st_p)
        new_s.append(st_s)
    hp, hs = hp.reshape(Bp, S, D), hs.reshape(Bd, T, D)
    return (hp, hs,
            jnp.stack([s[0] for s in new_p]), jnp.stack([s[0] for s in new_s]),
            jnp.stack([s[1] for s in new_p]), jnp.stack([s[1] for s in new_s]),
            jnp.stack([s[2] for s in new_p]), jnp.stack([s[2] for s in new_s]),
            jnp.stack([s[3] for s in new_p]), jnp.stack([s[3] for s in new_s]),
            jnp.stack([s[4] for s in new_p]), jnp.stack([s[4] for s in new_s]),
            jnp.stack([s[5] for s in new_p]), jnp.stack([s[5] for s in new_s]),
            jnp.stack([s[6] for s in new_p]), jnp.stack([s[6] for s in new_s]))
```

```python
import math, functools
import jax, jax.numpy as jnp
from jax import lax
from jax.experimental import pallas as pl
from jax.experimental.pallas import tpu as pltpu

D_MODEL = 1024
PAGE_SIZE = 128
EPS = 1e-6
F32 = jnp.float32
BF16 = jnp.bfloat16
RW_HEADS = 8
RW_HD = 64
RW_W = RW_HEADS * RW_HD
RW_DECAY_LORA = 64
RW_AAA_LORA = 64
RW_GATE_LORA = 128
RW_COLS = 3 * RW_W + RW_DECAY_LORA + RW_AAA_LORA + RW_GATE_LORA
RW_SPLITS = (RW_W, 2 * RW_W, 3 * RW_W, 3 * RW_W + RW_DECAY_LORA, 3 * RW_W + RW_DECAY_LORA + RW_AAA_LORA)
RW_GN_EPS = 64e-5
MLA_HEADS = 8
Q_LORA = 256
KV_LORA = 256
NOPE_DIM = 64
ROPE_DIM = 32
V_DIM = 64
QK_DIM = NOPE_DIM + ROPE_DIM
MLA_COLS = Q_LORA + KV_LORA + ROPE_DIM
ROPE_BASE = 10000.0
ATTN_SCALE = QK_DIM ** -0.5
Q_BLOCK = 128
SSM_HEADS = 8
SSM_HD = 64
SSM_W = SSM_HEADS * SSM_HD
SSM_GROUPS = 2
SSM_EPG = SSM_HEADS // SSM_GROUPS
SSM_STATE = 64
CONV_W = 4
CONV_CH = SSM_W + 2 * SSM_GROUPS * SSM_STATE
SSM_COLS = SSM_W + CONV_CH + SSM_HEADS
SSD_CHUNK = 128
N_BRANCH = 3
BR_W = RW_W
IN_COLS = RW_COLS + MLA_COLS + SSM_COLS + N_BRANCH * D_MODEL
IN_SPLITS = (RW_COLS, RW_COLS + MLA_COLS, RW_COLS + MLA_COLS + SSM_COLS)
PK_HEADS = 8
PK_DKEY = 256
PK_DHALF = PK_DKEY // 2
PK_NKEYS = 128
PK_TOPK = 16
PK_EXPERTS = PK_NKEYS * PK_NKEYS
PK_TOK_BLOCK = 256

LANE = 128
VMEM_LIMIT = 56 * 1024 * 1024

MLA_PAD = 640
SSM_PAD = 1408
PROJ_OFF = (0, RW_COLS, RW_COLS + MLA_PAD, RW_COLS + MLA_PAD + SSM_PAD)
PROJ_COLS = PROJ_OFF[3] + N_BRANCH * D_MODEL


def _cparams(sem):
    return pltpu.CompilerParams(dimension_semantics=sem, vmem_limit_bytes=VMEM_LIMIT)


def _pad_w_in(w_in):
    rw, mla, ssm, gate = jnp.split(w_in, IN_SPLITS, axis=-1)
    mla = jnp.pad(mla, ((0, 0), (0, MLA_PAD - MLA_COLS)))
    ssm = jnp.pad(ssm, ((0, 0), (0, SSM_PAD - SSM_COLS)))
    return jnp.concatenate([rw, mla, ssm, gate], axis=-1).astype(BF16)


PROJ_WIDTHS = (RW_COLS, MLA_PAD, SSM_PAD, N_BRANCH * D_MODEL)


def _inproj_kernel(x_ref, sc_ref, sh_ref, g_ref, w_rw, w_mla, w_ssm, w_gate, o_rw, o_mla, o_ssm, o_gate):
    x = x_ref[...]
    y = x * lax.rsqrt(jnp.mean(x * x, axis=-1, keepdims=True) + EPS) * g_ref[...]
    h = (y * (1.0 + sc_ref[...]) + sh_ref[...]).astype(BF16)
    for w_ref, o_ref in ((w_rw, o_rw), (w_mla, o_mla), (w_ssm, o_ssm), (w_gate, o_gate)):
        o_ref[...] = jnp.dot(h, w_ref[...], preferred_element_type=F32)


def _row_spec(rows_per_group, tm, d):
    if rows_per_group >= tm:
        tiles = rows_per_group // tm
        return pl.BlockSpec((None, 1, d), lambda i, *_: (i // tiles, 0, 0))
    return pl.BlockSpec((None, tm, d), lambda i, *_: (i, 0, 0))


def _row_operand(v, seq, tm):
    if seq >= tm:
        return v[:, None, :]
    return jnp.repeat(v, seq, axis=0).reshape(-1, tm, v.shape[-1])


def in_proj(x2, sc, sh, g, w_pad, seq, tm):
    T, D = x2.shape
    ws = [w_pad[:, PROJ_OFF[i]:PROJ_OFF[i] + PROJ_WIDTHS[i]] for i in range(4)]
    return pl.pallas_call(
        _inproj_kernel,
        grid=(T // tm,),
        in_specs=[_tok_spec(tm, D), _row_spec(seq, tm, D), _row_spec(seq, tm, D), _full_spec((1, D))]
        + [_full_spec((D, n)) for n in PROJ_WIDTHS],
        out_specs=[_tok_spec(tm, n) for n in PROJ_WIDTHS],
        out_shape=[jax.ShapeDtypeStruct((T, n), F32) for n in PROJ_WIDTHS],
        compiler_params=_cparams(("parallel",)),
        name="in_proj",
    )(x2, _row_operand(sc, seq, tm), _row_operand(sh, seq, tm), g[None, :], *ws)


def _seg_sum(x, seg):
    hi = x.astype(BF16)
    lo = (x - hi.astype(F32)).astype(BF16)
    return (jnp.dot(hi, seg, preferred_element_type=F32) + jnp.dot(lo, seg, preferred_element_type=F32))


def _seg_matrix(width, seg):
    i = jnp.arange(width) // seg
    return (i[:, None] == i[None, :]).astype(BF16)


def _gelu_tanh(x):
    return 0.5 * x * (1.0 + jnp.tanh(math.sqrt(2.0 / math.pi) * (x + 0.044715 * (x * x * x))))


def _tok_spec(tm, d, col_block=0):
    return pl.BlockSpec((tm, d), lambda i, *_: (i, col_block))


def _full_spec(shape):
    n = len(shape)
    return pl.BlockSpec(shape, lambda *_: (0,) * n)


def _rwkv_prep_kernel(c_ref, xp_ref, mu_ref, w0_ref, w2_ref, a0_ref, a2_ref, g2_ref, kk_ref, ka_ref, seg_ref,
                      r_ref, w_ref, k_ref, v_ref, a_ref, b_ref, g_ref):
    c = c_ref[...]
    m = c + (xp_ref[...] - c) * mu_ref[...]
    r, k, v = m[:, :RW_W], m[:, RW_W:2 * RW_W], m[:, 2 * RW_W:3 * RW_W]
    wl = m[:, RW_SPLITS[2]:RW_SPLITS[3]]
    al = m[:, RW_SPLITS[3]:RW_SPLITS[4]]
    gl = m[:, RW_SPLITS[4]:]
    nz = -(w0_ref[...] + jnp.dot(jnp.tanh(wl), w2_ref[...], preferred_element_type=F32))
    softplus = jnp.maximum(nz, 0.0) + jnp.log(1.0 + jnp.exp(-jnp.abs(nz)))
    decay = jnp.exp(-jnp.exp(-softplus - 0.5))
    a = jax.nn.sigmoid(a0_ref[...] + jnp.dot(al, a2_ref[...], preferred_element_type=F32))
    g = jnp.dot(jax.nn.sigmoid(gl), g2_ref[...], preferred_element_type=F32)
    kk = k * kk_ref[...]
    k = k * (1.0 + (a - 1.0) * ka_ref[...])
    kk = kk / jnp.maximum(jnp.sqrt(_seg_sum(kk * kk, seg_ref[...])), 1e-12)
    r_ref[...] = r
    w_ref[...] = decay
    k_ref[...] = k
    v_ref[...] = v
    a_ref[...] = -kk
    b_ref[...] = kk * a
    g_ref[...] = g


def rwkv_prep(proj, xprev, lp, tm):
    T = proj.shape[0]
    row = lambda v: v.reshape(1, -1)
    out = jax.ShapeDtypeStruct((T, RW_W), F32)
    return pl.pallas_call(
        _rwkv_prep_kernel,
        grid=(T // tm,),
        in_specs=[_tok_spec(tm, RW_COLS), _tok_spec(tm, RW_COLS), _full_spec((1, RW_COLS)),
                  _full_spec((1, RW_W)), _full_spec((RW_DECAY_LORA, RW_W)),
                  _full_spec((1, RW_W)), _full_spec((RW_AAA_LORA, RW_W)),
                  _full_spec((RW_GATE_LORA, RW_W)), _full_spec((1, RW_W)), _full_spec((1, RW_W)),
                  _full_spec((RW_W, RW_W))],
        out_specs=[_tok_spec(tm, RW_W)] * 7,
        out_shape=[out] * 7,
        compiler_params=_cparams(("parallel",)),
        name="rwkv_prep",
    )(proj, xprev, row(lp['rw_mu']), row(lp['rw_w0']), lp['rw_w2'], row(lp['rw_a0']), lp['rw_a2'], lp['rw_g2'],
      row(lp['rw_kk']), row(lp['rw_ka']), _seg_matrix(RW_W, RW_HD))


SCAN_HEADS = 32
SCAN_VM = LANE // SCAN_HEADS
SCAN_VD = RW_HD // SCAN_VM


def _scan_kernel(w_ref, a_ref, b_ref, k_ref, r_ref, v_ref, s0_ref, y_ref, sT_ref, s_scr):
    c = pl.program_id(1)

    @pl.when(c == 0)
    def _():
        s_scr[...] = s0_ref[...]

    def step(t, carry):
        A, W, Bm, K, R = a_ref[t], w_ref[t], b_ref[t], k_ref[t], r_ref[t]
        V = v_ref[t]
        ys = []
        for vd in range(SCAN_VD):
            S = s_scr[vd]
            sa = jnp.sum(S * A, axis=0, keepdims=True)
            S = S * W + sa * Bm + V[vd:vd + 1, :] * K
            s_scr[vd] = S
            ys.append(jnp.sum(S * R, axis=0, keepdims=True))
        y_ref[t] = jnp.concatenate(ys, axis=0)
        return carry

    lax.fori_loop(0, y_ref.shape[0], step, 0)

    @pl.when(c == pl.num_programs(1) - 1)
    def _():
        sT_ref[...] = s_scr[...]


def _to_scan_kh(x, n_groups):
    B, T, _ = x.shape
    x = x.reshape(B, T, RW_HEADS, RW_HD).transpose(1, 3, 0, 2).reshape(T, RW_HD, n_groups, SCAN_HEADS)
    x = jnp.broadcast_to(x[..., None], (T, RW_HD, n_groups, SCAN_HEADS, SCAN_VM))
    return x.transpose(2, 0, 1, 3, 4).reshape(n_groups, T, RW_HD, LANE)


def _to_scan_v(x, n_groups):
    B, T, _ = x.shape
    x = x.reshape(B, T, RW_HEADS, SCAN_VD, SCAN_VM).transpose(1, 3, 0, 2, 4)
    x = x.reshape(T, SCAN_VD, n_groups, SCAN_HEADS * SCAN_VM)
    return x.transpose(2, 0, 1, 3)


def _from_scan_v(y, B):
    G, T = y.shape[:2]
    y = y.transpose(1, 2, 0, 3).reshape(T, SCAN_VD, B, RW_HEADS, SCAN_VM)
    return y.transpose(2, 0, 3, 1, 4).reshape(B, T, RW_W)


def _state_to_scan(s, n_groups):
    B = s.shape[0]
    s = s.reshape(B, RW_HEADS, SCAN_VD, SCAN_VM, RW_HD).transpose(2, 4, 0, 1, 3)
    s = s.reshape(SCAN_VD, RW_HD, n_groups, SCAN_HEADS * SCAN_VM)
    return s.transpose(2, 0, 1, 3)


def _state_from_scan(s, B):
    s = s.transpose(1, 2, 0, 3).reshape(SCAN_VD, RW_HD, B, RW_HEADS, SCAN_VM)
    return s.transpose(2, 3, 0, 4, 1).reshape(B, RW_HEADS, RW_HD, RW_HD)


def state_scan(w, a, b, k, r, v, s0, tc):
    B, T, _ = w.shape
    G = (B * RW_HEADS) // SCAN_HEADS
    kh = [_to_scan_kh(t, G) for t in (w, a, b, k, r)]
    vv = _to_scan_v(v, G)
    ss = _state_to_scan(s0, G)
    kh_spec = pl.BlockSpec((None, tc, RW_HD, LANE), lambda g, c: (g, c, 0, 0))
    v_spec = pl.BlockSpec((None, tc, SCAN_VD, LANE), lambda g, c: (g, c, 0, 0))
    s_spec = pl.BlockSpec((None, SCAN_VD, RW_HD, LANE), lambda g, c: (g, 0, 0, 0))
    y, sT = pl.pallas_call(
        _scan_kernel,
        grid=(G, T // tc),
        in_specs=[kh_spec] * 5 + [v_spec, s_spec],
        out_specs=[v_spec, s_spec],
        out_shape=[jax.ShapeDtypeStruct((G, T, SCAN_VD, LANE), F32),
                   jax.ShapeDtypeStruct((G, SCAN_VD, RW_HD, LANE), F32)],
        scratch_shapes=[pltpu.VMEM((SCAN_VD, RW_HD, LANE), F32)],
        compiler_params=_cparams(("parallel", "arbitrary")),
        name="state_scan",
    )(*kh, vv, ss)
    return _from_scan_v(y, B), _state_from_scan(sT, B)


def _rwkv_post_kernel(y_ref, r_ref, k_ref, v_ref, g_ref, lnw_ref, lnb_ref, rk_ref, seg_ref, o_ref):
    seg = seg_ref[...]
    y = y_ref[...]
    d = y - _seg_sum(y, seg) * (1.0 / RW_HD)
    var = _seg_sum(d * d, seg) * (1.0 / RW_HD)
    yn = d * lax.rsqrt(var + RW_GN_EPS) * lnw_ref[...] + lnb_ref[...]
    bonus = _seg_sum(r_ref[...] * k_ref[...] * rk_ref[...], seg) * v_ref[...]
    o_ref[...] = (yn + bonus) * g_ref[...]


def rwkv_post(y, r, k, v, g, lp, tm):
    T = y.shape[0]
    row = lambda t: t.reshape(1, -1)
    return pl.pallas_call(
        _rwkv_post_kernel,
        grid=(T // tm,),
        in_specs=[_tok_spec(tm, RW_W)] * 5 + [_full_spec((1, RW_W))] * 3 + [_full_spec((RW_W, RW_W))],
        out_specs=_tok_spec(tm, RW_W),
        out_shape=jax.ShapeDtypeStruct((T, RW_W), F32),
        compiler_params=_cparams(("parallel",)),
        name="rwkv_post",
    )(y, r, k, v, g, row(lp['rw_ln_w']), row(lp['rw_ln_b']), row(lp['rw_rk']), _seg_matrix(RW_W, RW_HD))


def rwkv_branch(rw2, B, T, prev_row, s0, lp, tm, tc):
    rw_cols = rw2.reshape(B, T, RW_COLS)
    xprev = jnp.concatenate([prev_row[:, None, :], rw_cols[:, :-1]], axis=1).reshape(B * T, RW_COLS)
    r, w, k, v, a, b, g = rwkv_prep(rw2, xprev, lp, tm)
    sh = lambda t: t.reshape(B, T, RW_W)
    y, sT = state_scan(sh(w), sh(a), sh(b), sh(k), sh(r), sh(v), s0, tc)
    o = rwkv_post(y.reshape(B * T, RW_W), r, k, v, g, lp, tm)
    return o, rw_cols[:, -1], sT


def _topk_rows(s, rows, k, payload=None):
    n = s.shape[0]
    vals, outs = [], []
    for _ in range(k):
        m = jnp.max(s, axis=0, keepdims=True)
        pos = jnp.min(jnp.where(s == m, rows, n), axis=0, keepdims=True)
        sel = rows == pos
        vals.append(m)
        outs.append(pos if payload is None else jnp.sum(jnp.where(sel, payload, 0), axis=0, keepdims=True))
        s = jnp.where(sel, -jnp.inf, s)
    return jnp.concatenate(vals, axis=0), jnp.concatenate(outs, axis=0)


_PAIR_BLOCKS = ((0, 1, PK_TOPK, PK_TOPK),) + tuple((a, 1, 8, PK_TOPK // (a + 1)) for a in range(1, 8)) + ((8, 8, 1, 8),)
_PAIR_ROWS = sum(max(na, nb) for _, na, nb, _ in _PAIR_BLOCKS)


def _peer_route_kernel(x_ref, sc_ref, sh_ref, g_ref, wq_ref, keys_ref, h_ref, xt_ref, eid_ref, gate_ref):
    x = x_ref[...]
    tm = x.shape[0]
    y = x * lax.rsqrt(jnp.mean(x * x, axis=-1, keepdims=True) + EPS) * g_ref[...]
    h = y * (1.0 + sc_ref[...]) + sh_ref[...]
    h_ref[...] = h.reshape(h_ref.shape)
    xt_ref[...] = x.reshape(xt_ref.shape)
    q = jnp.dot(h.astype(BF16), wq_ref[...], preferred_element_type=F32)
    rows_k = lax.broadcasted_iota(jnp.int32, (PK_NKEYS, tm), 0)
    rows_c = lax.broadcasted_iota(jnp.int32, (_PAIR_ROWS, tm), 0)
    eids, gates = [], []
    for hh in range(PK_HEADS):
        tops = []
        for i in range(2):
            c0 = (hh * 2 + i) * PK_DHALF
            s = lax.dot_general(keys_ref[hh * 2 + i], q[:, c0:c0 + PK_DHALF].astype(BF16),
                                (((1,), (1,)), ((), ())), preferred_element_type=F32)
            tops.append(_topk_rows(s, rows_k, PK_TOPK))
        (s0, i0), (s1, i1) = tops
        cs, ci = [], []
        for a0, na, nb, keep_n in _PAIR_BLOCKS:
            blk = s0[a0:a0 + na] + s1[0:nb]
            if keep_n < blk.shape[0]:
                blk = jnp.where(lax.broadcasted_iota(jnp.int32, blk.shape, 0) < keep_n, blk, -jnp.inf)
            cs.append(blk)
            ci.append(i0[a0:a0 + na] * PK_NKEYS + i1[0:nb])
        cand = jnp.concatenate(cs, axis=0)
        cid = jnp.concatenate(ci, axis=0)
        best, eid = _topk_rows(cand, rows_c, PK_TOPK, payload=cid)
        e = jnp.exp(best - best[0:1])
        gates.append(e / jnp.sum(e, axis=0, keepdims=True))
        eids.append(eid)
    eid_ref[...] = jnp.concatenate(eids, axis=0).T
    gate_ref[...] = jnp.concatenate(gates, axis=0).T


def peer_route(x2, sc, sh, norm_g, wq_bf, keys_bf, seq, tm):
    T, D = x2.shape
    sub = D // LANE
    nq = PK_HEADS * PK_DKEY
    nsel = PK_HEADS * PK_TOPK
    return pl.pallas_call(
        _peer_route_kernel,
        grid=(T // tm,),
        in_specs=[_tok_spec(tm, D), _row_spec(seq, tm, D), _row_spec(seq, tm, D), _full_spec((1, D)),
                  _full_spec((D, nq)), _full_spec((2 * PK_HEADS, PK_NKEYS, PK_DHALF))],
        out_specs=[_tok_spec(tm * sub, LANE), _tok_spec(tm * sub, LANE), _tok_spec(tm, nsel), _tok_spec(tm, nsel)],
        out_shape=[jax.ShapeDtypeStruct((T * sub, LANE), F32), jax.ShapeDtypeStruct((T * sub, LANE), F32),
                   jax.ShapeDtypeStruct((T, nsel), jnp.int32), jax.ShapeDtypeStruct((T, nsel), F32)],
        compiler_params=_cparams(("parallel",)),
        name="peer_route",
    )(x2, _row_operand(sc, seq, tm), _row_operand(sh, seq, tm), norm_g[None, :], wq_bf, keys_bf)


PEER_TOK = 256
PEER_SLOTS = 4
PEER_SEL = PK_HEADS * PK_TOPK


PEER_ROWS = 2 * D_MODEL // LANE


def _fold_pairs(x, shift):
    n = x.shape[0] // 2
    x = x.reshape(n, 2, 8, LANE)
    a, b = x[:, 0], x[:, 1]
    low = (lax.broadcasted_iota(jnp.int32, (n, 8, LANE), 1) & shift) == 0
    return jnp.where(low, a + pltpu.roll(a, 8 - shift, 1), b + pltpu.roll(b, shift, 1))


def _fold_order(r):
    s = r & 7
    return (r & ~7) | ((s & 1) << 2) | (s & 2) | ((s >> 2) & 1)


def _peer_gather_kernel(eid_ref, h_ref, gate_ref, x_ref, g2_ref, uv_hbm, o_ref,
                        eid_smem, w_scr, y_scr, buf0, buf1, buf2, buf3, sem_idx, sem):
    bufs = (buf0, buf1, buf2, buf3)
    idx_copy = pltpu.make_async_copy(eid_ref, eid_smem, sem_idx)
    idx_copy.start()
    idx_copy.wait()
    ns = PEER_SLOTS
    n_tok = gate_ref.shape[0]
    n_groups = n_tok // ns
    rows = PEER_ROWS

    def issue(t, slot):
        base = t * PEER_SEL
        for kk in range(PEER_SEL):
            src = pl.multiple_of(eid_smem[base + kk] * rows, rows)
            pltpu.make_async_copy(uv_hbm.at[pl.ds(src, rows)], bufs[slot].at[pl.ds(kk * rows, rows)],
                                  sem.at[slot]).start(priority=kk % 2)

    def wait(slot):
        pltpu.make_async_copy(uv_hbm.at[pl.ds(0, PEER_SEL * rows)], bufs[slot], sem.at[slot]).wait()

    rr = lax.broadcasted_iota(jnp.int32, (PEER_SEL, PEER_SEL), 0)
    pick = _fold_order(rr) == lax.broadcasted_iota(jnp.int32, (PEER_SEL, PEER_SEL), 1)

    def compute(t, slot):
        buf = bufs[slot]
        t8 = pl.multiple_of(t * 8, 8)
        h = h_ref[pl.ds(t8, 8), :]
        u = buf[...].reshape(PEER_SEL, rows, LANE)[:, :8, :]
        part = _fold_pairs(_fold_pairs(_fold_pairs(u * h[None], 4), 2), 1)
        act = jnp.sum(part.reshape(PEER_SEL, LANE), axis=-1, keepdims=True)
        gate = jnp.sum(jnp.where(pick, gate_ref[pl.ds(t, 1), :], 0.0), axis=-1, keepdims=True)
        w_scr[...] = jnp.broadcast_to(gate * _gelu_tanh(act), (PEER_SEL, LANE))
        acc = [jnp.zeros((8, LANE), F32) for _ in range(4)]
        for r in range(PEER_SEL):
            kk = _fold_order(r)
            v = buf[kk * rows + 8:(kk + 1) * rows, :]
            acc[r % 4] = acc[r % 4] + jnp.broadcast_to(w_scr[r:r + 1, :], (8, LANE)) * v
        y = (acc[0] + acc[1]) + (acc[2] + acc[3])
        g2 = g2_ref[...] if g2_ref.shape[0] == 8 else g2_ref[pl.ds(t8, 8), :]
        y_scr[pl.ds(t8, 8), :] = x_ref[pl.ds(t8, 8), :] + g2 * y

    for j in range(ns - 1):
        issue(j, j)

    def group(gi, carry):
        for j in range(ns):
            t = gi * ns + j
            wait(j)
            issue(t + ns - 1, (j + ns - 1) % ns)
            compute(t, j)
        return carry

    lax.fori_loop(0, n_groups - 1, group, 0)
    t0 = (n_groups - 1) * ns
    for j in range(ns):
        wait(j)
        if j == 0:
            issue(t0 + ns - 1, ns - 1)
        compute(t0 + j, j)
    o_ref[...] = y_scr[...].reshape(o_ref.shape)


def peer_table(pk_u, pk_v):
    E, D = pk_u.shape
    t = jnp.concatenate([pk_u.reshape(E, D // LANE, LANE), pk_v.reshape(E, D // LANE, LANE)], axis=1)
    return t.reshape(E * PEER_ROWS, LANE)


def peer_gather(eid, h_tiles, gate, x_tiles, g2, seq, uv):
    sub = PEER_ROWS // 2
    T, D = h_tiles.shape[0] // sub, sub * LANE
    tb = min(PEER_TOK, T)
    assert T % tb == 0 and tb % PEER_SLOTS == 0 and tb // PEER_SLOTS >= 2 and sub == 8 and PEER_SLOTS == 4
    tiles = lambda a: a.reshape(-1, LANE)
    if seq >= tb:
        per_seq = seq // tb
        g2_op, g2_spec = tiles(g2), pl.BlockSpec((sub, LANE), lambda i: (i // per_seq, 0))
    else:
        g2_op, g2_spec = tiles(jnp.repeat(g2, seq, axis=0)), _tok_spec(tb * sub, LANE)
    return pl.pallas_call(
        _peer_gather_kernel,
        grid=(T // tb,),
        in_specs=[pl.BlockSpec((tb * PEER_SEL,), lambda i: (i,)), _tok_spec(tb * sub, LANE), _tok_spec(tb, PEER_SEL),
                  _tok_spec(tb * sub, LANE), g2_spec, pl.BlockSpec(memory_space=pl.ANY)],
        out_specs=_tok_spec(tb, D),
        out_shape=jax.ShapeDtypeStruct((T, D), F32),
        scratch_shapes=[pltpu.SMEM((tb * PEER_SEL,), jnp.int32), pltpu.VMEM((PEER_SEL, LANE), F32),
                        pltpu.VMEM((tb * sub, LANE), F32)]
        + [pltpu.VMEM((PEER_SEL * PEER_ROWS, LANE), F32)] * PEER_SLOTS
        + [pltpu.SemaphoreType.DMA, pltpu.SemaphoreType.DMA((PEER_SLOTS,))],
        compiler_params=_cparams(("arbitrary",)),
        name="peer_gather",
    )(eid.reshape(T * PEER_SEL), h_tiles, gate, x_tiles, g2_op, uv)


def _ada_kernel(c_ref, w_ref, b_ref, o_ref):
    c = c_ref[...]
    o_ref[...] = jnp.dot(c * jax.nn.sigmoid(c), w_ref[...], preferred_element_type=F32) + b_ref[...]


def ada_mod(c, ada_w, ada_b, layer):
    B, D = c.shape
    N = ada_w.shape[2]
    tn = D
    return pl.pallas_call(
        _ada_kernel,
        grid=(N // tn,),
        in_specs=[_full_spec((B, D)), pl.BlockSpec((None, D, tn), lambda j: (layer, 0, j)),
                  pl.BlockSpec((None, 1, tn), lambda j: (layer, 0, j))],
        out_specs=pl.BlockSpec((B, tn), lambda j: (0, j)),
        out_shape=jax.ShapeDtypeStruct((B, N), F32),
        compiler_params=_cparams(("parallel",)),
        name="ada_mod",
    )(c, ada_w, ada_b[:, None, :])


MLA_HP = LANE


def _rope_tables(pos):
    half = ROPE_DIM // 2
    inv = jnp.power(ROPE_BASE, -jnp.arange(half, dtype=F32) / half)
    ang = pos[:, None].astype(F32) * inv
    cos, sin = jnp.cos(ang), jnp.sin(ang)
    n = pos.shape[0]
    one, zero = jnp.ones((n, NOPE_DIM), F32), jnp.zeros((n, NOPE_DIM), F32)
    tail = jnp.zeros((n, MLA_HP - QK_DIM), F32)
    c = jnp.concatenate([one, cos, cos, tail], axis=-1)
    s_up = jnp.concatenate([zero, jnp.zeros_like(sin), sin, tail], axis=-1)
    s_dn = jnp.concatenate([zero, -sin, jnp.zeros_like(sin), tail], axis=-1)
    return c, s_up, s_dn


def _rope_apply(x, c, s_up, s_dn):
    half = ROPE_DIM // 2
    return x * c + pltpu.roll(x, half, 1) * s_up + pltpu.roll(x, MLA_HP - half, 1) * s_dn


def _head_ind(width, seg, n):
    return (jnp.arange(width)[:, None] // seg == jnp.arange(n)[None, :]).astype(BF16)


def _mla_common(c_ref, qag_ref, wqb_ref, qg_ref, kvg_ref, wkb_ref, kgn_ref, kgr_ref, ind_ref,
                cos_ref, sup_ref, sdn_ref):
    c = c_ref[...]
    cos, sup, sdn = cos_ref[...], sup_ref[...], sdn_ref[...]
    qa, ckv, kpe = c[:, :Q_LORA], c[:, Q_LORA:Q_LORA + KV_LORA], c[:, Q_LORA + KV_LORA:Q_LORA + KV_LORA + LANE]
    qa = qa * lax.rsqrt(jnp.mean(qa * qa, axis=-1, keepdims=True) + EPS) * qag_ref[...]
    q = jnp.dot(qa.astype(BF16), wqb_ref[...], preferred_element_type=F32)
    ind = ind_ref[...]
    qss = _seg_sum(q * q, ind) * (1.0 / QK_DIM)
    qs = []
    for h in range(MLA_HEADS):
        qh = q[:, h * MLA_HP:(h + 1) * MLA_HP] * lax.rsqrt(qss[:, h:h + 1] + EPS) * qg_ref[...]
        qs.append(_rope_apply(qh, cos, sup, sdn))
    ckv = ckv * lax.rsqrt(jnp.mean(ckv * ckv, axis=-1, keepdims=True) + EPS) * kvg_ref[...]
    kn = jnp.dot(ckv.astype(BF16), wkb_ref[...], preferred_element_type=F32)
    pe2 = jnp.sum(kpe * kpe, axis=-1, keepdims=True)
    ksc = lax.rsqrt((_seg_sum(kn * kn, ind) + pe2) * (1.0 / QK_DIM) + EPS)
    kr = _rope_apply(pltpu.roll(kpe, NOPE_DIM, 1) * kgr_ref[...], cos, sup, sdn)
    return qs, ckv, kn, kr, ksc


def _mla_prompt_kernel(c_ref, qag_ref, wqb_ref, qg_ref, kvg_ref, wkb_ref, kgn_ref, kgr_ref, ind_ref,
                       cos_ref, sup_ref, sdn_ref, wvb_ref, q_ref, k_ref, v_ref, ckv_ref, ksc_ref):
    qs, ckv, kn, kr, ksc = _mla_common(c_ref, qag_ref, wqb_ref, qg_ref, kvg_ref, wkb_ref, kgn_ref, kgr_ref,
                                       ind_ref, cos_ref, sup_ref, sdn_ref)
    for h in range(MLA_HEADS):
        q_ref[h] = (qs[h] * ATTN_SCALE).astype(BF16)
        kh = (kn[:, h * MLA_HP:(h + 1) * MLA_HP] * kgn_ref[...] + kr) * ksc[:, h:h + 1]
        k_ref[h] = kh.astype(BF16)
    v_ref[...] = jnp.dot(ckv.astype(BF16), wvb_ref[...], preferred_element_type=F32).astype(BF16)
    ckv_ref[...] = ckv
    ksc_ref[...] = ksc


def _mla_weights(lp):
    pad_h = lambda w, d: jnp.pad(w.reshape(w.shape[0], MLA_HEADS, d), ((0, 0), (0, 0), (0, MLA_HP - d))
                                 ).reshape(w.shape[0], MLA_HEADS * MLA_HP).astype(BF16)
    gk = lp['mla_k_g']
    return dict(
        qag=lp['mla_qa_g'][None, :], wqb=pad_h(lp['mla_wqb'], QK_DIM),
        qg=jnp.pad(lp['mla_q_g'], (0, MLA_HP - QK_DIM))[None, :], kvg=lp['mla_kv_g'][None, :],
        wkb=pad_h(lp['mla_wkb'].reshape(KV_LORA, -1), NOPE_DIM),
        kgn=jnp.pad(gk[:NOPE_DIM], (0, MLA_HP - NOPE_DIM))[None, :],
        kgr=jnp.pad(gk[NOPE_DIM:], (NOPE_DIM, MLA_HP - QK_DIM))[None, :],
        ind=_head_ind(MLA_HEADS * MLA_HP, MLA_HP, MLA_HEADS),
        wvb=lp['mla_wvb'].reshape(KV_LORA, MLA_HEADS * V_DIM).astype(BF16))


def _mla_in_specs(tm, seq):
    tiles = max(seq // tm, 1)
    tab = pl.BlockSpec((tm, MLA_HP), lambda i: (i % tiles, 0))
    return [_tok_spec(tm, MLA_PAD), _full_spec((1, Q_LORA)), _full_spec((Q_LORA, MLA_HEADS * MLA_HP)),
            _full_spec((1, MLA_HP)), _full_spec((1, KV_LORA)), _full_spec((KV_LORA, MLA_HEADS * MLA_HP)),
            _full_spec((1, MLA_HP)), _full_spec((1, MLA_HP)), _full_spec((MLA_HEADS * MLA_HP, MLA_HEADS)),
            tab, tab, tab]


def mla_prompt_proj(mla_cols, pos, lp, B, seq, tm):
    T = B * seq
    w = _mla_weights(lp)
    tiles = seq // tm
    hspec = pl.BlockSpec((None, MLA_HEADS, tm, MLA_HP), lambda i: (i // tiles, 0, i % tiles, 0))
    q, k, v, ckv, ksc = pl.pallas_call(
        _mla_prompt_kernel,
        grid=(T // tm,),
        in_specs=_mla_in_specs(tm, seq) + [_full_spec((KV_LORA, MLA_HEADS * V_DIM))],
        out_specs=[hspec, hspec, _tok_spec(tm, MLA_HEADS * V_DIM), _tok_spec(tm, KV_LORA), _tok_spec(tm, MLA_HEADS)],
        out_shape=[jax.ShapeDtypeStruct((B, MLA_HEADS, seq, MLA_HP), BF16)] * 2
        + [jax.ShapeDtypeStruct((T, MLA_HEADS * V_DIM), BF16), jax.ShapeDtypeStruct((T, KV_LORA), F32),
           jax.ShapeDtypeStruct((T, MLA_HEADS), F32)],
        compiler_params=_cparams(("parallel",)),
        name="mla_prompt_proj",
    )(mla_cols, w['qag'], w['wqb'], w['qg'], w['kvg'], w['wkb'], w['kgn'], w['kgr'], w['ind'],
      *_rope_tables(pos), w['wvb'])
    return q, k, v.reshape(B, seq, -1), ckv, ksc


def _flash_kernel(q_ref, k_ref, v_ref, o_ref, m_scr, l_scr, acc_scr):
    qi, ki = pl.program_id(2), pl.program_id(3)
    tq, tk = q_ref.shape[1], k_ref.shape[1]

    @pl.when(ki == 0)
    def _():
        m_scr[...] = jnp.full(m_scr.shape, -jnp.inf, F32)
        l_scr[...] = jnp.zeros(l_scr.shape, F32)
        acc_scr[...] = jnp.zeros(acc_scr.shape, F32)

    def block(on_diagonal):
        for h in range(2):
            s = lax.dot_general(q_ref[h], k_ref[h], (((1,), (1,)), ((), ())), preferred_element_type=F32)
            if on_diagonal:
                keep = lax.broadcasted_iota(jnp.int32, (tq, tk), 1) <= lax.broadcasted_iota(jnp.int32, (tq, tk), 0)
                s = jnp.where(keep, s, -jnp.inf)
            m_old = m_scr[h]
            m_new = jnp.maximum(m_old, jnp.max(s, axis=-1, keepdims=True))
            alpha = jnp.exp(m_old - m_new)
            p = jnp.exp(s - m_new)
            l_scr[h] = alpha * l_scr[h] + jnp.sum(p, axis=-1, keepdims=True)
            acc_scr[h] = alpha * acc_scr[h] + jnp.dot(p.astype(BF16), v_ref[:, h * V_DIM:(h + 1) * V_DIM],
                                                      preferred_element_type=F32)
            m_scr[h] = m_new

    @pl.when(ki < qi)
    def _():
        block(False)

    @pl.when(ki == qi)
    def _():
        block(True)
        o_ref[...] = jnp.concatenate([acc_scr[h] / l_scr[h] for h in range(2)], axis=-1)


def flash_attention(q, k, v, tq):
    B, H, S, _ = q.shape
    n = S // tq
    return pl.pallas_call(
        _flash_kernel,
        grid=(B, H // 2, n, n),
        in_specs=[pl.BlockSpec((None, 2, tq, MLA_HP), lambda b, h, qi, ki: (b, h, qi, 0)),
                  pl.BlockSpec((None, 2, tq, MLA_HP), lambda b, h, qi, ki: (b, h, jnp.minimum(ki, qi), 0)),
                  pl.BlockSpec((None, tq, 2 * V_DIM), lambda b, h, qi, ki: (b, jnp.minimum(ki, qi), h))],
        out_specs=pl.BlockSpec((None, tq, 2 * V_DIM), lambda b, h, qi, ki: (b, qi, h)),
        out_shape=jax.ShapeDtypeStruct((B, S, H * V_DIM), F32),
        scratch_shapes=[pltpu.VMEM((2, tq, 1), F32), pltpu.VMEM((2, tq, 1), F32), pltpu.VMEM((2, tq, V_DIM), F32)],
        compiler_params=_cparams(("parallel", "parallel", "parallel", "arbitrary")),
        name="flash_attention",
    )(q, k, v)


MLA_QCAT = KV_LORA + MLA_HP


def _mla_sample_kernel(c_ref, qag_ref, wqb_ref, qg_ref, kvg_ref, wkb_ref, kgn_ref, kgr_ref, ind_ref,
                       cos_ref, sup_ref, sdn_ref, q_ref, knew_ref, ckv_ref, ksc_ref):
    qs, ckv, kn, kr, ksc = _mla_common(c_ref, qag_ref, wqb_ref, qg_ref, kvg_ref, wkb_ref, kgn_ref, kgr_ref,
                                       ind_ref, cos_ref, sup_ref, sdn_ref)
    tm = ckv.shape[0]
    rope_lanes = lax.broadcasted_iota(jnp.int32, (tm, MLA_HP), 1) >= NOPE_DIM
    wkb = wkb_ref[...]
    for h in range(MLA_HEADS):
        qh = qs[h] * ATTN_SCALE
        q_lat = lax.dot_general((qh * kgn_ref[...]).astype(BF16), wkb[:, h * MLA_HP:(h + 1) * MLA_HP],
                                (((1,), (1,)), ((), ())), preferred_element_type=F32)
        q_ref[:, h * MLA_QCAT:h * MLA_QCAT + KV_LORA] = q_lat.astype(BF16)
        q_ref[:, h * MLA_QCAT + KV_LORA:(h + 1) * MLA_QCAT] = jnp.where(rope_lanes, qh, 0.0).astype(BF16)
    knew_ref[:, :KV_LORA] = ckv.astype(BF16)
    knew_ref[:, KV_LORA:] = kr.astype(BF16)
    ckv_ref[...] = ckv
    ksc_ref[...] = ksc


def mla_sample_proj(mla_cols, pos, lp, seq, tm):
    T = mla_cols.shape[0]
    w = _mla_weights(lp)
    pos_tok = jnp.tile(pos, tm // seq)
    tables = _rope_tables(pos_tok)
    specs = _mla_in_specs(tm, tm)
    return pl.pallas_call(
        _mla_sample_kernel,
        grid=(T // tm,),
        in_specs=specs,
        out_specs=[_tok_spec(tm, MLA_HEADS * MLA_QCAT), _tok_spec(tm, MLA_QCAT), _tok_spec(tm, KV_LORA),
                   _tok_spec(tm, MLA_HEADS)],
        out_shape=[jax.ShapeDtypeStruct((T, MLA_HEADS * MLA_QCAT), BF16), jax.ShapeDtypeStruct((T, MLA_QCAT), BF16),
                   jax.ShapeDtypeStruct((T, KV_LORA), F32), jax.ShapeDtypeStruct((T, MLA_HEADS), F32)],
        compiler_params=_cparams(("parallel",)),
        name="mla_sample_proj",
    )(mla_cols, w['qag'], w['wqb'], w['qg'], w['kvg'], w['wkb'], w['kgn'], w['kgr'], w['ind'], *tables)


PAGES_PER_STEP = 16


def _paged_attn_kernel(pt_ref, q_ref, knew_ref, kscn_ref, cos_ref, sin_ref, kgr_ref, *rest):
    n = PAGES_PER_STEP
    c_refs, side_refs = rest[:n], rest[n:2 * n]
    o_ref, m_scr, l_scr, acc_scr = rest[2 * n:]
    j = pl.program_id(1)
    L = n * PAGE_SIZE
    rows = q_ref.shape[0]
    T = rows // MLA_HEADS
    half = ROPE_DIM // 2
    eye = (lax.broadcasted_iota(jnp.int32, (MLA_HEADS, MLA_HEADS), 0)
           == lax.broadcasted_iota(jnp.int32, (MLA_HEADS, MLA_HEADS), 1)).astype(F32)
    nt = (((1,), (1,)), ((), ()))

    @pl.when(j == 0)
    def _():
        m_scr[...] = jnp.full(m_scr.shape, -jnp.inf, F32)
        l_scr[...] = jnp.zeros(l_scr.shape, F32)
        acc_scr[...] = jnp.zeros(acc_scr.shape, F32)

    def update(s, value_fn):
        m_old = m_scr[...]
        m_new = jnp.maximum(m_old, jnp.max(s, axis=-1, keepdims=True))
        alpha = jnp.exp(m_old - m_new)
        p = jnp.exp(s - m_new)
        l_scr[...] = alpha * l_scr[...] + jnp.sum(p, axis=-1, keepdims=True)
        acc_scr[...] = alpha * acc_scr[...] + value_fn(p)
        m_scr[...] = m_new

    q_lat = q_ref[:, :KV_LORA]
    q_rope = q_ref[:, KV_LORA + NOPE_DIM:KV_LORA + QK_DIM]
    s_parts, c_parts = [], []
    for i in range(n):
        c_i = c_refs[i][...].astype(BF16)
        x = side_refs[i][:ROPE_DIM, :] * kgr_ref[...]
        cs, sn = cos_ref[j * n + i], sin_ref[j * n + i]
        x1, x2 = x[:half], x[half:]
        kr = jnp.concatenate([x1 * cs - x2 * sn, x1 * sn + x2 * cs], axis=0).astype(BF16)
        s_i = (lax.dot_general(q_lat, c_i, nt, preferred_element_type=F32)
               + jnp.dot(q_rope, kr, preferred_element_type=F32))
        ksc_i = side_refs[i][ROPE_DIM:, :]
        s_parts.append((s_i.reshape(T, MLA_HEADS, PAGE_SIZE) * ksc_i[None]).reshape(rows, PAGE_SIZE))
        c_parts.append(c_i)
    s = jnp.concatenate(s_parts, axis=1)
    update(s, lambda p: sum(jnp.dot(p[:, i * PAGE_SIZE:(i + 1) * PAGE_SIZE].astype(BF16), c_parts[i],
                                    preferred_element_type=F32) for i in range(n)))

    @pl.when(j == pl.num_programs(1) - 1)
    def _():
        knew = knew_ref[...]
        s_new = (lax.dot_general(q_lat, knew[:, :KV_LORA], nt, preferred_element_type=F32)
                 + lax.dot_general(q_rope, knew[:, KV_LORA + NOPE_DIM:KV_LORA + QK_DIM], nt,
                                   preferred_element_type=F32))
        kscn_t = lax.dot_general(eye, kscn_ref[...], nt, precision=lax.Precision.HIGHEST,
                                 preferred_element_type=F32)
        s_new = (s_new.reshape(T, MLA_HEADS, T) * kscn_t[None]).reshape(rows, T)
        q_tok = lax.broadcasted_iota(jnp.int32, (rows, T), 0) // MLA_HEADS
        s_new = jnp.where(lax.broadcasted_iota(jnp.int32, (rows, T), 1) <= q_tok, s_new, -jnp.inf)
        c_new = knew[:, :KV_LORA].astype(F32)

        def new_values(p):
            pb = p.astype(BF16).astype(F32)
            return sum(pb[:, t:t + 1] * c_new[t:t + 1, :] for t in range(T))

        update(s_new, new_values)
        o_ref[...] = acc_scr[...] / l_scr[...]


def paged_attention(q, knew, ksc_new, cache_ckv, cache_side, page_table, layer, gk_rope):
    B, rows, _ = q.shape
    T = rows // MLA_HEADS
    n_pages = page_table.shape[1]
    n = PAGES_PER_STEP
    half = ROPE_DIM // 2
    inv = jnp.power(ROPE_BASE, -jnp.arange(half, dtype=F32) / half)
    pos = jnp.arange(n_pages * PAGE_SIZE).astype(F32).reshape(n_pages, 1, PAGE_SIZE)
    ang = pos * inv[None, :, None]
    cos3, sin3 = jnp.cos(ang), jnp.sin(ang)

    def page_spec(shape, i):
        return pl.BlockSpec((None, None) + shape, lambda b, j, pt: (layer, pt[b, j * n + i], 0, 0))

    grid_spec = pltpu.PrefetchScalarGridSpec(
        num_scalar_prefetch=1,
        grid=(B, n_pages // n),
        in_specs=[pl.BlockSpec((None, rows, MLA_QCAT), lambda b, j, pt: (b, 0, 0)),
                  pl.BlockSpec((None, T, MLA_QCAT), lambda b, j, pt: (b, 0, 0)),
                  pl.BlockSpec((None, T, MLA_HEADS), lambda b, j, pt: (b, 0, 0)),
                  pl.BlockSpec((n_pages, half, PAGE_SIZE), lambda b, j, pt: (0, 0, 0)),
                  pl.BlockSpec((n_pages, half, PAGE_SIZE), lambda b, j, pt: (0, 0, 0)),
                  pl.BlockSpec((ROPE_DIM, 1), lambda b, j, pt: (0, 0))]
        + [page_spec((PAGE_SIZE, KV_LORA), i) for i in range(n)]
        + [page_spec((ROPE_DIM + MLA_HEADS, PAGE_SIZE), i) for i in range(n)],
        out_specs=pl.BlockSpec((None, rows, KV_LORA), lambda b, j, pt: (b, 0, 0)),
        scratch_shapes=[pltpu.VMEM((rows, 1), F32), pltpu.VMEM((rows, 1), F32), pltpu.VMEM((rows, KV_LORA), F32)],
    )
    return pl.pallas_call(
        _paged_attn_kernel,
        grid_spec=grid_spec,
        out_shape=jax.ShapeDtypeStruct((B, rows, KV_LORA), F32),
        compiler_params=_cparams(("parallel", "arbitrary")),
        name="paged_attention",
    )(page_table, q, knew, ksc_new, cos3, sin3, gk_rope[:, None],
      *([cache_ckv] * n), *([cache_side] * n))


def _vproj_kernel(o_ref, w_ref, out_ref):
    out_ref[...] = jnp.concatenate(
        [jnp.dot(o_ref[:, h * KV_LORA:(h + 1) * KV_LORA].astype(BF16), w_ref[h], preferred_element_type=F32)
         for h in range(MLA_HEADS)], axis=-1)


def mla_value_proj(o_lat, wvb):
    T = o_lat.shape[0]
    w = wvb.transpose(1, 0, 2).astype(BF16)
    return pl.pallas_call(
        _vproj_kernel,
        grid=(1,),
        in_specs=[_full_spec((T, MLA_HEADS * KV_LORA)), _full_spec((MLA_HEADS, KV_LORA, V_DIM))],
        out_specs=_full_spec((T, MLA_HEADS * V_DIM)),
        out_shape=jax.ShapeDtypeStruct((T, MLA_HEADS * V_DIM), F32),
        compiler_params=_cparams(("arbitrary",)),
        name="mla_value_proj",
    )(o_lat, w)


SSM_BC = SSM_GROUPS * SSM_STATE


def _softplus(x):
    return jnp.maximum(x, 0.0) + jnp.log(1.0 + jnp.exp(-jnp.abs(x)))


def _silu(x):
    return x * jax.nn.sigmoid(x)


def _gated_norm(y, z, g):
    y = y * _silu(z)
    return y * lax.rsqrt(jnp.mean(y * y, axis=-1, keepdims=True) + EPS) * g


def _ssd_prompt_kernel(c_ref, buf_ref, s0_ref, cw_ref, cb_ref, dtb_ref, a_ref, d_ref, ng_ref,
                       o_ref, conv_ref, sT_ref, carry_scr, s_scr):
    ci = pl.program_id(1)
    Lc = c_ref.shape[0]

    @pl.when(ci == 0)
    def _():
        carry_scr[...] = buf_ref[...]
        s_scr[...] = s0_ref[...]

    cols = c_ref[...]
    z = cols[:, :SSM_W]
    xbc = cols[:, SSM_W:SSM_W + CONV_CH]
    dtr = cols[:, SSM_W + CONV_CH:SSM_W + CONV_CH + LANE]
    ext = jnp.concatenate([carry_scr[...], xbc], axis=0)
    conv = cb_ref[...]
    for i in range(CONV_W):
        lo = 8 - (CONV_W - 1) + i
        conv = conv + ext[lo:lo + Lc, :] * cw_ref[i:i + 1, :]
    carry_scr[...] = xbc[Lc - 8:, :]
    conv = _silu(conv)
    xs, Bm, Cm = conv[:, :SSM_W], conv[:, SSM_W:SSM_W + SSM_BC], conv[:, SSM_W + SSM_BC:]
    dt = _softplus(dtr + dtb_ref[...])
    a = dt * a_ref[...]
    ri = lax.broadcasted_iota(jnp.int32, (Lc, Lc), 0)
    cj = lax.broadcasted_iota(jnp.int32, (Lc, Lc), 1)
    causal = cj <= ri
    acs = jnp.dot(causal.astype(F32), a, precision=lax.Precision.HIGHEST, preferred_element_type=F32)
    acs_t = acs.T
    nt = (((1,), (1,)), ((), ()))
    ys = []
    for g in range(SSM_GROUPS):
        Bg = Bm[:, g * SSM_STATE:(g + 1) * SSM_STATE].astype(BF16)
        Cg = Cm[:, g * SSM_STATE:(g + 1) * SSM_STATE].astype(BF16)
        cb = lax.dot_general(Cg, Bg, nt, preferred_element_type=F32)
        for e in range(SSM_EPG):
            hd = g * SSM_EPG + e
            col = acs[:, hd:hd + 1]
            lmat = jnp.exp(jnp.where(causal, col - acs_t[hd:hd + 1, :], -jnp.inf))
            x_e = xs[:, hd * SSM_HD:(hd + 1) * SSM_HD]
            xdt = x_e * dt[:, hd:hd + 1]
            y = jnp.dot((cb * lmat).astype(BF16), xdt.astype(BF16), preferred_element_type=F32)
            last = acs[Lc - 1:Lc, hd:hd + 1]
            s_in = s_scr[hd]
            y = y + lax.dot_general(Cg, s_in.astype(BF16), nt, preferred_element_type=F32) * jnp.exp(col)
            st = lax.dot_general((xdt * jnp.exp(last - col)).astype(BF16), Bg, (((0,), (0,)), ((), ())),
                                 preferred_element_type=F32)
            s_scr[hd] = s_in * jnp.exp(last) + st
            ys.append(y + x_e * d_ref[:, hd:hd + 1])
    o_ref[...] = _gated_norm(jnp.concatenate(ys, axis=-1), z, ng_ref[...])

    @pl.when(ci == pl.num_programs(1) - 1)
    def _():
        conv_ref[...] = xbc[Lc - (CONV_W - 1):, :]
        sT_ref[...] = s_scr[...]


def _ssm_params(lp):
    pad = lambda v: jnp.pad(v, (0, LANE - SSM_HEADS))[None, :]
    return dict(cw=lp['ssm_conv_w'], cb=lp['ssm_conv_b'][None, :], dtb=pad(lp['ssm_dt_bias']),
                a=pad(-jnp.exp(lp['ssm_a_log'])), d=pad(lp['ssm_d']), ng=lp['ssm_norm_g'][None, :])


def ssd_prompt(ssm_cols, conv_buf, s0, lp, B, seq):
    Lc = SSD_CHUNK
    nc = seq // Lc
    p = _ssm_params(lp)
    buf8 = jnp.pad(conv_buf, ((0, 0), (8 - (CONV_W - 1), 0), (0, 0)))
    st_spec = pl.BlockSpec((None, SSM_HEADS, SSM_HD, SSM_STATE), lambda b, c: (b, 0, 0, 0))
    return pl.pallas_call(
        _ssd_prompt_kernel,
        grid=(B, nc),
        in_specs=[pl.BlockSpec((Lc, SSM_PAD), lambda b, c: (b * nc + c, 0)),
                  pl.BlockSpec((None, 8, CONV_CH), lambda b, c: (b, 0, 0)), st_spec,
                  _full_spec((CONV_W, CONV_CH)), _full_spec((1, CONV_CH)), _full_spec((1, LANE)),
                  _full_spec((1, LANE)), _full_spec((1, LANE)), _full_spec((1, SSM_W))],
        out_specs=[pl.BlockSpec((Lc, SSM_W), lambda b, c: (b * nc + c, 0)),
                   pl.BlockSpec((None, CONV_W - 1, CONV_CH), lambda b, c: (b, 0, 0)), st_spec],
        out_shape=[jax.ShapeDtypeStruct((B * seq, SSM_W), F32), jax.ShapeDtypeStruct((B, CONV_W - 1, CONV_CH), F32),
                   jax.ShapeDtypeStruct((B, SSM_HEADS, SSM_HD, SSM_STATE), F32)],
        scratch_shapes=[pltpu.VMEM((8, CONV_CH), F32), pltpu.VMEM((SSM_HEADS, SSM_HD, SSM_STATE), F32)],
        compiler_params=_cparams(("parallel", "arbitrary")),
        name="ssd_prompt",
    )(ssm_cols, buf8, s0, p['cw'], p['cb'], p['dtb'], p['a'], p['d'], p['ng'])


def _ssd_step_prep_kernel(c_ref, x0_ref, x1_ref, x2_ref, cw_ref, cb_ref, dtb_ref, a_ref, exp_ref, expg_ref,
                          w_ref, k_ref, r_ref, v_ref, xs_ref):
    cols = c_ref[...]
    xbc = cols[:, SSM_W:SSM_W + CONV_CH]
    dtr = cols[:, SSM_W + CONV_CH:SSM_W + CONV_CH + LANE]
    conv = cb_ref[...] + x0_ref[...] * cw_ref[0:1, :] + x1_ref[...] * cw_ref[1:2, :] + x2_ref[...] * cw_ref[2:3, :] \
        + xbc * cw_ref[3:4, :]
    conv = _silu(conv)
    xs, Bm, Cm = conv[:, :SSM_W], conv[:, SSM_W:SSM_W + SSM_BC], conv[:, SSM_W + SSM_BC:]
    dt = _softplus(dtr + dtb_ref[...])
    hi = lax.Precision.HIGHEST
    dt_h = jnp.dot(dt, exp_ref[...], precision=hi, preferred_element_type=F32)
    a_h = jnp.dot(dt * a_ref[...], exp_ref[...], precision=hi, preferred_element_type=F32)
    w_ref[...] = jnp.exp(a_h)
    k_ref[...] = jnp.dot(Bm, expg_ref[...], precision=hi, preferred_element_type=F32)
    r_ref[...] = jnp.dot(Cm, expg_ref[...], precision=hi, preferred_element_type=F32)
    v_ref[...] = xs * dt_h
    xs_ref[...] = xs


def _ssd_step_post_kernel(y_ref, xs_ref, c_ref, dh_ref, ng_ref, o_ref):
    y = y_ref[...] + xs_ref[...] * dh_ref[...]
    o_ref[...] = _gated_norm(y, c_ref[:, :SSM_W], ng_ref[...])


def ssd_decode(ssm_cols, conv_buf, s0, lp, B, seq, tm):
    T = B * seq
    p = _ssm_params(lp)
    xbc = ssm_cols[:, SSM_W:SSM_W + CONV_CH].reshape(B, seq, CONV_CH)
    xpad = jnp.concatenate([conv_buf, xbc], axis=1)
    shifted = [xpad[:, i:i + seq].reshape(T, CONV_CH) for i in range(CONV_W - 1)]
    head_of_lane = jnp.arange(SSM_W) // SSM_HD
    expand = (jnp.arange(LANE)[:, None] == head_of_lane[None, :]).astype(F32)
    src = (head_of_lane // SSM_EPG) * SSM_STATE + jnp.arange(SSM_W) % SSM_STATE
    expand_g = (jnp.arange(SSM_BC)[:, None] == src[None, :]).astype(F32)
    out = jax.ShapeDtypeStruct((T, SSM_W), F32)
    w, k, r, v, xs = pl.pallas_call(
        _ssd_step_prep_kernel,
        grid=(T // tm,),
        in_specs=[_tok_spec(tm, SSM_PAD)] + [_tok_spec(tm, CONV_CH)] * 3
        + [_full_spec((CONV_W, CONV_CH)), _full_spec((1, CONV_CH)), _full_spec((1, LANE)), _full_spec((1, LANE)),
           _full_spec((LANE, SSM_W)), _full_spec((SSM_BC, SSM_W))],
        out_specs=[_tok_spec(tm, SSM_W)] * 5,
        out_shape=[out] * 5,
        compiler_params=_cparams(("parallel",)),
        name="ssd_step_prep",
    )(ssm_cols, *shifted, p['cw'], p['cb'], p['dtb'], p['a'], expand, expand_g)
    sh = lambda t: t.reshape(B, seq, SSM_W)
    zeros = jnp.zeros((B, seq, SSM_W), F32)
    y, sT = state_scan(sh(w), zeros, zeros, sh(k), sh(r), sh(v), s0, seq)
    d_h = jnp.repeat(lp['ssm_d'], SSM_HD)[None, :]
    o = pl.pallas_call(
        _ssd_step_post_kernel,
        grid=(T // tm,),
        in_specs=[_tok_spec(tm, SSM_W), _tok_spec(tm, SSM_W), _tok_spec(tm, SSM_PAD), _full_spec((1, SSM_W)),
                  _full_spec((1, SSM_W))],
        out_specs=_tok_spec(tm, SSM_W),
        out_shape=out,
        compiler_params=_cparams(("parallel",)),
        name="ssd_step_post",
    )(y.reshape(T, SSM_W), xs, ssm_cols, d_h, p['ng'])
    return o, xpad[:, -(CONV_W - 1):], sT


def _merge_kernel(oa_ref, ob_ref, oc_ref, gate_ref, x_ref, g1_ref, wb_ref, wo_ref, o_ref):
    D = x_ref.shape[1]
    acc = jnp.zeros(x_ref.shape, F32)
    for i, ref in enumerate((oa_ref, ob_ref, oc_ref)):
        u = jnp.dot(ref[...].astype(BF16), wb_ref[i], preferred_element_type=F32)
        acc = acc + jax.nn.sigmoid(gate_ref[:, i * D:(i + 1) * D]) * u
    y = jnp.dot(acc.astype(BF16), wo_ref[...], preferred_element_type=F32)
    o_ref[...] = x_ref[...] + g1_ref[...] * y


def branch_merge(o_a, o_b, o_c, gates, x2, g1, wb_bf, wo_bf, seq, tm):
    T, D = x2.shape
    return pl.pallas_call(
        _merge_kernel,
        grid=(T // tm,),
        in_specs=[_tok_spec(tm, BR_W)] * 3 + [_tok_spec(tm, N_BRANCH * D), _tok_spec(tm, D), _row_spec(seq, tm, D),
                                              _full_spec((N_BRANCH, BR_W, D)), _full_spec((D, D))],
        out_specs=_tok_spec(tm, D),
        out_shape=jax.ShapeDtypeStruct((T, D), F32),
        compiler_params=_cparams(("parallel",)),
        name="branch_merge",
    )(o_a, o_b, o_c, gates, x2, _row_operand(g1, seq, tm), wb_bf, wo_bf)


TM_TOKENS = 512
TM_INPROJ = 256
TM_ROUTE = 256
SCAN_CHUNK = 32
FLASH_BLOCK = 2048


def _layer(x2, B, seq, c, lw, st, pos, layer, decode_ctx):
    T = B * seq
    tm = min(TM_TOKENS, T)
    sh1, sc1, g1, sh2, sc2, g2 = jnp.split(ada_mod(c, lw['ada_w'], lw['ada_b'], layer), 6, axis=-1)
    lp = lw['lp']
    rw, mla, ssm, gates = in_proj(x2, sc1, sh1, lp['norm1_g'], lw['w_in_pad'], seq, min(TM_INPROJ, T))
    o_a, rw_shift, rw_state = rwkv_branch(rw, B, seq, st[0], st[1], lp, tm, min(SCAN_CHUNK, seq))
    if decode_ctx is None:
        q, k, v, ckv, ksc = mla_prompt_proj(mla, pos, lp, B, seq, min(tm, seq))
        o_b = flash_attention(q, k, v, min(FLASH_BLOCK, seq)).reshape(T, BR_W)
        o_c, conv_buf, ssm_state = ssd_prompt(ssm, st[2], st[3], lp, B, seq)
    else:
        cache_ckv, cache_side, page_table = decode_ctx
        q, knew, ckv, ksc = mla_sample_proj(mla, pos, lp, seq, tm)
        o_lat = paged_attention(q.reshape(B, seq * MLA_HEADS, MLA_QCAT), knew.reshape(B, seq, MLA_QCAT),
                                ksc.reshape(B, seq, MLA_HEADS), cache_ckv, cache_side, page_table,
                                layer, lp['mla_k_g'][NOPE_DIM:])
        o_b = mla_value_proj(o_lat.reshape(T, MLA_HEADS * KV_LORA), lp['mla_wvb'])
        o_c, conv_buf, ssm_state = ssd_decode(ssm, st[2], st[3], lp, B, seq, tm)
    x2 = branch_merge(o_a, o_b, o_c, gates, x2, g1, lw['wb_bf'], lw['wo_bf'], seq, tm)
    h_tiles, x_tiles, eid, gate = peer_route(x2, sc2, sh2, lp['norm2_g'], lw['wq_bf'], lw['keys_bf'], seq,
                                             min(TM_ROUTE, T))
    x2 = peer_gather(eid, h_tiles, gate, x_tiles, g2, seq, lw['uv'])
    kpe = mla[:, Q_LORA + KV_LORA:MLA_COLS]
    return x2, (ckv.reshape(B, seq, KV_LORA), kpe.reshape(B, seq, ROPE_DIM), ksc.reshape(B, seq, MLA_HEADS),
                rw_shift, rw_state, conv_buf, ssm_state)


def kernel(x_prompt, x_sample, cache_ckv, cache_kpe, cache_kscale, state_rwkv_shift, state_rwkv_wkv,
           state_ssm_conv, state_ssm, page_table, c_prompt, c_sample, ada_w, ada_b, norm1_g, norm2_g, w_in,
           rw_mu, rw_w0, rw_w2, rw_a0, rw_a2, rw_g2, rw_kk, rw_ka, rw_rk, rw_ln_w, rw_ln_b,
           mla_qa_g, mla_wqb, mla_kv_g, mla_wkb, mla_wvb, mla_q_g, mla_k_g,
           ssm_conv_w, ssm_conv_b, ssm_dt_bias, ssm_a_log, ssm_d, ssm_norm_g,
           w_branch, w_out, pk_wq, pk_keys, pk_u, pk_v):
    small = dict(norm1_g=norm1_g, norm2_g=norm2_g,
                 rw_mu=rw_mu, rw_w0=rw_w0, rw_w2=rw_w2, rw_a0=rw_a0, rw_a2=rw_a2, rw_g2=rw_g2,
                 rw_kk=rw_kk, rw_ka=rw_ka, rw_rk=rw_rk, rw_ln_w=rw_ln_w, rw_ln_b=rw_ln_b,
                 mla_qa_g=mla_qa_g, mla_wqb=mla_wqb, mla_kv_g=mla_kv_g, mla_wkb=mla_wkb, mla_wvb=mla_wvb,
                 mla_q_g=mla_q_g, mla_k_g=mla_k_g,
                 ssm_conv_w=ssm_conv_w, ssm_conv_b=ssm_conv_b, ssm_dt_bias=ssm_dt_bias, ssm_a_log=ssm_a_log,
                 ssm_d=ssm_d, ssm_norm_g=ssm_norm_g)
    depth = ada_w.shape[0]
    Bp, S, D = x_prompt.shape
    Bd, T, _ = x_sample.shape
    assert SSD_CHUNK % 8 == 0 and S % SSD_CHUNK == 0 and T < SSD_CHUNK
    past_len = page_table.shape[1] * PAGE_SIZE
    pos_p = jnp.arange(S)
    pos_s = past_len + jnp.arange(T)
    zero_state = (jnp.zeros((Bp, RW_COLS), F32), jnp.zeros((Bp, RW_HEADS, RW_HD, RW_HD), F32),
                  jnp.zeros((Bp, CONV_W - 1, CONV_CH), F32), jnp.zeros((Bp, SSM_HEADS, SSM_HD, SSM_STATE), F32))
    hp, hs = x_prompt.reshape(Bp * S, D), x_sample.reshape(Bd * T, D)
    cache_side = jnp.concatenate([jnp.swapaxes(cache_kpe, -1, -2), jnp.swapaxes(cache_kscale, -1, -2)], axis=-2)
    decode_ctx = (cache_ckv, cache_side, page_table)
    new_p, new_s = [], []
    for l in range(depth):
        lw = dict(lp={name: arr[l] for name, arr in small.items()}, ada_w=ada_w, ada_b=ada_b,
                  w_in_pad=_pad_w_in(w_in[l]), wb_bf=w_branch[l].astype(BF16), wo_bf=w_out[l].astype(BF16),
                  wq_bf=pk_wq[l].astype(BF16),
                  keys_bf=pk_keys[l].reshape(2 * PK_HEADS, PK_NKEYS, PK_DHALF).astype(BF16),
                  uv=peer_table(pk_u[l], pk_v[l]))
        hp, st_p = _layer(hp, Bp, S, c_prompt, lw, zero_state, pos_p, l, None)
        st_in = (state_rwkv_shift[l], state_rwkv_wkv[l], state_ssm_conv[l], state_ssm[l])
        hs, st_s = _layer(hs, Bd, T, c_sample, lw, st_in, pos_s, l, decode_ctx)
        new_p.append(st_p)
        new_s.append(st_s)
    hp, hs = hp.reshape(Bp, S, D), hs.reshape(Bd, T, D)
    return (hp, hs,
            jnp.stack([s[0] for s in new_p]), jnp.stack([s[0] for s in new_s]),
            jnp.stack([s[1] for s in new_p]), jnp.stack([s[1] for s in new_s]),
            jnp.stack([s[2] for s in new_p]), jnp.stack([s[2] for s in new_s]),
            jnp.stack([s[3] for s in new_p]), jnp.stack([s[3] for s in new_s]),
            jnp.stack([s[4] for s in new_p]), jnp.stack([s[4] for s in new_s]),
            jnp.stack([s[5] for s in new_p]), jnp.stack([s[5] for s in new_s]),
            jnp.stack([s[6] for s in new_p]), jnp.stack([s[6] for s in new_s]))
```

```python
import math, functools
import jax, jax.numpy as jnp
from jax import lax
from jax.experimental import pallas as pl
from jax.experimental.pallas import tpu as pltpu

D_MODEL = 1024
PAGE_SIZE = 128
EPS = 1e-6
F32 = jnp.float32
BF16 = jnp.bfloat16
RW_HEADS = 8
RW_HD = 64
RW_W = RW_HEADS * RW_HD
RW_DECAY_LORA = 64
RW_AAA_LORA = 64
RW_GATE_LORA = 128
RW_COLS = 3 * RW_W + RW_DECAY_LORA + RW_AAA_LORA + RW_GATE_LORA
RW_SPLITS = (RW_W, 2 * RW_W, 3 * RW_W, 3 * RW_W + RW_DECAY_LORA, 3 * RW_W + RW_DECAY_LORA + RW_AAA_LORA)
RW_GN_EPS = 64e-5
MLA_HEADS = 8
Q_LORA = 256
KV_LORA = 256
NOPE_DIM = 64
ROPE_DIM = 32
V_DIM = 64
QK_DIM = NOPE_DIM + ROPE_DIM
MLA_COLS = Q_LORA + KV_LORA + ROPE_DIM
ROPE_BASE = 10000.0
ATTN_SCALE = QK_DIM ** -0.5
Q_BLOCK = 128
SSM_HEADS = 8
SSM_HD = 64
SSM_W = SSM_HEADS * SSM_HD
SSM_GROUPS = 2
SSM_EPG = SSM_HEADS // SSM_GROUPS
SSM_STATE = 64
CONV_W = 4
CONV_CH = SSM_W + 2 * SSM_GROUPS * SSM_STATE
SSM_COLS = SSM_W + CONV_CH + SSM_HEADS
SSD_CHUNK = 128
N_BRANCH = 3
BR_W = RW_W
IN_COLS = RW_COLS + MLA_COLS + SSM_COLS + N_BRANCH * D_MODEL
IN_SPLITS = (RW_COLS, RW_COLS + MLA_COLS, RW_COLS + MLA_COLS + SSM_COLS)
PK_HEADS = 8
PK_DKEY = 256
PK_DHALF = PK_DKEY // 2
PK_NKEYS = 128
PK_TOPK = 16
PK_EXPERTS = PK_NKEYS * PK_NKEYS
PK_TOK_BLOCK = 256

LANE = 128
VMEM_LIMIT = 56 * 1024 * 1024

MLA_PAD = 640
SSM_PAD = 1408
PROJ_OFF = (0, RW_COLS, RW_COLS + MLA_PAD, RW_COLS + MLA_PAD + SSM_PAD)
PROJ_COLS = PROJ_OFF[3] + N_BRANCH * D_MODEL


def _cparams(sem):
    return pltpu.CompilerParams(dimension_semantics=sem, vmem_limit_bytes=VMEM_LIMIT)


def _pad_w_in(w_in):
    rw, mla, ssm, gate = jnp.split(w_in, IN_SPLITS, axis=-1)
    mla = jnp.pad(mla, ((0, 0), (0, MLA_PAD - MLA_COLS)))
    ssm = jnp.pad(ssm, ((0, 0), (0, SSM_PAD - SSM_COLS)))
    return jnp.concatenate([rw, mla, ssm, gate], axis=-1).astype(BF16)


PROJ_WIDTHS = (RW_COLS, MLA_PAD, SSM_PAD, N_BRANCH * D_MODEL)


def _inproj_kernel(x_ref, sc_ref, sh_ref, g_ref, w_rw, w_mla, w_ssm, w_gate, o_rw, o_mla, o_ssm, o_gate):
    x = x_ref[...]
    y = x * lax.rsqrt(jnp.mean(x * x, axis=-1, keepdims=True) + EPS) * g_ref[...]
    h = (y * (1.0 + sc_ref[...]) + sh_ref[...]).astype(BF16)
    for w_ref, o_ref in ((w_rw, o_rw), (w_mla, o_mla), (w_ssm, o_ssm), (w_gate, o_gate)):
        o_ref[...] = jnp.dot(h, w_ref[...], preferred_element_type=F32)


def _row_spec(rows_per_group, tm, d):
    if rows_per_group >= tm:
        tiles = rows_per_group // tm
        return pl.BlockSpec((None, 1, d), lambda i, *_: (i // tiles, 0, 0))
    return pl.BlockSpec((None, tm, d), lambda i, *_: (i, 0, 0))


def _row_operand(v, seq, tm):
    if seq >= tm:
        return v[:, None, :]
    return jnp.repeat(v, seq, axis=0).reshape(-1, tm, v.shape[-1])


def in_proj(x2, sc, sh, g, w_pad, seq, tm):
    T, D = x2.shape
    ws = [w_pad[:, PROJ_OFF[i]:PROJ_OFF[i] + PROJ_WIDTHS[i]] for i in range(4)]
    return pl.pallas_call(
        _inproj_kernel,
        grid=(T // tm,),
        in_specs=[_tok_spec(tm, D), _row_spec(seq, tm, D), _row_spec(seq, tm, D), _full_spec((1, D))]
        + [_full_spec((D, n)) for n in PROJ_WIDTHS],
        out_specs=[_tok_spec(tm, n) for n in PROJ_WIDTHS],
        out_shape=[jax.ShapeDtypeStruct((T, n), F32) for n in PROJ_WIDTHS],
        compiler_params=_cparams(("parallel",)),
        name="in_proj",
    )(x2, _row_operand(sc, seq, tm), _row_operand(sh, seq, tm), g[None, :], *ws)


def _seg_sum(x, seg):
    hi = x.astype(BF16)
    lo = (x - hi.astype(F32)).astype(BF16)
    return (jnp.dot(hi, seg, preferred_element_type=F32) + jnp.dot(lo, seg, preferred_element_type=F32))


def _seg_matrix(width, seg):
    i = jnp.arange(width) // seg
    return (i[:, None] == i[None, :]).astype(BF16)


def _gelu_tanh(x):
    return 0.5 * x * (1.0 + jnp.tanh(math.sqrt(2.0 / math.pi) * (x + 0.044715 * (x * x * x))))


def _tok_spec(tm, d, col_block=0):
    return pl.BlockSpec((tm, d), lambda i, *_: (i, col_block))


def _full_spec(shape):
    n = len(shape)
    return pl.BlockSpec(shape, lambda *_: (0,) * n)


def _rwkv_prep_kernel(c_ref, xp_ref, mu_ref, w0_ref, w2_ref, a0_ref, a2_ref, g2_ref, kk_ref, ka_ref, seg_ref,
                      r_ref, w_ref, k_ref, v_ref, a_ref, b_ref, g_ref):
    c = c_ref[...]
    m = c + (xp_ref[...] - c) * mu_ref[...]
    r, k, v = m[:, :RW_W], m[:, RW_W:2 * RW_W], m[:, 2 * RW_W:3 * RW_W]
    wl = m[:, RW_SPLITS[2]:RW_SPLITS[3]]
    al = m[:, RW_SPLITS[3]:RW_SPLITS[4]]
    gl = m[:, RW_SPLITS[4]:]
    nz = -(w0_ref[...] + jnp.dot(jnp.tanh(wl), w2_ref[...], preferred_element_type=F32))
    softplus = jnp.maximum(nz, 0.0) + jnp.log(1.0 + jnp.exp(-jnp.abs(nz)))
    decay = jnp.exp(-jnp.exp(-softplus - 0.5))
    a = jax.nn.sigmoid(a0_ref[...] + jnp.dot(al, a2_ref[...], preferred_element_type=F32))
    g = jnp.dot(jax.nn.sigmoid(gl), g2_ref[...], preferred_element_type=F32)
    kk = k * kk_ref[...]
    k = k * (1.0 + (a - 1.0) * ka_ref[...])
    kk = kk / jnp.maximum(jnp.sqrt(_seg_sum(kk * kk, seg_ref[...])), 1e-12)
    r_ref[...] = r
    w_ref[...] = decay
    k_ref[...] = k
    v_ref[...] = v
    a_ref[...] = -kk
    b_ref[...] = kk * a
    g_ref[...] = g


def rwkv_prep(proj, xprev, lp, tm):
    T = proj.shape[0]
    row = lambda v: v.reshape(1, -1)
    out = jax.ShapeDtypeStruct((T, RW_W), F32)
    return pl.pallas_call(
        _rwkv_prep_kernel,
        grid=(T // tm,),
        in_specs=[_tok_spec(tm, RW_COLS), _tok_spec(tm, RW_COLS), _full_spec((1, RW_COLS)),
                  _full_spec((1, RW_W)), _full_spec((RW_DECAY_LORA, RW_W)),
                  _full_spec((1, RW_W)), _full_spec((RW_AAA_LORA, RW_W)),
                  _full_spec((RW_GATE_LORA, RW_W)), _full_spec((1, RW_W)), _full_spec((1, RW_W)),
                  _full_spec((RW_W, RW_W))],
        out_specs=[_tok_spec(tm, RW_W)] * 7,
        out_shape=[out] * 7,
        compiler_params=_cparams(("parallel",)),
        name="rwkv_prep",
    )(proj, xprev, row(lp['rw_mu']), row(lp['rw_w0']), lp['rw_w2'], row(lp['rw_a0']), lp['rw_a2'], lp['rw_g2'],
      row(lp['rw_kk']), row(lp['rw_ka']), _seg_matrix(RW_W, RW_HD))


SCAN_HEADS = 32
SCAN_VM = LANE // SCAN_HEADS
SCAN_VD = RW_HD // SCAN_VM


def _scan_kernel(w_ref, a_ref, b_ref, k_ref, r_ref, v_ref, s0_ref, y_ref, sT_ref, s_scr):
    c = pl.program_id(1)

    @pl.when(c == 0)
    def _():
        s_scr[...] = s0_ref[...]

    def step(t, carry):
        A, W, Bm, K, R = a_ref[t], w_ref[t], b_ref[t], k_ref[t], r_ref[t]
        V = v_ref[t]
        ys = []
        for vd in range(SCAN_VD):
            S = s_scr[vd]
            sa = jnp.sum(S * A, axis=0, keepdims=True)
            S = S * W + sa * Bm + V[vd:vd + 1, :] * K
            s_scr[vd] = S
            ys.append(jnp.sum(S * R, axis=0, keepdims=True))
        y_ref[t] = jnp.concatenate(ys, axis=0)
        return carry

    lax.fori_loop(0, y_ref.shape[0], step, 0)

    @pl.when(c == pl.num_programs(1) - 1)
    def _():
        sT_ref[...] = s_scr[...]


def _to_scan_kh(x, n_groups):
    B, T, _ = x.shape
    x = x.reshape(B, T, RW_HEADS, RW_HD).transpose(1, 3, 0, 2).reshape(T, RW_HD, n_groups, SCAN_HEADS)
    x = jnp.broadcast_to(x[..., None], (T, RW_HD, n_groups, SCAN_HEADS, SCAN_VM))
    return x.transpose(2, 0, 1, 3, 4).reshape(n_groups, T, RW_HD, LANE)


def _to_scan_v(x, n_groups):
    B, T, _ = x.shape
    x = x.reshape(B, T, RW_HEADS, SCAN_VD, SCAN_VM).transpose(1, 3, 0, 2, 4)
    x = x.reshape(T, SCAN_VD, n_groups, SCAN_HEADS * SCAN_VM)
    return x.transpose(2, 0, 1, 3)


def _from_scan_v(y, B):
    G, T = y.shape[:2]
    y = y.transpose(1, 2, 0, 3).reshape(T, SCAN_VD, B, RW_HEADS, SCAN_VM)
    return y.transpose(2, 0, 3, 1, 4).reshape(B, T, RW_W)


def _state_to_scan(s, n_groups):
    B = s.shape[0]
    s = s.reshape(B, RW_HEADS, SCAN_VD, SCAN_VM, RW_HD).transpose(2, 4, 0, 1, 3)
    s = s.reshape(SCAN_VD, RW_HD, n_groups, SCAN_HEADS * SCAN_VM)
    return s.transpose(2, 0, 1, 3)


def _state_from_scan(s, B):
    s = s.transpose(1, 2, 0, 3).reshape(SCAN_VD, RW_HD, B, RW_HEADS, SCAN_VM)
    return s.transpose(2, 3, 0, 4, 1).reshape(B, RW_HEADS, RW_HD, RW_HD)


def state_scan(w, a, b, k, r, v, s0, tc):
    B, T, _ = w.shape
    G = (B * RW_HEADS) // SCAN_HEADS
    kh = [_to_scan_kh(t, G) for t in (w, a, b, k, r)]
    vv = _to_scan_v(v, G)
    ss = _state_to_scan(s0, G)
    kh_spec = pl.BlockSpec((None, tc, RW_HD, LANE), lambda g, c: (g, c, 0, 0))
    v_spec = pl.BlockSpec((None, tc, SCAN_VD, LANE), lambda g, c: (g, c, 0, 0))
    s_spec = pl.BlockSpec((None, SCAN_VD, RW_HD, LANE), lambda g, c: (g, 0, 0, 0))
    y, sT = pl.pallas_call(
        _scan_kernel,
        grid=(G, T // tc),
        in_specs=[kh_spec] * 5 + [v_spec, s_spec],
        out_specs=[v_spec, s_spec],
        out_shape=[jax.ShapeDtypeStruct((G, T, SCAN_VD, LANE), F32),
                   jax.ShapeDtypeStruct((G, SCAN_VD, RW_HD, LANE), F32)],
        scratch_shapes=[pltpu.VMEM((SCAN_VD, RW_HD, LANE), F32)],
        compiler_params=_cparams(("parallel", "arbitrary")),
        name="state_scan",
    )(*kh, vv, ss)
    return _from_scan_v(y, B), _state_from_scan(sT, B)


def _rwkv_post_kernel(y_ref, r_ref, k_ref, v_ref, g_ref, lnw_ref, lnb_ref, rk_ref, seg_ref, o_ref):
    seg = seg_ref[...]
    y = y_ref[...]
    d = y - _seg_sum(y, seg) * (1.0 / RW_HD)
    var = _seg_sum(d * d, seg) * (1.0 / RW_HD)
    yn = d * lax.rsqrt(var + RW_GN_EPS) * lnw_ref[...] + lnb_ref[...]
    bonus = _seg_sum(r_ref[...] * k_ref[...] * rk_ref[...], seg) * v_ref[...]
    o_ref[...] = (yn + bonus) * g_ref[...]


def rwkv_post(y, r, k, v, g, lp, tm):
    T = y.shape[0]
    row = lambda t: t.reshape(1, -1)
    return pl.pallas_call(
        _rwkv_post_kernel,
        grid=(T // tm,),
        in_specs=[_tok_spec(tm, RW_W)] * 5 + [_full_spec((1, RW_W))] * 3 + [_full_spec((RW_W, RW_W))],
        out_specs=_tok_spec(tm, RW_W),
        out_shape=jax.ShapeDtypeStruct((T, RW_W), F32),
        compiler_params=_cparams(("parallel",)),
        name="rwkv_post",
    )(y, r, k, v, g, row(lp['rw_ln_w']), row(lp['rw_ln_b']), row(lp['rw_rk']), _seg_matrix(RW_W, RW_HD))


def rwkv_branch(rw2, B, T, prev_row, s0, lp, tm, tc):
    rw_cols = rw2.reshape(B, T, RW_COLS)
    xprev = jnp.concatenate([prev_row[:, None, :], rw_cols[:, :-1]], axis=1).reshape(B * T, RW_COLS)
    r, w, k, v, a, b, g = rwkv_prep(rw2, xprev, lp, tm)
    sh = lambda t: t.reshape(B, T, RW_W)
    y, sT = state_scan(sh(w), sh(a), sh(b), sh(k), sh(r), sh(v), s0, tc)
    o = rwkv_post(y.reshape(B * T, RW_W), r, k, v, g, lp, tm)
    return o, rw_cols[:, -1], sT


def _topk_rows(s, rows, k, payload=None):
    n = s.shape[0]
    vals, outs = [], []
    for _ in range(k):
        m = jnp.max(s, axis=0, keepdims=True)
        pos = jnp.min(jnp.where(s == m, rows, n), axis=0, keepdims=True)
        sel = rows == pos
        vals.append(m)
        outs.append(pos if payload is None else jnp.sum(jnp.where(sel, payload, 0), axis=0, keepdims=True))
        s = jnp.where(sel, -jnp.inf, s)
    return jnp.concatenate(vals, axis=0), jnp.concatenate(outs, axis=0)


_PAIR_BLOCKS = ((0, 1, PK_TOPK, PK_TOPK),) + tuple((a, 1, 8, PK_TOPK // (a + 1)) for a in range(1, 8)) + ((8, 8, 1, 8),)
_PAIR_ROWS = sum(max(na, nb) for _, na, nb, _ in _PAIR_BLOCKS)


def _peer_route_kernel(x_ref, sc_ref, sh_ref, g_ref, wq_ref, keys_ref, h_ref, xt_ref, eid_ref, gate_ref):
    x = x_ref[...]
    tm = x.shape[0]
    y = x * lax.rsqrt(jnp.mean(x * x, axis=-1, keepdims=True) + EPS) * g_ref[...]
    h = y * (1.0 + sc_ref[...]) + sh_ref[...]
    h_ref[...] = h.reshape(h_ref.shape)
    xt_ref[...] = x.reshape(xt_ref.shape)
    q = jnp.dot(h.astype(BF16), wq_ref[...], preferred_element_type=F32)
    rows_k = lax.broadcasted_iota(jnp.int32, (PK_NKEYS, tm), 0)
    rows_c = lax.broadcasted_iota(jnp.int32, (_PAIR_ROWS, tm), 0)
    eids, gates = [], []
    for hh in range(PK_HEADS):
        tops = []
        for i in range(2):
            c0 = (hh * 2 + i) * PK_DHALF
            s = lax.dot_general(keys_ref[hh * 2 + i], q[:, c0:c0 + PK_DHALF].astype(BF16),
                                (((1,), (1,)), ((), ())), preferred_element_type=F32)
            tops.append(_topk_rows(s, rows_k, PK_TOPK))
        (s0, i0), (s1, i1) = tops
        cs, ci = [], []
        for a0, na, nb, keep_n in _PAIR_BLOCKS:
            blk = s0[a0:a0 + na] + s1[0:nb]
            if keep_n < blk.shape[0]:
                blk = jnp.where(lax.broadcasted_iota(jnp.int32, blk.shape, 0) < keep_n, blk, -jnp.inf)
            cs.append(blk)
            ci.append(i0[a0:a0 + na] * PK_NKEYS + i1[0:nb])
        cand = jnp.concatenate(cs, axis=0)
        cid = jnp.concatenate(ci, axis=0)
        best, eid = _topk_rows(cand, rows_c, PK_TOPK, payload=cid)
        e = jnp.exp(best - best[0:1])
        gates.append(e / jnp.sum(e, axis=0, keepdims=True))
        eids.append(eid)
    eid_ref[...] = jnp.concatenate(eids, axis=0).T
    gate_ref[...] = jnp.concatenate(gates, axis=0).T


def peer_route(x2, sc, sh, norm_g, wq_bf, keys_bf, seq, tm):
    T, D = x2.shape
    sub = D // LANE
    nq = PK_HEADS * PK_DKEY
    nsel = PK_HEADS * PK_TOPK
    return pl.pallas_call(
        _peer_route_kernel,
        grid=(T // tm,),
        in_specs=[_tok_spec(tm, D), _row_spec(seq, tm, D), _row_spec(seq, tm, D), _full_spec((1, D)),
                  _full_spec((D, nq)), _full_spec((2 * PK_HEADS, PK_NKEYS, PK_DHALF))],
        out_specs=[_tok_spec(tm * sub, LANE), _tok_spec(tm * sub, LANE), _tok_spec(tm, nsel), _tok_spec(tm, nsel)],
        out_shape=[jax.ShapeDtypeStruct((T * sub, LANE), F32), jax.ShapeDtypeStruct((T * sub, LANE), F32),
                   jax.ShapeDtypeStruct((T, nsel), jnp.int32), jax.ShapeDtypeStruct((T, nsel), F32)],
        compiler_params=_cparams(("parallel",)),
        name="peer_route",
    )(x2, _row_operand(sc, seq, tm), _row_operand(sh, seq, tm), norm_g[None, :], wq_bf, keys_bf)


PEER_TOK = 256
PEER_SLOTS = 8
PEER_SEL = PK_HEADS * PK_TOPK


PEER_ROWS = 2 * D_MODEL // LANE


def _fold_pairs(x, shift):
    n = x.shape[0] // 2
    x = x.reshape(n, 2, 8, LANE)
    a, b = x[:, 0], x[:, 1]
    low = (lax.broadcasted_iota(jnp.int32, (n, 8, LANE), 1) & shift) == 0
    return jnp.where(low, a + pltpu.roll(a, 8 - shift, 1), b + pltpu.roll(b, shift, 1))


def _fold_order(r):
    s = r & 7
    return (r & ~7) | ((s & 1) << 2) | (s & 2) | ((s >> 2) & 1)


def _peer_gather_kernel(eid_ref, h_ref, gate_ref, x_ref, g2_ref, uv_hbm, o_ref,
                        eid_smem, w_scr, y_scr, *rest):
    bufs, (sem_idx, sem) = rest[:PEER_SLOTS], rest[PEER_SLOTS:]
    idx_copy = pltpu.make_async_copy(eid_ref, eid_smem, sem_idx)
    idx_copy.start()
    idx_copy.wait()
    ns = PEER_SLOTS
    n_tok = gate_ref.shape[0]
    n_groups = n_tok // ns
    rows = PEER_ROWS

    def issue(t, slot):
        base = t * PEER_SEL
        for kk in range(PEER_SEL):
            src = pl.multiple_of(eid_smem[base + kk] * rows, rows)
            pltpu.make_async_copy(uv_hbm.at[pl.ds(src, rows)], bufs[slot].at[pl.ds(kk * rows, rows)],
                                  sem.at[slot]).start(priority=kk % 2)

    def wait(slot):
        pltpu.make_async_copy(uv_hbm.at[pl.ds(0, PEER_SEL * rows)], bufs[slot], sem.at[slot]).wait()

    rr = lax.broadcasted_iota(jnp.int32, (PEER_SEL, PEER_SEL), 0)
    pick = _fold_order(rr) == lax.broadcasted_iota(jnp.int32, (PEER_SEL, PEER_SEL), 1)

    def compute(t, slot):
        buf = bufs[slot]
        t8 = pl.multiple_of(t * 8, 8)
        h = h_ref[pl.ds(t8, 8), :]
        u = buf[...].reshape(PEER_SEL, rows, LANE)[:, :8, :]
        part = _fold_pairs(_fold_pairs(_fold_pairs(u * h[None], 4), 2), 1)
        act = jnp.sum(part.reshape(PEER_SEL, LANE), axis=-1, keepdims=True)
        gate = jnp.sum(jnp.where(pick, gate_ref[pl.ds(t, 1), :], 0.0), axis=-1, keepdims=True)
        w_scr[...] = jnp.broadcast_to(gate * _gelu_tanh(act), (PEER_SEL, LANE))
        acc = [jnp.zeros((8, LANE), F32) for _ in range(4)]
        for r in range(PEER_SEL):
            kk = _fold_order(r)
            v = buf[kk * rows + 8:(kk + 1) * rows, :]
            acc[r % 4] = acc[r % 4] + jnp.broadcast_to(w_scr[r:r + 1, :], (8, LANE)) * v
        y = (acc[0] + acc[1]) + (acc[2] + acc[3])
        g2 = g2_ref[...] if g2_ref.shape[0] == 8 else g2_ref[pl.ds(t8, 8), :]
        y_scr[pl.ds(t8, 8), :] = x_ref[pl.ds(t8, 8), :] + g2 * y

    for j in range(ns - 1):
        issue(j, j)

    def group(gi, carry):
        for j in range(ns):
            t = gi * ns + j
            wait(j)
            issue(t + ns - 1, (j + ns - 1) % ns)
            compute(t, j)
        return carry

    lax.fori_loop(0, n_groups - 1, group, 0)
    t0 = (n_groups - 1) * ns
    for j in range(ns):
        wait(j)
        if j == 0:
            issue(t0 + ns - 1, ns - 1)
        compute(t0 + j, j)
    o_ref[...] = y_scr[...].reshape(o_ref.shape)


def peer_table(pk_u, pk_v):
    E, D = pk_u.shape
    t = jnp.concatenate([pk_u.reshape(E, D // LANE, LANE), pk_v.reshape(E, D // LANE, LANE)], axis=1)
    return t.reshape(E * PEER_ROWS, LANE)


def peer_gather(eid, h_tiles, gate, x_tiles, g2, seq, uv):
    sub = PEER_ROWS // 2
    T, D = h_tiles.shape[0] // sub, sub * LANE
    tb = min(PEER_TOK, T)
    assert T % tb == 0 and tb % PEER_SLOTS == 0 and tb // PEER_SLOTS >= 2 and sub == 8
    tiles = lambda a: a.reshape(-1, LANE)
    if seq >= tb:
        per_seq = seq // tb
        g2_op, g2_spec = tiles(g2), pl.BlockSpec((sub, LANE), lambda i: (i // per_seq, 0))
    else:
        g2_op, g2_spec = tiles(jnp.repeat(g2, seq, axis=0)), _tok_spec(tb * sub, LANE)
    return pl.pallas_call(
        _peer_gather_kernel,
        grid=(T // tb,),
        in_specs=[pl.BlockSpec((tb * PEER_SEL,), lambda i: (i,)), _tok_spec(tb * sub, LANE), _tok_spec(tb, PEER_SEL),
                  _tok_spec(tb * sub, LANE), g2_spec, pl.BlockSpec(memory_space=pl.ANY)],
        out_specs=_tok_spec(tb, D),
        out_shape=jax.ShapeDtypeStruct((T, D), F32),
        scratch_shapes=[pltpu.SMEM((tb * PEER_SEL,), jnp.int32), pltpu.VMEM((PEER_SEL, LANE), F32),
                        pltpu.VMEM((tb * sub, LANE), F32)]
        + [pltpu.VMEM((PEER_SEL * PEER_ROWS, LANE), F32)] * PEER_SLOTS
        + [pltpu.SemaphoreType.DMA, pltpu.SemaphoreType.DMA((PEER_SLOTS,))],
        compiler_params=_cparams(("arbitrary",)),
        name="peer_gather",
    )(eid.reshape(T * PEER_SEL), h_tiles, gate, x_tiles, g2_op, uv)


def _ada_kernel(c_ref, w_ref, b_ref, o_ref):
    c = c_ref[...]
    o_ref[...] = jnp.dot(c * jax.nn.sigmoid(c), w_ref[...], preferred_element_type=F32) + b_ref[...]


def ada_mod(c, ada_w, ada_b, layer):
    B, D = c.shape
    N = ada_w.shape[2]
    tn = D
    return pl.pallas_call(
        _ada_kernel,
        grid=(N // tn,),
        in_specs=[_full_spec((B, D)), pl.BlockSpec((None, D, tn), lambda j: (layer, 0, j)),
                  pl.BlockSpec((None, 1, tn), lambda j: (layer, 0, j))],
        out_specs=pl.BlockSpec((B, tn), lambda j: (0, j)),
        out_shape=jax.ShapeDtypeStruct((B, N), F32),
        compiler_params=_cparams(("parallel",)),
        name="ada_mod",
    )(c, ada_w, ada_b[:, None, :])


MLA_HP = LANE


def _rope_tables(pos):
    half = ROPE_DIM // 2
    inv = jnp.power(ROPE_BASE, -jnp.arange(half, dtype=F32) / half)
    ang = pos[:, None].astype(F32) * inv
    cos, sin = jnp.cos(ang), jnp.sin(ang)
    n = pos.shape[0]
    one, zero = jnp.ones((n, NOPE_DIM), F32), jnp.zeros((n, NOPE_DIM), F32)
    tail = jnp.zeros((n, MLA_HP - QK_DIM), F32)
    c = jnp.concatenate([one, cos, cos, tail], axis=-1)
    s_up = jnp.concatenate([zero, jnp.zeros_like(sin), sin, tail], axis=-1)
    s_dn = jnp.concatenate([zero, -sin, jnp.zeros_like(sin), tail], axis=-1)
    return c, s_up, s_dn


def _rope_apply(x, c, s_up, s_dn):
    half = ROPE_DIM // 2
    return x * c + pltpu.roll(x, half, 1) * s_up + pltpu.roll(x, MLA_HP - half, 1) * s_dn


def _head_ind(width, seg, n):
    return (jnp.arange(width)[:, None] // seg == jnp.arange(n)[None, :]).astype(BF16)


def _mla_common(c_ref, qag_ref, wqb_ref, qg_ref, kvg_ref, wkb_ref, kgn_ref, kgr_ref, ind_ref,
                cos_ref, sup_ref, sdn_ref):
    c = c_ref[...]
    cos, sup, sdn = cos_ref[...], sup_ref[...], sdn_ref[...]
    qa, ckv, kpe = c[:, :Q_LORA], c[:, Q_LORA:Q_LORA + KV_LORA], c[:, Q_LORA + KV_LORA:Q_LORA + KV_LORA + LANE]
    qa = qa * lax.rsqrt(jnp.mean(qa * qa, axis=-1, keepdims=True) + EPS) * qag_ref[...]
    q = jnp.dot(qa.astype(BF16), wqb_ref[...], preferred_element_type=F32)
    ind = ind_ref[...]
    qss = _seg_sum(q * q, ind) * (1.0 / QK_DIM)
    qs = []
    for h in range(MLA_HEADS):
        qh = q[:, h * MLA_HP:(h + 1) * MLA_HP] * lax.rsqrt(qss[:, h:h + 1] + EPS) * qg_ref[...]
        qs.append(_rope_apply(qh, cos, sup, sdn))
    ckv = ckv * lax.rsqrt(jnp.mean(ckv * ckv, axis=-1, keepdims=True) + EPS) * kvg_ref[...]
    kn = jnp.dot(ckv.astype(BF16), wkb_ref[...], preferred_element_type=F32)
    pe2 = jnp.sum(kpe * kpe, axis=-1, keepdims=True)
    ksc = lax.rsqrt((_seg_sum(kn * kn, ind) + pe2) * (1.0 / QK_DIM) + EPS)
    kr = _rope_apply(pltpu.roll(kpe, NOPE_DIM, 1) * kgr_ref[...], cos, sup, sdn)
    return qs, ckv, kn, kr, ksc


def _mla_prompt_kernel(c_ref, qag_ref, wqb_ref, qg_ref, kvg_ref, wkb_ref, kgn_ref, kgr_ref, ind_ref,
                       cos_ref, sup_ref, sdn_ref, wvb_ref, q_ref, k_ref, v_ref, ckv_ref, ksc_ref):
    qs, ckv, kn, kr, ksc = _mla_common(c_ref, qag_ref, wqb_ref, qg_ref, kvg_ref, wkb_ref, kgn_ref, kgr_ref,
                                       ind_ref, cos_ref, sup_ref, sdn_ref)
    for h in range(MLA_HEADS):
        q_ref[h] = (qs[h] * ATTN_SCALE).astype(BF16)
        kh = (kn[:, h * MLA_HP:(h + 1) * MLA_HP] * kgn_ref[...] + kr) * ksc[:, h:h + 1]
        k_ref[h] = kh.astype(BF16)
    v_ref[...] = jnp.dot(ckv.astype(BF16), wvb_ref[...], preferred_element_type=F32).astype(BF16)
    ckv_ref[...] = ckv
    ksc_ref[...] = ksc


def _mla_weights(lp):
    pad_h = lambda w, d: jnp.pad(w.reshape(w.shape[0], MLA_HEADS, d), ((0, 0), (0, 0), (0, MLA_HP - d))
                                 ).reshape(w.shape[0], MLA_HEADS * MLA_HP).astype(BF16)
    gk = lp['mla_k_g']
    return dict(
        qag=lp['mla_qa_g'][None, :], wqb=pad_h(lp['mla_wqb'], QK_DIM),
        qg=jnp.pad(lp['mla_q_g'], (0, MLA_HP - QK_DIM))[None, :], kvg=lp['mla_kv_g'][None, :],
        wkb=pad_h(lp['mla_wkb'].reshape(KV_LORA, -1), NOPE_DIM),
        kgn=jnp.pad(gk[:NOPE_DIM], (0, MLA_HP - NOPE_DIM))[None, :],
        kgr=jnp.pad(gk[NOPE_DIM:], (NOPE_DIM, MLA_HP - QK_DIM))[None, :],
        ind=_head_ind(MLA_HEADS * MLA_HP, MLA_HP, MLA_HEADS),
        wvb=lp['mla_wvb'].reshape(KV_LORA, MLA_HEADS * V_DIM).astype(BF16))


def _mla_in_specs(tm, seq):
    tiles = max(seq // tm, 1)
    tab = pl.BlockSpec((tm, MLA_HP), lambda i: (i % tiles, 0))
    return [_tok_spec(tm, MLA_PAD), _full_spec((1, Q_LORA)), _full_spec((Q_LORA, MLA_HEADS * MLA_HP)),
            _full_spec((1, MLA_HP)), _full_spec((1, KV_LORA)), _full_spec((KV_LORA, MLA_HEADS * MLA_HP)),
            _full_spec((1, MLA_HP)), _full_spec((1, MLA_HP)), _full_spec((MLA_HEADS * MLA_HP, MLA_HEADS)),
            tab, tab, tab]


def mla_prompt_proj(mla_cols, pos, lp, B, seq, tm):
    T = B * seq
    w = _mla_weights(lp)
    tiles = seq // tm
    hspec = pl.BlockSpec((None, MLA_HEADS, tm, MLA_HP), lambda i: (i // tiles, 0, i % tiles, 0))
    q, k, v, ckv, ksc = pl.pallas_call(
        _mla_prompt_kernel,
        grid=(T // tm,),
        in_specs=_mla_in_specs(tm, seq) + [_full_spec((KV_LORA, MLA_HEADS * V_DIM))],
        out_specs=[hspec, hspec, _tok_spec(tm, MLA_HEADS * V_DIM), _tok_spec(tm, KV_LORA), _tok_spec(tm, MLA_HEADS)],
        out_shape=[jax.ShapeDtypeStruct((B, MLA_HEADS, seq, MLA_HP), BF16)] * 2
        + [jax.ShapeDtypeStruct((T, MLA_HEADS * V_DIM), BF16), jax.ShapeDtypeStruct((T, KV_LORA), F32),
           jax.ShapeDtypeStruct((T, MLA_HEADS), F32)],
        compiler_params=_cparams(("parallel",)),
        name="mla_prompt_proj",
    )(mla_cols, w['qag'], w['wqb'], w['qg'], w['kvg'], w['wkb'], w['kgn'], w['kgr'], w['ind'],
      *_rope_tables(pos), w['wvb'])
    return q, k, v.reshape(B, seq, -1), ckv, ksc


def _flash_kernel(q_ref, k_ref, v_ref, o_ref, m_scr, l_scr, acc_scr):
    qi, ki = pl.program_id(2), pl.program_id(3)
    tq, tk = q_ref.shape[1], k_ref.shape[1]

    @pl.when(ki == 0)
    def _():
        m_scr[...] = jnp.full(m_scr.shape, -jnp.inf, F32)
        l_scr[...] = jnp.zeros(l_scr.shape, F32)
        acc_scr[...] = jnp.zeros(acc_scr.shape, F32)

    def block(on_diagonal):
        for h in range(2):
            s = lax.dot_general(q_ref[h], k_ref[h], (((1,), (1,)), ((), ())), preferred_element_type=F32)
            if on_diagonal:
                keep = lax.broadcasted_iota(jnp.int32, (tq, tk), 1) <= lax.broadcasted_iota(jnp.int32, (tq, tk), 0)
                s = jnp.where(keep, s, -jnp.inf)
            m_old = m_scr[h]
            m_new = jnp.maximum(m_old, jnp.max(s, axis=-1, keepdims=True))
            alpha = jnp.exp(m_old - m_new)
            p = jnp.exp(s - m_new)
            l_scr[h] = alpha * l_scr[h] + jnp.sum(p, axis=-1, keepdims=True)
            acc_scr[h] = alpha * acc_scr[h] + jnp.dot(p.astype(BF16), v_ref[:, h * V_DIM:(h + 1) * V_DIM],
                                                      preferred_element_type=F32)
            m_scr[h] = m_new

    @pl.when(ki < qi)
    def _():
        block(False)

    @pl.when(ki == qi)
    def _():
        block(True)
        o_ref[...] = jnp.concatenate([acc_scr[h] / l_scr[h] for h in range(2)], axis=-1)


def flash_attention(q, k, v, tq):
    B, H, S, _ = q.shape
    n = S // tq
    return pl.pallas_call(
        _flash_kernel,
        grid=(B, H // 2, n, n),
        in_specs=[pl.BlockSpec((None, 2, tq, MLA_HP), lambda b, h, qi, ki: (b, h, qi, 0)),
                  pl.BlockSpec((None, 2, tq, MLA_HP), lambda b, h, qi, ki: (b, h, jnp.minimum(ki, qi), 0)),
                  pl.BlockSpec((None, tq, 2 * V_DIM), lambda b, h, qi, ki: (b, jnp.minimum(ki, qi), h))],
        out_specs=pl.BlockSpec((None, tq, 2 * V_DIM), lambda b, h, qi, ki: (b, qi, h)),
        out_shape=jax.ShapeDtypeStruct((B, S, H * V_DIM), F32),
        scratch_shapes=[pltpu.VMEM((2, tq, 1), F32), pltpu.VMEM((2, tq, 1), F32), pltpu.VMEM((2, tq, V_DIM), F32)],
        compiler_params=_cparams(("parallel", "parallel", "parallel", "arbitrary")),
        name="flash_attention",
    )(q, k, v)


MLA_QCAT = KV_LORA + MLA_HP


def _mla_sample_kernel(c_ref, qag_ref, wqb_ref, qg_ref, kvg_ref, wkb_ref, kgn_ref, kgr_ref, ind_ref,
                       cos_ref, sup_ref, sdn_ref, q_ref, knew_ref, ckv_ref, ksc_ref):
    qs, ckv, kn, kr, ksc = _mla_common(c_ref, qag_ref, wqb_ref, qg_ref, kvg_ref, wkb_ref, kgn_ref, kgr_ref,
                                       ind_ref, cos_ref, sup_ref, sdn_ref)
    tm = ckv.shape[0]
    rope_lanes = lax.broadcasted_iota(jnp.int32, (tm, MLA_HP), 1) >= NOPE_DIM
    wkb = wkb_ref[...]
    for h in range(MLA_HEADS):
        qh = qs[h] * ATTN_SCALE
        q_lat = lax.dot_general((qh * kgn_ref[...]).astype(BF16), wkb[:, h * MLA_HP:(h + 1) * MLA_HP],
                                (((1,), (1,)), ((), ())), preferred_element_type=F32)
        q_ref[:, h * MLA_QCAT:h * MLA_QCAT + KV_LORA] = q_lat.astype(BF16)
        q_ref[:, h * MLA_QCAT + KV_LORA:(h + 1) * MLA_QCAT] = jnp.where(rope_lanes, qh, 0.0).astype(BF16)
    knew_ref[:, :KV_LORA] = ckv.astype(BF16)
    knew_ref[:, KV_LORA:] = kr.astype(BF16)
    ckv_ref[...] = ckv
    ksc_ref[...] = ksc


def mla_sample_proj(mla_cols, pos, lp, seq, tm):
    T = mla_cols.shape[0]
    w = _mla_weights(lp)
    pos_tok = jnp.tile(pos, tm // seq)
    tables = _rope_tables(pos_tok)
    specs = _mla_in_specs(tm, tm)
    return pl.pallas_call(
        _mla_sample_kernel,
        grid=(T // tm,),
        in_specs=specs,
        out_specs=[_tok_spec(tm, MLA_HEADS * MLA_QCAT), _tok_spec(tm, MLA_QCAT), _tok_spec(tm, KV_LORA),
                   _tok_spec(tm, MLA_HEADS)],
        out_shape=[jax.ShapeDtypeStruct((T, MLA_HEADS * MLA_QCAT), BF16), jax.ShapeDtypeStruct((T, MLA_QCAT), BF16),
                   jax.ShapeDtypeStruct((T, KV_LORA), F32), jax.ShapeDtypeStruct((T, MLA_HEADS), F32)],
        compiler_params=_cparams(("parallel",)),
        name="mla_sample_proj",
    )(mla_cols, w['qag'], w['wqb'], w['qg'], w['kvg'], w['wkb'], w['kgn'], w['kgr'], w['ind'], *tables)


PAGES_PER_STEP = 32


def _paged_attn_kernel(pt_ref, q_ref, knew_ref, kscn_ref, cos_ref, sin_ref, kgr_ref, *rest):
    n = PAGES_PER_STEP
    c_refs, side_refs = rest[:n], rest[n:2 * n]
    o_ref, m_scr, l_scr, acc_scr = rest[2 * n:]
    j = pl.program_id(1)
    L = n * PAGE_SIZE
    rows = q_ref.shape[0]
    T = rows // MLA_HEADS
    half = ROPE_DIM // 2
    eye = (lax.broadcasted_iota(jnp.int32, (MLA_HEADS, MLA_HEADS), 0)
           == lax.broadcasted_iota(jnp.int32, (MLA_HEADS, MLA_HEADS), 1)).astype(F32)
    nt = (((1,), (1,)), ((), ()))

    @pl.when(j == 0)
    def _():
        m_scr[...] = jnp.full(m_scr.shape, -jnp.inf, F32)
        l_scr[...] = jnp.zeros(l_scr.shape, F32)
        acc_scr[...] = jnp.zeros(acc_scr.shape, F32)

    def update(s, value_fn):
        m_old = m_scr[...]
        m_new = jnp.maximum(m_old, jnp.max(s, axis=-1, keepdims=True))
        alpha = jnp.exp(m_old - m_new)
        p = jnp.exp(s - m_new)
        l_scr[...] = alpha * l_scr[...] + jnp.sum(p, axis=-1, keepdims=True)
        acc_scr[...] = alpha * acc_scr[...] + value_fn(p)
        m_scr[...] = m_new

    q_lat = q_ref[:, :KV_LORA]
    q_rope = q_ref[:, KV_LORA + NOPE_DIM:KV_LORA + QK_DIM]
    s_parts, c_parts = [], []
    for i in range(n):
        c_i = c_refs[i][...].astype(BF16)
        x = side_refs[i][:ROPE_DIM, :] * kgr_ref[...]
        cs, sn = cos_ref[j * n + i], sin_ref[j * n + i]
        x1, x2 = x[:half], x[half:]
        kr = jnp.concatenate([x1 * cs - x2 * sn, x1 * sn + x2 * cs], axis=0).astype(BF16)
        s_i = (lax.dot_general(q_lat, c_i, nt, preferred_element_type=F32)
               + jnp.dot(q_rope, kr, preferred_element_type=F32))
        ksc_i = side_refs[i][ROPE_DIM:, :]
        s_parts.append((s_i.reshape(T, MLA_HEADS, PAGE_SIZE) * ksc_i[None]).reshape(rows, PAGE_SIZE))
        c_parts.append(c_i)
    s = jnp.concatenate(s_parts, axis=1)
    update(s, lambda p: sum(jnp.dot(p[:, i * PAGE_SIZE:(i + 1) * PAGE_SIZE].astype(BF16), c_parts[i],
                                    preferred_element_type=F32) for i in range(n)))

    @pl.when(j == pl.num_programs(1) - 1)
    def _():
        knew = knew_ref[...]
        s_new = (lax.dot_general(q_lat, knew[:, :KV_LORA], nt, preferred_element_type=F32)
                 + lax.dot_general(q_rope, knew[:, KV_LORA + NOPE_DIM:KV_LORA + QK_DIM], nt,
                                   preferred_element_type=F32))
        kscn_t = lax.dot_general(eye, kscn_ref[...], nt, precision=lax.Precision.HIGHEST,
                                 preferred_element_type=F32)
        s_new = (s_new.reshape(T, MLA_HEADS, T) * kscn_t[None]).reshape(rows, T)
        q_tok = lax.broadcasted_iota(jnp.int32, (rows, T), 0) // MLA_HEADS
        s_new = jnp.where(lax.broadcasted_iota(jnp.int32, (rows, T), 1) <= q_tok, s_new, -jnp.inf)
        c_new = knew[:, :KV_LORA].astype(F32)

        def new_values(p):
            pb = p.astype(BF16).astype(F32)
            return sum(pb[:, t:t + 1] * c_new[t:t + 1, :] for t in range(T))

        update(s_new, new_values)
        o_ref[...] = acc_scr[...] / l_scr[...]


def paged_attention(q, knew, ksc_new, cache_ckv, cache_side, page_table, layer, gk_rope):
    B, rows, _ = q.shape
    T = rows // MLA_HEADS
    n_pages = page_table.shape[1]
    n = PAGES_PER_STEP
    half = ROPE_DIM // 2
    inv = jnp.power(ROPE_BASE, -jnp.arange(half, dtype=F32) / half)
    pos = jnp.arange(n_pages * PAGE_SIZE).astype(F32).reshape(n_pages, 1, PAGE_SIZE)
    ang = pos * inv[None, :, None]
    cos3, sin3 = jnp.cos(ang), jnp.sin(ang)

    def page_spec(shape, i):
        return pl.BlockSpec((None, None) + shape, lambda b, j, pt: (layer, pt[b, j * n + i], 0, 0))

    grid_spec = pltpu.PrefetchScalarGridSpec(
        num_scalar_prefetch=1,
        grid=(B, n_pages // n),
        in_specs=[pl.BlockSpec((None, rows, MLA_QCAT), lambda b, j, pt: (b, 0, 0)),
                  pl.BlockSpec((None, T, MLA_QCAT), lambda b, j, pt: (b, 0, 0)),
                  pl.BlockSpec((None, T, MLA_HEADS), lambda b, j, pt: (b, 0, 0)),
                  pl.BlockSpec((n_pages, half, PAGE_SIZE), lambda b, j, pt: (0, 0, 0)),
                  pl.BlockSpec((n_pages, half, PAGE_SIZE), lambda b, j, pt: (0, 0, 0)),
                  pl.BlockSpec((ROPE_DIM, 1), lambda b, j, pt: (0, 0))]
        + [page_spec((PAGE_SIZE, KV_LORA), i) for i in range(n)]
        + [page_spec((ROPE_DIM + MLA_HEADS, PAGE_SIZE), i) for i in range(n)],
        out_specs=pl.BlockSpec((None, rows, KV_LORA), lambda b, j, pt: (b, 0, 0)),
        scratch_shapes=[pltpu.VMEM((rows, 1), F32), pltpu.VMEM((rows, 1), F32), pltpu.VMEM((rows, KV_LORA), F32)],
    )
    return pl.pallas_call(
        _paged_attn_kernel,
        grid_spec=grid_spec,
        out_shape=jax.ShapeDtypeStruct((B, rows, KV_LORA), F32),
        compiler_params=_cparams(("parallel", "arbitrary")),
        name="paged_attention",
    )(page_table, q, knew, ksc_new, cos3, sin3, gk_rope[:, None],
      *([cache_ckv] * n), *([cache_side] * n))


def _vproj_kernel(o_ref, w_ref, out_ref):
    out_ref[...] = jnp.concatenate(
        [jnp.dot(o_ref[:, h * KV_LORA:(h + 1) * KV_LORA].astype(BF16), w_ref[h], preferred_element_type=F32)
         for h in range(MLA_HEADS)], axis=-1)


def mla_value_proj(o_lat, wvb):
    T = o_lat.shape[0]
    w = wvb.transpose(1, 0, 2).astype(BF16)
    return pl.pallas_call(
        _vproj_kernel,
        grid=(1,),
        in_specs=[_full_spec((T, MLA_HEADS * KV_LORA)), _full_spec((MLA_HEADS, KV_LORA, V_DIM))],
        out_specs=_full_spec((T, MLA_HEADS * V_DIM)),
        out_shape=jax.ShapeDtypeStruct((T, MLA_HEADS * V_DIM), F32),
        compiler_params=_cparams(("arbitrary",)),
        name="mla_value_proj",
    )(o_lat, w)


SSM_BC = SSM_GROUPS * SSM_STATE


def _softplus(x):
    return jnp.maximum(x, 0.0) + jnp.log(1.0 + jnp.exp(-jnp.abs(x)))


def _silu(x):
    return x * jax.nn.sigmoid(x)


def _gated_norm(y, z, g):
    y = y * _silu(z)
    return y * lax.rsqrt(jnp.mean(y * y, axis=-1, keepdims=True) + EPS) * g


def _ssd_prompt_kernel(c_ref, buf_ref, s0_ref, cw_ref, cb_ref, dtb_ref, a_ref, d_ref, ng_ref,
                       o_ref, conv_ref, sT_ref, carry_scr, s_scr):
    ci = pl.program_id(1)
    Lc = c_ref.shape[0]

    @pl.when(ci == 0)
    def _():
        carry_scr[...] = buf_ref[...]
        s_scr[...] = s0_ref[...]

    cols = c_ref[...]
    z = cols[:, :SSM_W]
    xbc = cols[:, SSM_W:SSM_W + CONV_CH]
    dtr = cols[:, SSM_W + CONV_CH:SSM_W + CONV_CH + LANE]
    ext = jnp.concatenate([carry_scr[...], xbc], axis=0)
    conv = cb_ref[...]
    for i in range(CONV_W):
        lo = 8 - (CONV_W - 1) + i
        conv = conv + ext[lo:lo + Lc, :] * cw_ref[i:i + 1, :]
    carry_scr[...] = xbc[Lc - 8:, :]
    conv = _silu(conv)
    xs, Bm, Cm = conv[:, :SSM_W], conv[:, SSM_W:SSM_W + SSM_BC], conv[:, SSM_W + SSM_BC:]
    dt = _softplus(dtr + dtb_ref[...])
    a = dt * a_ref[...]
    ri = lax.broadcasted_iota(jnp.int32, (Lc, Lc), 0)
    cj = lax.broadcasted_iota(jnp.int32, (Lc, Lc), 1)
    causal = cj <= ri
    acs = jnp.dot(causal.astype(F32), a, precision=lax.Precision.HIGHEST, preferred_element_type=F32)
    acs_t = acs.T
    nt = (((1,), (1,)), ((), ()))
    ys = []
    for g in range(SSM_GROUPS):
        Bg = Bm[:, g * SSM_STATE:(g + 1) * SSM_STATE].astype(BF16)
        Cg = Cm[:, g * SSM_STATE:(g + 1) * SSM_STATE].astype(BF16)
        cb = lax.dot_general(Cg, Bg, nt, preferred_element_type=F32)
        for e in range(SSM_EPG):
            hd = g * SSM_EPG + e
            col = acs[:, hd:hd + 1]
            lmat = jnp.exp(jnp.where(causal, col - acs_t[hd:hd + 1, :], -jnp.inf))
            x_e = xs[:, hd * SSM_HD:(hd + 1) * SSM_HD]
            xdt = x_e * dt[:, hd:hd + 1]
            y = jnp.dot((cb * lmat).astype(BF16), xdt.astype(BF16), preferred_element_type=F32)
            last = acs[Lc - 1:Lc, hd:hd + 1]
            s_in = s_scr[hd]
            y = y + lax.dot_general(Cg, s_in.astype(BF16), nt, preferred_element_type=F32) * jnp.exp(col)
            st = lax.dot_general((xdt * jnp.exp(last - col)).astype(BF16), Bg, (((0,), (0,)), ((), ())),
                                 preferred_element_type=F32)
            s_scr[hd] = s_in * jnp.exp(last) + st
            ys.append(y + x_e * d_ref[:, hd:hd + 1])
    o_ref[...] = _gated_norm(jnp.concatenate(ys, axis=-1), z, ng_ref[...])

    @pl.when(ci == pl.num_programs(1) - 1)
    def _():
        conv_ref[...] = xbc[Lc - (CONV_W - 1):, :]
        sT_ref[...] = s_scr[...]


def _ssm_params(lp):
    pad = lambda v: jnp.pad(v, (0, LANE - SSM_HEADS))[None, :]
    return dict(cw=lp['ssm_conv_w'], cb=lp['ssm_conv_b'][None, :], dtb=pad(lp['ssm_dt_bias']),
                a=pad(-jnp.exp(lp['ssm_a_log'])), d=pad(lp['ssm_d']), ng=lp['ssm_norm_g'][None, :])


def ssd_prompt(ssm_cols, conv_buf, s0, lp, B, seq):
    Lc = SSD_CHUNK
    nc = seq // Lc
    p = _ssm_params(lp)
    buf8 = jnp.pad(conv_buf, ((0, 0), (8 - (CONV_W - 1), 0), (0, 0)))
    st_spec = pl.BlockSpec((None, SSM_HEADS, SSM_HD, SSM_STATE), lambda b, c: (b, 0, 0, 0))
    return pl.pallas_call(
        _ssd_prompt_kernel,
        grid=(B, nc),
        in_specs=[pl.BlockSpec((Lc, SSM_PAD), lambda b, c: (b * nc + c, 0)),
                  pl.BlockSpec((None, 8, CONV_CH), lambda b, c: (b, 0, 0)), st_spec,
                  _full_spec((CONV_W, CONV_CH)), _full_spec((1, CONV_CH)), _full_spec((1, LANE)),
                  _full_spec((1, LANE)), _full_spec((1, LANE)), _full_spec((1, SSM_W))],
        out_specs=[pl.BlockSpec((Lc, SSM_W), lambda b, c: (b * nc + c, 0)),
                   pl.BlockSpec((None, CONV_W - 1, CONV_CH), lambda b, c: (b, 0, 0)), st_spec],
        out_shape=[jax.ShapeDtypeStruct((B * seq, SSM_W), F32), jax.ShapeDtypeStruct((B, CONV_W - 1, CONV_CH), F32),
                   jax.ShapeDtypeStruct((B, SSM_HEADS, SSM_HD, SSM_STATE), F32)],
        scratch_shapes=[pltpu.VMEM((8, CONV_CH), F32), pltpu.VMEM((SSM_HEADS, SSM_HD, SSM_STATE), F32)],
        compiler_params=_cparams(("parallel", "arbitrary")),
        name="ssd_prompt",
    )(ssm_cols, buf8, s0, p['cw'], p['cb'], p['dtb'], p['a'], p['d'], p['ng'])


def _ssd_step_prep_kernel(c_ref, x0_ref, x1_ref, x2_ref, cw_ref, cb_ref, dtb_ref, a_ref, exp_ref, expg_ref,
                          w_ref, k_ref, r_ref, v_ref, xs_ref):
    cols = c_ref[...]
    xbc = cols[:, SSM_W:SSM_W + CONV_CH]
    dtr = cols[:, SSM_W + CONV_CH:SSM_W + CONV_CH + LANE]
    conv = cb_ref[...] + x0_ref[...] * cw_ref[0:1, :] + x1_ref[...] * cw_ref[1:2, :] + x2_ref[...] * cw_ref[2:3, :] \
        + xbc * cw_ref[3:4, :]
    conv = _silu(conv)
    xs, Bm, Cm = conv[:, :SSM_W], conv[:, SSM_W:SSM_W + SSM_BC], conv[:, SSM_W + SSM_BC:]
    dt = _softplus(dtr + dtb_ref[...])
    hi = lax.Precision.HIGHEST
    dt_h = jnp.dot(dt, exp_ref[...], precision=hi, preferred_element_type=F32)
    a_h = jnp.dot(dt * a_ref[...], exp_ref[...], precision=hi, preferred_element_type=F32)
    w_ref[...] = jnp.exp(a_h)
    k_ref[...] = jnp.dot(Bm, expg_ref[...], precision=hi, preferred_element_type=F32)
    r_ref[...] = jnp.dot(Cm, expg_ref[...], precision=hi, preferred_element_type=F32)
    v_ref[...] = xs * dt_h
    xs_ref[...] = xs


def _ssd_step_post_kernel(y_ref, xs_ref, c_ref, dh_ref, ng_ref, o_ref):
    y = y_ref[...] + xs_ref[...] * dh_ref[...]
    o_ref[...] = _gated_norm(y, c_ref[:, :SSM_W], ng_ref[...])


def ssd_decode(ssm_cols, conv_buf, s0, lp, B, seq, tm):
    T = B * seq
    p = _ssm_params(lp)
    xbc = ssm_cols[:, SSM_W:SSM_W + CONV_CH].reshape(B, seq, CONV_CH)
    xpad = jnp.concatenate([conv_buf, xbc], axis=1)
    shifted = [xpad[:, i:i + seq].reshape(T, CONV_CH) for i in range(CONV_W - 1)]
    head_of_lane = jnp.arange(SSM_W) // SSM_HD
    expand = (jnp.arange(LANE)[:, None] == head_of_lane[None, :]).astype(F32)
    src = (head_of_lane // SSM_EPG) * SSM_STATE + jnp.arange(SSM_W) % SSM_STATE
    expand_g = (jnp.arange(SSM_BC)[:, None] == src[None, :]).astype(F32)
    out = jax.ShapeDtypeStruct((T, SSM_W), F32)
    w, k, r, v, xs = pl.pallas_call(
        _ssd_step_prep_kernel,
        grid=(T // tm,),
        in_specs=[_tok_spec(tm, SSM_PAD)] + [_tok_spec(tm, CONV_CH)] * 3
        + [_full_spec((CONV_W, CONV_CH)), _full_spec((1, CONV_CH)), _full_spec((1, LANE)), _full_spec((1, LANE)),
           _full_spec((LANE, SSM_W)), _full_spec((SSM_BC, SSM_W))],
        out_specs=[_tok_spec(tm, SSM_W)] * 5,
        out_shape=[out] * 5,
        compiler_params=_cparams(("parallel",)),
        name="ssd_step_prep",
    )(ssm_cols, *shifted, p['cw'], p['cb'], p['dtb'], p['a'], expand, expand_g)
    sh = lambda t: t.reshape(B, seq, SSM_W)
    zeros = jnp.zeros((B, seq, SSM_W), F32)
    y, sT = state_scan(sh(w), zeros, zeros, sh(k), sh(r), sh(v), s0, seq)
    d_h = jnp.repeat(lp['ssm_d'], SSM_HD)[None, :]
    o = pl.pallas_call(
        _ssd_step_post_kernel,
        grid=(T // tm,),
        in_specs=[_tok_spec(tm, SSM_W), _tok_spec(tm, SSM_W), _tok_spec(tm, SSM_PAD), _full_spec((1, SSM_W)),
                  _full_spec((1, SSM_W))],
        out_specs=_tok_spec(tm, SSM_W),
        out_shape=out,
        compiler_params=_cparams(("parallel",)),
        name="ssd_step_post",
    )(y.reshape(T, SSM_W), xs, ssm_cols, d_h, p['ng'])
    return o, xpad[:, -(CONV_W - 1):], sT


def _merge_kernel(oa_ref, ob_ref, oc_ref, gate_ref, x_ref, g1_ref, wb_ref, wo_ref, o_ref):
    D = x_ref.shape[1]
    acc = jnp.zeros(x_ref.shape, F32)
    for i, ref in enumerate((oa_ref, ob_ref, oc_ref)):
        u = jnp.dot(ref[...].astype(BF16), wb_ref[i], preferred_element_type=F32)
        acc = acc + jax.nn.sigmoid(gate_ref[:, i * D:(i + 1) * D]) * u
    y = jnp.dot(acc.astype(BF16), wo_ref[...], preferred_element_type=F32)
    o_ref[...] = x_ref[...] + g1_ref[...] * y


def branch_merge(o_a, o_b, o_c, gates, x2, g1, wb_bf, wo_bf, seq, tm):
    T, D = x2.shape
    return pl.pallas_call(
        _merge_kernel,
        grid=(T // tm,),
        in_specs=[_tok_spec(tm, BR_W)] * 3 + [_tok_spec(tm, N_BRANCH * D), _tok_spec(tm, D), _row_spec(seq, tm, D),
                                              _full_spec((N_BRANCH, BR_W, D)), _full_spec((D, D))],
        out_specs=_tok_spec(tm, D),
        out_shape=jax.ShapeDtypeStruct((T, D), F32),
        compiler_params=_cparams(("parallel",)),
        name="branch_merge",
    )(o_a, o_b, o_c, gates, x2, _row_operand(g1, seq, tm), wb_bf, wo_bf)


TM_TOKENS = 512
TM_INPROJ = 256
TM_ROUTE = 256
SCAN_CHUNK = 32
FLASH_BLOCK = 2048


def _layer(x2, B, seq, c, lw, st, pos, layer, decode_ctx):
    T = B * seq
    tm = min(TM_TOKENS, T)
    sh1, sc1, g1, sh2, sc2, g2 = jnp.split(ada_mod(c, lw['ada_w'], lw['ada_b'], layer), 6, axis=-1)
    lp = lw['lp']
    rw, mla, ssm, gates = in_proj(x2, sc1, sh1, lp['norm1_g'], lw['w_in_pad'], seq, min(TM_INPROJ, T))
    o_a, rw_shift, rw_state = rwkv_branch(rw, B, seq, st[0], st[1], lp, tm, min(SCAN_CHUNK, seq))
    if decode_ctx is None:
        q, k, v, ckv, ksc = mla_prompt_proj(mla, pos, lp, B, seq, min(tm, seq))
        o_b = flash_attention(q, k, v, min(FLASH_BLOCK, seq)).reshape(T, BR_W)
        o_c, conv_buf, ssm_state = ssd_prompt(ssm, st[2], st[3], lp, B, seq)
    else:
        cache_ckv, cache_side, page_table = decode_ctx
        q, knew, ckv, ksc = mla_sample_proj(mla, pos, lp, seq, tm)
        o_lat = paged_attention(q.reshape(B, seq * MLA_HEADS, MLA_QCAT), knew.reshape(B, seq, MLA_QCAT),
                                ksc.reshape(B, seq, MLA_HEADS), cache_ckv, cache_side, page_table,
                                layer, lp['mla_k_g'][NOPE_DIM:])
        o_b = mla_value_proj(o_lat.reshape(T, MLA_HEADS * KV_LORA), lp['mla_wvb'])
        o_c, conv_buf, ssm_state = ssd_decode(ssm, st[2], st[3], lp, B, seq, tm)
    x2 = branch_merge(o_a, o_b, o_c, gates, x2, g1, lw['wb_bf'], lw['wo_bf'], seq, tm)
    h_tiles, x_tiles, eid, gate = peer_route(x2, sc2, sh2, lp['norm2_g'], lw['wq_bf'], lw['keys_bf'], seq,
                                             min(TM_ROUTE, T))
    x2 = peer_gather(eid, h_tiles, gate, x_tiles, g2, seq, lw['uv'])
    kpe = mla[:, Q_LORA + KV_LORA:MLA_COLS]
    return x2, (ckv.reshape(B, seq, KV_LORA), kpe.reshape(B, seq, ROPE_DIM), ksc.reshape(B, seq, MLA_HEADS),
                rw_shift, rw_state, conv_buf, ssm_state)


def kernel(x_prompt, x_sample, cache_ckv, cache_kpe, cache_kscale, state_rwkv_shift, state_rwkv_wkv,
           state_ssm_conv, state_ssm, page_table, c_prompt, c_sample, ada_w, ada_b, norm1_g, norm2_g, w_in,
           rw_mu, rw_w0, rw_w2, rw_a0, rw_a2, rw_g2, rw_kk, rw_ka, rw_rk, rw_ln_w, rw_ln_b,
           mla_qa_g, mla_wqb, mla_kv_g, mla_wkb, mla_wvb, mla_q_g, mla_k_g,
           ssm_conv_w, ssm_conv_b, ssm_dt_bias, ssm_a_log, ssm_d, ssm_norm_g,
           w_branch, w_out, pk_wq, pk_keys, pk_u, pk_v):
    small = dict(norm1_g=norm1_g, norm2_g=norm2_g,
                 rw_mu=rw_mu, rw_w0=rw_w0, rw_w2=rw_w2, rw_a0=rw_a0, rw_a2=rw_a2, rw_g2=rw_g2,
                 rw_kk=rw_kk, rw_ka=rw_ka, rw_rk=rw_rk, rw_ln_w=rw_ln_w, rw_ln_b=rw_ln_b,
                 mla_qa_g=mla_qa_g, mla_wqb=mla_wqb, mla_kv_g=mla_kv_g, mla_wkb=mla_wkb, mla_wvb=mla_wvb,
                 mla_q_g=mla_q_g, mla_k_g=mla_k_g,
                 ssm_conv_w=ssm_conv_w, ssm_conv_b=ssm_conv_b, ssm_dt_bias=ssm_dt_bias, ssm_a_log=ssm_a_log,
                 ssm_d=ssm_d, ssm_norm_g=ssm_norm_g)
    depth = ada_w.shape[0]
    Bp, S, D = x_prompt.shape
    Bd, T, _ = x_sample.shape
    assert SSD_CHUNK % 8 == 0 and S % SSD_CHUNK == 0 and T < SSD_CHUNK
    past_len = page_table.shape[1] * PAGE_SIZE
    pos_p = jnp.arange(S)
    pos_s = past_len + jnp.arange(T)
    zero_state = (jnp.zeros((Bp, RW_COLS), F32), jnp.zeros((Bp, RW_HEADS, RW_HD, RW_HD), F32),
                  jnp.zeros((Bp, CONV_W - 1, CONV_CH), F32), jnp.zeros((Bp, SSM_HEADS, SSM_HD, SSM_STATE), F32))
    hp, hs = x_prompt.reshape(Bp * S, D), x_sample.reshape(Bd * T, D)
    cache_side = jnp.concatenate([jnp.swapaxes(cache_kpe, -1, -2), jnp.swapaxes(cache_kscale, -1, -2)], axis=-2)
    decode_ctx = (cache_ckv, cache_side, page_table)
    new_p, new_s = [], []
    for l in range(depth):
        lw = dict(lp={name: arr[l] for name, arr in small.items()}, ada_w=ada_w, ada_b=ada_b,
                  w_in_pad=_pad_w_in(w_in[l]), wb_bf=w_branch[l].astype(BF16), wo_bf=w_out[l].astype(BF16),
                  wq_bf=pk_wq[l].astype(BF16),
                  keys_bf=pk_keys[l].reshape(2 * PK_HEADS, PK_NKEYS, PK_DHALF).astype(BF16),
                  uv=peer_table(pk_u[l], pk_v[l]))
        hp, st_p = _layer(hp, Bp, S, c_prompt, lw, zero_state, pos_p, l, None)
        st_in = (state_rwkv_shift[l], state_rwkv_wkv[l], state_ssm_conv[l], state_ssm[l])
        hs, st_s = _layer(hs, Bd, T, c_sample, lw, st_in, pos_s, l, decode_ctx)
        new_p.append(st_p)
        new_s.append(st_s)
    hp, hs = hp.reshape(Bp, S, D), hs.reshape(Bd, T, D)
    return (hp, hs,
            jnp.stack([s[0] for s in new_p]), jnp.stack([s[0] for s in new_s]),
            jnp.stack([s[1] for s in new_p]), jnp.stack([s[1] for s in new_s]),
            jnp.stack([s[2] for s in new_p]), jnp.stack([s[2] for s in new_s]),
            jnp.stack([s[3] for s in new_p]), jnp.stack([s[3] for s in new_s]),
            jnp.stack([s[4] for s in new_p]), jnp.stack([s[4] for s in new_s]),
            jnp.stack([s[5] for s in new_p]), jnp.stack([s[5] for s in new_s]),
            jnp.stack([s[6] for s in new_p]), jnp.stack([s[6] for s in new_s]))
```

```python
import math, functools
import jax, jax.numpy as jnp
from jax import lax
from jax.experimental import pallas as pl
from jax.experimental.pallas import tpu as pltpu

D_MODEL = 1024
PAGE_SIZE = 128
EPS = 1e-6
F32 = jnp.float32
BF16 = jnp.bfloat16
RW_HEADS = 8
RW_HD = 64
RW_W = RW_HEADS * RW_HD
RW_DECAY_LORA = 64
RW_AAA_LORA = 64
RW_GATE_LORA = 128
RW_COLS = 3 * RW_W + RW_DECAY_LORA + RW_AAA_LORA + RW_GATE_LORA
RW_SPLITS = (RW_W, 2 * RW_W, 3 * RW_W, 3 * RW_W + RW_DECAY_LORA, 3 * RW_W + RW_DECAY_LORA + RW_AAA_LORA)
RW_GN_EPS = 64e-5
MLA_HEADS = 8
Q_LORA = 256
KV_LORA = 256
NOPE_DIM = 64
ROPE_DIM = 32
V_DIM = 64
QK_DIM = NOPE_DIM + ROPE_DIM
MLA_COLS = Q_LORA + KV_LORA + ROPE_DIM
ROPE_BASE = 10000.0
ATTN_SCALE = QK_DIM ** -0.5
Q_BLOCK = 128
SSM_HEADS = 8
SSM_HD = 64
SSM_W = SSM_HEADS * SSM_HD
SSM_GROUPS = 2
SSM_EPG = SSM_HEADS // SSM_GROUPS
SSM_STATE = 64
CONV_W = 4
CONV_CH = SSM_W + 2 * SSM_GROUPS * SSM_STATE
SSM_COLS = SSM_W + CONV_CH + SSM_HEADS
SSD_CHUNK = 128
N_BRANCH = 3
BR_W = RW_W
IN_COLS = RW_COLS + MLA_COLS + SSM_COLS + N_BRANCH * D_MODEL
IN_SPLITS = (RW_COLS, RW_COLS + MLA_COLS, RW_COLS + MLA_COLS + SSM_COLS)
PK_HEADS = 8
PK_DKEY = 256
PK_DHALF = PK_DKEY // 2
PK_NKEYS = 128
PK_TOPK = 16
PK_EXPERTS = PK_NKEYS * PK_NKEYS
PK_TOK_BLOCK = 256

LANE = 128
VMEM_LIMIT = 56 * 1024 * 1024

MLA_PAD = 640
SSM_PAD = 1408
PROJ_OFF = (0, RW_COLS, RW_COLS + MLA_PAD, RW_COLS + MLA_PAD + SSM_PAD)
PROJ_COLS = PROJ_OFF[3] + N_BRANCH * D_MODEL


def _cparams(sem):
    return pltpu.CompilerParams(dimension_semantics=sem, vmem_limit_bytes=VMEM_LIMIT)


def _pad_w_in(w_in):
    rw, mla, ssm, gate = jnp.split(w_in, IN_SPLITS, axis=-1)
    mla = jnp.pad(mla, ((0, 0), (0, MLA_PAD - MLA_COLS)))
    ssm = jnp.pad(ssm, ((0, 0), (0, SSM_PAD - SSM_COLS)))
    return jnp.concatenate([rw, mla, ssm, gate], axis=-1).astype(BF16)


PROJ_WIDTHS = (RW_COLS, MLA_PAD, SSM_PAD, N_BRANCH * D_MODEL)


def _inproj_kernel(x_ref, sc_ref, sh_ref, g_ref, w_rw, w_mla, w_ssm, w_gate, o_rw, o_mla, o_ssm, o_gate):
    x = x_ref[...]
    y = x * lax.rsqrt(jnp.mean(x * x, axis=-1, keepdims=True) + EPS) * g_ref[...]
    h = (y * (1.0 + sc_ref[...]) + sh_ref[...]).astype(BF16)
    for w_ref, o_ref in ((w_rw, o_rw), (w_mla, o_mla), (w_ssm, o_ssm), (w_gate, o_gate)):
        o_ref[...] = jnp.dot(h, w_ref[...], preferred_element_type=F32)


def _row_spec(rows_per_group, tm, d):
    if rows_per_group >= tm:
        tiles = rows_per_group // tm
        return pl.BlockSpec((None, 1, d), lambda i, *_: (i // tiles, 0, 0))
    return pl.BlockSpec((None, tm, d), lambda i, *_: (i, 0, 0))


def _row_operand(v, seq, tm):
    if seq >= tm:
        return v[:, None, :]
    return jnp.repeat(v, seq, axis=0).reshape(-1, tm, v.shape[-1])


def in_proj(x2, sc, sh, g, w_pad, seq, tm):
    T, D = x2.shape
    ws = [w_pad[:, PROJ_OFF[i]:PROJ_OFF[i] + PROJ_WIDTHS[i]] for i in range(4)]
    return pl.pallas_call(
        _inproj_kernel,
        grid=(T // tm,),
        in_specs=[_tok_spec(tm, D), _row_spec(seq, tm, D), _row_spec(seq, tm, D), _full_spec((1, D))]
        + [_full_spec((D, n)) for n in PROJ_WIDTHS],
        out_specs=[_tok_spec(tm, n) for n in PROJ_WIDTHS],
        out_shape=[jax.ShapeDtypeStruct((T, n), F32) for n in PROJ_WIDTHS],
        compiler_params=_cparams(("parallel",)),
        name="in_proj",
    )(x2, _row_operand(sc, seq, tm), _row_operand(sh, seq, tm), g[None, :], *ws)


def _seg_sum(x, seg):
    hi = x.astype(BF16)
    lo = (x - hi.astype(F32)).astype(BF16)
    return (jnp.dot(hi, seg, preferred_element_type=F32) + jnp.dot(lo, seg, preferred_element_type=F32))


def _seg_matrix(width, seg):
    i = jnp.arange(width) // seg
    return (i[:, None] == i[None, :]).astype(BF16)


def _gelu_tanh(x):
    return 0.5 * x * (1.0 + jnp.tanh(math.sqrt(2.0 / math.pi) * (x + 0.044715 * (x * x * x))))


def _tok_spec(tm, d, col_block=0):
    return pl.BlockSpec((tm, d), lambda i, *_: (i, col_block))


def _full_spec(shape):
    n = len(shape)
    return pl.BlockSpec(shape, lambda *_: (0,) * n)


def _rwkv_prep_kernel(c_ref, xp_ref, mu_ref, w0_ref, w2_ref, a0_ref, a2_ref, g2_ref, kk_ref, ka_ref, seg_ref,
                      r_ref, w_ref, k_ref, v_ref, a_ref, b_ref, g_ref):
    c = c_ref[...]
    m = c + (xp_ref[...] - c) * mu_ref[...]
    r, k, v = m[:, :RW_W], m[:, RW_W:2 * RW_W], m[:, 2 * RW_W:3 * RW_W]
    wl = m[:, RW_SPLITS[2]:RW_SPLITS[3]]
    al = m[:, RW_SPLITS[3]:RW_SPLITS[4]]
    gl = m[:, RW_SPLITS[4]:]
    nz = -(w0_ref[...] + jnp.dot(jnp.tanh(wl), w2_ref[...], preferred_element_type=F32))
    softplus = jnp.maximum(nz, 0.0) + jnp.log(1.0 + jnp.exp(-jnp.abs(nz)))
    decay = jnp.exp(-jnp.exp(-softplus - 0.5))
    a = jax.nn.sigmoid(a0_ref[...] + jnp.dot(al, a2_ref[...], preferred_element_type=F32))
    g = jnp.dot(jax.nn.sigmoid(gl), g2_ref[...], preferred_element_type=F32)
    kk = k * kk_ref[...]
    k = k * (1.0 + (a - 1.0) * ka_ref[...])
    kk = kk / jnp.maximum(jnp.sqrt(_seg_sum(kk * kk, seg_ref[...])), 1e-12)
    r_ref[...] = r
    w_ref[...] = decay
    k_ref[...] = k
    v_ref[...] = v
    a_ref[...] = -kk
    b_ref[...] = kk * a
    g_ref[...] = g


def rwkv_prep(proj, xprev, lp, tm):
    T = proj.shape[0]
    row = lambda v: v.reshape(1, -1)
    out = jax.ShapeDtypeStruct((T, RW_W), F32)
    return pl.pallas_call(
        _rwkv_prep_kernel,
        grid=(T // tm,),
        in_specs=[_tok_spec(tm, RW_COLS), _tok_spec(tm, RW_COLS), _full_spec((1, RW_COLS)),
                  _full_spec((1, RW_W)), _full_spec((RW_DECAY_LORA, RW_W)),
                  _full_spec((1, RW_W)), _full_spec((RW_AAA_LORA, RW_W)),
                  _full_spec((RW_GATE_LORA, RW_W)), _full_spec((1, RW_W)), _full_spec((1, RW_W)),
                  _full_spec((RW_W, RW_W))],
        out_specs=[_tok_spec(tm, RW_W)] * 7,
        out_shape=[out] * 7,
        compiler_params=_cparams(("parallel",)),
        name="rwkv_prep",
    )(proj, xprev, row(lp['rw_mu']), row(lp['rw_w0']), lp['rw_w2'], row(lp['rw_a0']), lp['rw_a2'], lp['rw_g2'],
      row(lp['rw_kk']), row(lp['rw_ka']), _seg_matrix(RW_W, RW_HD))


SCAN_HEADS = 32
SCAN_VM = LANE // SCAN_HEADS
SCAN_VD = RW_HD // SCAN_VM


def _scan_kernel(w_ref, a_ref, b_ref, k_ref, r_ref, v_ref, s0_ref, y_ref, sT_ref, s_scr):
    c = pl.program_id(1)

    @pl.when(c == 0)
    def _():
        s_scr[...] = s0_ref[...]

    def step(t, carry):
        A, W, Bm, K, R = a_ref[t], w_ref[t], b_ref[t], k_ref[t], r_ref[t]
        V = v_ref[t]
        ys = []
        for vd in range(SCAN_VD):
            S = s_scr[vd]
            sa = jnp.sum(S * A, axis=0, keepdims=True)
            S = S * W + sa * Bm + V[vd:vd + 1, :] * K
            s_scr[vd] = S
            ys.append(jnp.sum(S * R, axis=0, keepdims=True))
        y_ref[t] = jnp.concatenate(ys, axis=0)
        return carry

    lax.fori_loop(0, y_ref.shape[0], step, 0)

    @pl.when(c == pl.num_programs(1) - 1)
    def _():
        sT_ref[...] = s_scr[...]


def _to_scan_kh(x, n_groups):
    B, T, _ = x.shape
    x = x.reshape(B, T, RW_HEADS, RW_HD).transpose(1, 3, 0, 2).reshape(T, RW_HD, n_groups, SCAN_HEADS)
    x = jnp.broadcast_to(x[..., None], (T, RW_HD, n_groups, SCAN_HEADS, SCAN_VM))
    return x.transpose(2, 0, 1, 3, 4).reshape(n_groups, T, RW_HD, LANE)


def _to_scan_v(x, n_groups):
    B, T, _ = x.shape
    x = x.reshape(B, T, RW_HEADS, SCAN_VD, SCAN_VM).transpose(1, 3, 0, 2, 4)
    x = x.reshape(T, SCAN_VD, n_groups, SCAN_HEADS * SCAN_VM)
    return x.transpose(2, 0, 1, 3)


def _from_scan_v(y, B):
    G, T = y.shape[:2]
    y = y.transpose(1, 2, 0, 3).reshape(T, SCAN_VD, B, RW_HEADS, SCAN_VM)
    return y.transpose(2, 0, 3, 1, 4).reshape(B, T, RW_W)


def _state_to_scan(s, n_groups):
    B = s.shape[0]
    s = s.reshape(B, RW_HEADS, SCAN_VD, SCAN_VM, RW_HD).transpose(2, 4, 0, 1, 3)
    s = s.reshape(SCAN_VD, RW_HD, n_groups, SCAN_HEADS * SCAN_VM)
    return s.transpose(2, 0, 1, 3)


def _state_from_scan(s, B):
    s = s.transpose(1, 2, 0, 3).reshape(SCAN_VD, RW_HD, B, RW_HEADS, SCAN_VM)
    return s.transpose(2, 3, 0, 4, 1).reshape(B, RW_HEADS, RW_HD, RW_HD)


def state_scan(w, a, b, k, r, v, s0, tc):
    B, T, _ = w.shape
    G = (B * RW_HEADS) // SCAN_HEADS
    kh = [_to_scan_kh(t, G) for t in (w, a, b, k, r)]
    vv = _to_scan_v(v, G)
    ss = _state_to_scan(s0, G)
    kh_spec = pl.BlockSpec((None, tc, RW_HD, LANE), lambda g, c: (g, c, 0, 0))
    v_spec = pl.BlockSpec((None, tc, SCAN_VD, LANE), lambda g, c: (g, c, 0, 0))
    s_spec = pl.BlockSpec((None, SCAN_VD, RW_HD, LANE), lambda g, c: (g, 0, 0, 0))
    y, sT = pl.pallas_call(
        _scan_kernel,
        grid=(G, T // tc),
        in_specs=[kh_spec] * 5 + [v_spec, s_spec],
        out_specs=[v_spec, s_spec],
        out_shape=[jax.ShapeDtypeStruct((G, T, SCAN_VD, LANE), F32),
                   jax.ShapeDtypeStruct((G, SCAN_VD, RW_HD, LANE), F32)],
        scratch_shapes=[pltpu.VMEM((SCAN_VD, RW_HD, LANE), F32)],
        compiler_params=_cparams(("parallel", "arbitrary")),
        name="state_scan",
    )(*kh, vv, ss)
    return _from_scan_v(y, B), _state_from_scan(sT, B)


def _rwkv_post_kernel(y_ref, r_ref, k_ref, v_ref, g_ref, lnw_ref, lnb_ref, rk_ref, seg_ref, o_ref):
    seg = seg_ref[...]
    y = y_ref[...]
    d = y - _seg_sum(y, seg) * (1.0 / RW_HD)
    var = _seg_sum(d * d, seg) * (1.0 / RW_HD)
    yn = d * lax.rsqrt(var + RW_GN_EPS) * lnw_ref[...] + lnb_ref[...]
    bonus = _seg_sum(r_ref[...] * k_ref[...] * rk_ref[...], seg) * v_ref[...]
    o_ref[...] = (yn + bonus) * g_ref[...]


def rwkv_post(y, r, k, v, g, lp, tm):
    T = y.shape[0]
    row = lambda t: t.reshape(1, -1)
    return pl.pallas_call(
        _rwkv_post_kernel,
        grid=(T // tm,),
        in_specs=[_tok_spec(tm, RW_W)] * 5 + [_full_spec((1, RW_W))] * 3 + [_full_spec((RW_W, RW_W))],
        out_specs=_tok_spec(tm, RW_W),
        out_shape=jax.ShapeDtypeStruct((T, RW_W), F32),
        compiler_params=_cparams(("parallel",)),
        name="rwkv_post",
    )(y, r, k, v, g, row(lp['rw_ln_w']), row(lp['rw_ln_b']), row(lp['rw_rk']), _seg_matrix(RW_W, RW_HD))


def rwkv_branch(rw2, B, T, prev_row, s0, lp, tm, tc):
    rw_cols = rw2.reshape(B, T, RW_COLS)
    xprev = jnp.concatenate([prev_row[:, None, :], rw_cols[:, :-1]], axis=1).reshape(B * T, RW_COLS)
    r, w, k, v, a, b, g = rwkv_prep(rw2, xprev, lp, tm)
    sh = lambda t: t.reshape(B, T, RW_W)
    y, sT = state_scan(sh(w), sh(a), sh(b), sh(k), sh(r), sh(v), s0, tc)
    o = rwkv_post(y.reshape(B * T, RW_W), r, k, v, g, lp, tm)
    return o, rw_cols[:, -1], sT


def _topk_rows(s, rows, k, payload=None):
    n = s.shape[0]
    vals, outs = [], []
    for _ in range(k):
        m = jnp.max(s, axis=0, keepdims=True)
        pos = jnp.min(jnp.where(s == m, rows, n), axis=0, keepdims=True)
        sel = rows == pos
        vals.append(m)
        outs.append(pos if payload is None else jnp.sum(jnp.where(sel, payload, 0), axis=0, keepdims=True))
        s = jnp.where(sel, -jnp.inf, s)
    return jnp.concatenate(vals, axis=0), jnp.concatenate(outs, axis=0)


_PAIR_BLOCKS = ((0, 1, PK_TOPK, PK_TOPK),) + tuple((a, 1, 8, PK_TOPK // (a + 1)) for a in range(1, 8)) + ((8, 8, 1, 8),)
_PAIR_ROWS = sum(max(na, nb) for _, na, nb, _ in _PAIR_BLOCKS)


def _peer_route_kernel(x_ref, sc_ref, sh_ref, g_ref, wq_ref, keys_ref, h_ref, xt_ref, eid_ref, gate_ref):
    x = x_ref[...]
    tm = x.shape[0]
    y = x * lax.rsqrt(jnp.mean(x * x, axis=-1, keepdims=True) + EPS) * g_ref[...]
    h = y * (1.0 + sc_ref[...]) + sh_ref[...]
    h_ref[...] = h.reshape(h_ref.shape)
    xt_ref[...] = x.reshape(xt_ref.shape)
    q = jnp.dot(h.astype(BF16), wq_ref[...], preferred_element_type=F32)
    rows_k = lax.broadcasted_iota(jnp.int32, (PK_NKEYS, tm), 0)
    rows_c = lax.broadcasted_iota(jnp.int32, (_PAIR_ROWS, tm), 0)
    eids, gates = [], []
    for hh in range(PK_HEADS):
        tops = []
        for i in range(2):
            c0 = (hh * 2 + i) * PK_DHALF
            s = lax.dot_general(keys_ref[hh * 2 + i], q[:, c0:c0 + PK_DHALF].astype(BF16),
                                (((1,), (1,)), ((), ())), preferred_element_type=F32)
            tops.append(_topk_rows(s, rows_k, PK_TOPK))
        (s0, i0), (s1, i1) = tops
        cs, ci = [], []
        for a0, na, nb, keep_n in _PAIR_BLOCKS:
            blk = s0[a0:a0 + na] + s1[0:nb]
            if keep_n < blk.shape[0]:
                blk = jnp.where(lax.broadcasted_iota(jnp.int32, blk.shape, 0) < keep_n, blk, -jnp.inf)
            cs.append(blk)
            ci.append(i0[a0:a0 + na] * PK_NKEYS + i1[0:nb])
        cand = jnp.concatenate(cs, axis=0)
        cid = jnp.concatenate(ci, axis=0)
        best, eid = _topk_rows(cand, rows_c, PK_TOPK, payload=cid)
        e = jnp.exp(best - best[0:1])
        gates.append(e / jnp.sum(e, axis=0, keepdims=True))
        eids.append(eid)
    eid_ref[...] = jnp.concatenate(eids, axis=0).T
    gate_ref[...] = jnp.concatenate(gates, axis=0).T


def peer_route(x2, sc, sh, norm_g, wq_bf, keys_bf, seq, tm):
    T, D = x2.shape
    sub = D // LANE
    nq = PK_HEADS * PK_DKEY
    nsel = PK_HEADS * PK_TOPK
    return pl.pallas_call(
        _peer_route_kernel,
        grid=(T // tm,),
        in_specs=[_tok_spec(tm, D), _row_spec(seq, tm, D), _row_spec(seq, tm, D), _full_spec((1, D)),
                  _full_spec((D, nq)), _full_spec((2 * PK_HEADS, PK_NKEYS, PK_DHALF))],
        out_specs=[_tok_spec(tm * sub, LANE), _tok_spec(tm * sub, LANE), _tok_spec(tm, nsel), _tok_spec(tm, nsel)],
        out_shape=[jax.ShapeDtypeStruct((T * sub, LANE), F32), jax.ShapeDtypeStruct((T * sub, LANE), F32),
                   jax.ShapeDtypeStruct((T, nsel), jnp.int32), jax.ShapeDtypeStruct((T, nsel), F32)],
        compiler_params=_cparams(("parallel",)),
        name="peer_route",
    )(x2, _row_operand(sc, seq, tm), _row_operand(sh, seq, tm), norm_g[None, :], wq_bf, keys_bf)


PEER_TOK = 256
PEER_SLOTS = 8
PEER_SEL = PK_HEADS * PK_TOPK


_HI16 = -65536
PEER_ROWS = D_MODEL // LANE


def _fold_pairs(x, shift):
    n = x.shape[0] // 2
    x = x.reshape(n, 2, 8, LANE)
    a, b = x[:, 0], x[:, 1]
    low = (lax.broadcasted_iota(jnp.int32, (n, 8, LANE), 1) & shift) == 0
    return jnp.where(low, a + pltpu.roll(a, 8 - shift, 1), b + pltpu.roll(b, shift, 1))


def _fold_order(r):
    s = r & 7
    return (r & ~7) | ((s & 1) << 2) | (s & 2) | ((s >> 2) & 1)


def _peer_gather_kernel(eid_ref, h_ref, gate_ref, x_ref, g2_ref, uv_hbm, o_ref,
                        eid_smem, w_scr, y_scr, *rest):
    bufs, (sem_idx, sem) = rest[:PEER_SLOTS], rest[PEER_SLOTS:]
    idx_copy = pltpu.make_async_copy(eid_ref, eid_smem, sem_idx)
    idx_copy.start()
    idx_copy.wait()
    ns = PEER_SLOTS
    n_tok = gate_ref.shape[0]
    n_groups = n_tok // ns
    rows = PEER_ROWS

    def issue(t, slot):
        base = t * PEER_SEL
        for kk in range(PEER_SEL):
            src = pl.multiple_of(eid_smem[base + kk] * rows, rows)
            pltpu.make_async_copy(uv_hbm.at[pl.ds(src, rows)], bufs[slot].at[pl.ds(kk * rows, rows)],
                                  sem.at[slot]).start(priority=kk % 2)

    def wait(slot):
        pltpu.make_async_copy(uv_hbm.at[pl.ds(0, PEER_SEL * rows)], bufs[slot], sem.at[slot]).wait()

    rr = lax.broadcasted_iota(jnp.int32, (PEER_SEL, PEER_SEL), 0)
    pick = _fold_order(rr) == lax.broadcasted_iota(jnp.int32, (PEER_SEL, PEER_SEL), 1)

    def compute(t, slot):
        buf = bufs[slot]
        t8 = pl.multiple_of(t * 8, 8)
        h = h_ref[pl.ds(t8, 8), :]
        u = lax.bitcast_convert_type(buf[...] & _HI16, F32).reshape(PEER_SEL, rows, LANE)
        part = _fold_pairs(_fold_pairs(_fold_pairs(u * h[None], 4), 2), 1)
        act = jnp.sum(part.reshape(PEER_SEL, LANE), axis=-1, keepdims=True)
        gate = jnp.sum(jnp.where(pick, gate_ref[pl.ds(t, 1), :], 0.0), axis=-1, keepdims=True)
        w_scr[...] = jnp.broadcast_to(gate * _gelu_tanh(act), (PEER_SEL, LANE))
        acc = [jnp.zeros((8, LANE), F32) for _ in range(4)]
        for r in range(PEER_SEL):
            kk = _fold_order(r)
            v = lax.bitcast_convert_type(buf[kk * rows:(kk + 1) * rows, :] << 16, F32)
            acc[r % 4] = acc[r % 4] + jnp.broadcast_to(w_scr[r:r + 1, :], (8, LANE)) * v
        y = (acc[0] + acc[1]) + (acc[2] + acc[3])
        g2 = g2_ref[...] if g2_ref.shape[0] == 8 else g2_ref[pl.ds(t8, 8), :]
        y_scr[pl.ds(t8, 8), :] = x_ref[pl.ds(t8, 8), :] + g2 * y

    for j in range(ns - 1):
        issue(j, j)

    def group(gi, carry):
        for j in range(ns):
            t = gi * ns + j
            wait(j)
            issue(t + ns - 1, (j + ns - 1) % ns)
            compute(t, j)
        return carry

    lax.fori_loop(0, n_groups - 1, group, 0)
    t0 = (n_groups - 1) * ns
    for j in range(ns):
        wait(j)
        if j == 0:
            issue(t0 + ns - 1, ns - 1)
        compute(t0 + j, j)
    o_ref[...] = y_scr[...].reshape(o_ref.shape)


def peer_table(pk_u, pk_v):
    E, D = pk_u.shape
    bits = lambda t: lax.bitcast_convert_type(t.astype(BF16), jnp.uint16).astype(jnp.uint32)
    packed = (bits(pk_u) << 16) | bits(pk_v)
    return lax.bitcast_convert_type(packed, jnp.int32).reshape(E * PEER_ROWS, LANE)


def peer_gather(eid, h_tiles, gate, x_tiles, g2, seq, uv):
    sub = PEER_ROWS
    T, D = h_tiles.shape[0] // sub, sub * LANE
    tb = min(PEER_TOK, T)
    assert T % tb == 0 and tb % PEER_SLOTS == 0 and tb // PEER_SLOTS >= 2 and sub == 8
    tiles = lambda a: a.reshape(-1, LANE)
    if seq >= tb:
        per_seq = seq // tb
        g2_op, g2_spec = tiles(g2), pl.BlockSpec((sub, LANE), lambda i: (i // per_seq, 0))
    else:
        g2_op, g2_spec = tiles(jnp.repeat(g2, seq, axis=0)), _tok_spec(tb * sub, LANE)
    return pl.pallas_call(
        _peer_gather_kernel,
        grid=(T // tb,),
        in_specs=[pl.BlockSpec((tb * PEER_SEL,), lambda i: (i,)), _tok_spec(tb * sub, LANE), _tok_spec(tb, PEER_SEL),
                  _tok_spec(tb * sub, LANE), g2_spec, pl.BlockSpec(memory_space=pl.ANY)],
        out_specs=_tok_spec(tb, D),
        out_shape=jax.ShapeDtypeStruct((T, D), F32),
        scratch_shapes=[pltpu.SMEM((tb * PEER_SEL,), jnp.int32), pltpu.VMEM((PEER_SEL, LANE), F32),
                        pltpu.VMEM((tb * sub, LANE), F32)]
        + [pltpu.VMEM((PEER_SEL * PEER_ROWS, LANE), jnp.int32)] * PEER_SLOTS
        + [pltpu.SemaphoreType.DMA, pltpu.SemaphoreType.DMA((PEER_SLOTS,))],
        compiler_params=_cparams(("arbitrary",)),
        name="peer_gather",
    )(eid.reshape(T * PEER_SEL), h_tiles, gate, x_tiles, g2_op, uv)


def _ada_kernel(c_ref, w_ref, b_ref, o_ref):
    c = c_ref[...]
    o_ref[...] = jnp.dot(c * jax.nn.sigmoid(c), w_ref[...], preferred_element_type=F32) + b_ref[...]


def ada_mod(c, ada_w, ada_b, layer):
    B, D = c.shape
    N = ada_w.shape[2]
    tn = D
    return pl.pallas_call(
        _ada_kernel,
        grid=(N // tn,),
        in_specs=[_full_spec((B, D)), pl.BlockSpec((None, D, tn), lambda j: (layer, 0, j)),
                  pl.BlockSpec((None, 1, tn), lambda j: (layer, 0, j))],
        out_specs=pl.BlockSpec((B, tn), lambda j: (0, j)),
        out_shape=jax.ShapeDtypeStruct((B, N), F32),
        compiler_params=_cparams(("parallel",)),
        name="ada_mod",
    )(c, ada_w, ada_b[:, None, :])


MLA_HP = LANE


def _rope_tables(pos):
    half = ROPE_DIM // 2
    inv = jnp.power(ROPE_BASE, -jnp.arange(half, dtype=F32) / half)
    ang = pos[:, None].astype(F32) * inv
    cos, sin = jnp.cos(ang), jnp.sin(ang)
    n = pos.shape[0]
    one, zero = jnp.ones((n, NOPE_DIM), F32), jnp.zeros((n, NOPE_DIM), F32)
    tail = jnp.zeros((n, MLA_HP - QK_DIM), F32)
    c = jnp.concatenate([one, cos, cos, tail], axis=-1)
    s_up = jnp.concatenate([zero, jnp.zeros_like(sin), sin, tail], axis=-1)
    s_dn = jnp.concatenate([zero, -sin, jnp.zeros_like(sin), tail], axis=-1)
    return c, s_up, s_dn


def _rope_apply(x, c, s_up, s_dn):
    half = ROPE_DIM // 2
    return x * c + pltpu.roll(x, half, 1) * s_up + pltpu.roll(x, MLA_HP - half, 1) * s_dn


def _head_ind(width, seg, n):
    return (jnp.arange(width)[:, None] // seg == jnp.arange(n)[None, :]).astype(BF16)


def _mla_common(c_ref, qag_ref, wqb_ref, qg_ref, kvg_ref, wkb_ref, kgn_ref, kgr_ref, ind_ref,
                cos_ref, sup_ref, sdn_ref):
    c = c_ref[...]
    cos, sup, sdn = cos_ref[...], sup_ref[...], sdn_ref[...]
    qa, ckv, kpe = c[:, :Q_LORA], c[:, Q_LORA:Q_LORA + KV_LORA], c[:, Q_LORA + KV_LORA:Q_LORA + KV_LORA + LANE]
    qa = qa * lax.rsqrt(jnp.mean(qa * qa, axis=-1, keepdims=True) + EPS) * qag_ref[...]
    q = jnp.dot(qa.astype(BF16), wqb_ref[...], preferred_element_type=F32)
    ind = ind_ref[...]
    qss = _seg_sum(q * q, ind) * (1.0 / QK_DIM)
    qs = []
    for h in range(MLA_HEADS):
        qh = q[:, h * MLA_HP:(h + 1) * MLA_HP] * lax.rsqrt(qss[:, h:h + 1] + EPS) * qg_ref[...]
        qs.append(_rope_apply(qh, cos, sup, sdn))
    ckv = ckv * lax.rsqrt(jnp.mean(ckv * ckv, axis=-1, keepdims=True) + EPS) * kvg_ref[...]
    kn = jnp.dot(ckv.astype(BF16), wkb_ref[...], preferred_element_type=F32)
    pe2 = jnp.sum(kpe * kpe, axis=-1, keepdims=True)
    ksc = lax.rsqrt((_seg_sum(kn * kn, ind) + pe2) * (1.0 / QK_DIM) + EPS)
    kr = _rope_apply(pltpu.roll(kpe, NOPE_DIM, 1) * kgr_ref[...], cos, sup, sdn)
    return qs, ckv, kn, kr, ksc


def _mla_prompt_kernel(c_ref, qag_ref, wqb_ref, qg_ref, kvg_ref, wkb_ref, kgn_ref, kgr_ref, ind_ref,
                       cos_ref, sup_ref, sdn_ref, wvb_ref, q_ref, k_ref, v_ref, ckv_ref, ksc_ref):
    qs, ckv, kn, kr, ksc = _mla_common(c_ref, qag_ref, wqb_ref, qg_ref, kvg_ref, wkb_ref, kgn_ref, kgr_ref,
                                       ind_ref, cos_ref, sup_ref, sdn_ref)
    for h in range(MLA_HEADS):
        q_ref[h] = (qs[h] * ATTN_SCALE).astype(BF16)
        kh = (kn[:, h * MLA_HP:(h + 1) * MLA_HP] * kgn_ref[...] + kr) * ksc[:, h:h + 1]
        k_ref[h] = kh.astype(BF16)
    v_ref[...] = jnp.dot(ckv.astype(BF16), wvb_ref[...], preferred_element_type=F32).astype(BF16)
    ckv_ref[...] = ckv
    ksc_ref[...] = ksc


def _mla_weights(lp):
    pad_h = lambda w, d: jnp.pad(w.reshape(w.shape[0], MLA_HEADS, d), ((0, 0), (0, 0), (0, MLA_HP - d))
                                 ).reshape(w.shape[0], MLA_HEADS * MLA_HP).astype(BF16)
    gk = lp['mla_k_g']
    return dict(
        qag=lp['mla_qa_g'][None, :], wqb=pad_h(lp['mla_wqb'], QK_DIM),
        qg=jnp.pad(lp['mla_q_g'], (0, MLA_HP - QK_DIM))[None, :], kvg=lp['mla_kv_g'][None, :],
        wkb=pad_h(lp['mla_wkb'].reshape(KV_LORA, -1), NOPE_DIM),
        kgn=jnp.pad(gk[:NOPE_DIM], (0, MLA_HP - NOPE_DIM))[None, :],
        kgr=jnp.pad(gk[NOPE_DIM:], (NOPE_DIM, MLA_HP - QK_DIM))[None, :],
        ind=_head_ind(MLA_HEADS * MLA_HP, MLA_HP, MLA_HEADS),
        wvb=lp['mla_wvb'].reshape(KV_LORA, MLA_HEADS * V_DIM).astype(BF16))


def _mla_in_specs(tm, seq):
    tiles = max(seq // tm, 1)
    tab = pl.BlockSpec((tm, MLA_HP), lambda i: (i % tiles, 0))
    return [_tok_spec(tm, MLA_PAD), _full_spec((1, Q_LORA)), _full_spec((Q_LORA, MLA_HEADS * MLA_HP)),
            _full_spec((1, MLA_HP)), _full_spec((1, KV_LORA)), _full_spec((KV_LORA, MLA_HEADS * MLA_HP)),
            _full_spec((1, MLA_HP)), _full_spec((1, MLA_HP)), _full_spec((MLA_HEADS * MLA_HP, MLA_HEADS)),
            tab, tab, tab]


def mla_prompt_proj(mla_cols, pos, lp, B, seq, tm):
    T = B * seq
    w = _mla_weights(lp)
    tiles = seq // tm
    hspec = pl.BlockSpec((None, MLA_HEADS, tm, MLA_HP), lambda i: (i // tiles, 0, i % tiles, 0))
    q, k, v, ckv, ksc = pl.pallas_call(
        _mla_prompt_kernel,
        grid=(T // tm,),
        in_specs=_mla_in_specs(tm, seq) + [_full_spec((KV_LORA, MLA_HEADS * V_DIM))],
        out_specs=[hspec, hspec, _tok_spec(tm, MLA_HEADS * V_DIM), _tok_spec(tm, KV_LORA), _tok_spec(tm, MLA_HEADS)],
        out_shape=[jax.ShapeDtypeStruct((B, MLA_HEADS, seq, MLA_HP), BF16)] * 2
        + [jax.ShapeDtypeStruct((T, MLA_HEADS * V_DIM), BF16), jax.ShapeDtypeStruct((T, KV_LORA), F32),
           jax.ShapeDtypeStruct((T, MLA_HEADS), F32)],
        compiler_params=_cparams(("parallel",)),
        name="mla_prompt_proj",
    )(mla_cols, w['qag'], w['wqb'], w['qg'], w['kvg'], w['wkb'], w['kgn'], w['kgr'], w['ind'],
      *_rope_tables(pos), w['wvb'])
    return q, k, v.reshape(B, seq, -1), ckv, ksc


def _flash_kernel(q_ref, k_ref, v_ref, o_ref, m_scr, l_scr, acc_scr):
    qi, ki = pl.program_id(2), pl.program_id(3)
    tq, tk = q_ref.shape[1], k_ref.shape[1]

    @pl.when(ki == 0)
    def _():
        m_scr[...] = jnp.full(m_scr.shape, -jnp.inf, F32)
        l_scr[...] = jnp.zeros(l_scr.shape, F32)
        acc_scr[...] = jnp.zeros(acc_scr.shape, F32)

    def block(on_diagonal):
        for h in range(2):
            s = lax.dot_general(q_ref[h], k_ref[h], (((1,), (1,)), ((), ())), preferred_element_type=F32)
            if on_diagonal:
                keep = lax.broadcasted_iota(jnp.int32, (tq, tk), 1) <= lax.broadcasted_iota(jnp.int32, (tq, tk), 0)
                s = jnp.where(keep, s, -jnp.inf)
            m_old = m_scr[h]
            m_new = jnp.maximum(m_old, jnp.max(s, axis=-1, keepdims=True))
            alpha = jnp.exp(m_old - m_new)
            p = jnp.exp(s - m_new)
            l_scr[h] = alpha * l_scr[h] + jnp.sum(p, axis=-1, keepdims=True)
            acc_scr[h] = alpha * acc_scr[h] + jnp.dot(p.astype(BF16), v_ref[:, h * V_DIM:(h + 1) * V_DIM],
                                                      preferred_element_type=F32)
            m_scr[h] = m_new

    @pl.when(ki < qi)
    def _():
        block(False)

    @pl.when(ki == qi)
    def _():
        block(True)
        o_ref[...] = jnp.concatenate([acc_scr[h] / l_scr[h] for h in range(2)], axis=-1)


def flash_attention(q, k, v, tq):
    B, H, S, _ = q.shape
    n = S // tq
    return pl.pallas_call(
        _flash_kernel,
        grid=(B, H // 2, n, n),
        in_specs=[pl.BlockSpec((None, 2, tq, MLA_HP), lambda b, h, qi, ki: (b, h, qi, 0)),
                  pl.BlockSpec((None, 2, tq, MLA_HP), lambda b, h, qi, ki: (b, h, jnp.minimum(ki, qi), 0)),
                  pl.BlockSpec((None, tq, 2 * V_DIM), lambda b, h, qi, ki: (b, jnp.minimum(ki, qi), h))],
        out_specs=pl.BlockSpec((None, tq, 2 * V_DIM), lambda b, h, qi, ki: (b, qi, h)),
        out_shape=jax.ShapeDtypeStruct((B, S, H * V_DIM), F32),
        scratch_shapes=[pltpu.VMEM((2, tq, 1), F32), pltpu.VMEM((2, tq, 1), F32), pltpu.VMEM((2, tq, V_DIM), F32)],
        compiler_params=_cparams(("parallel", "parallel", "parallel", "arbitrary")),
        name="flash_attention",
    )(q, k, v)


MLA_QCAT = KV_LORA + MLA_HP


def _mla_sample_kernel(c_ref, qag_ref, wqb_ref, qg_ref, kvg_ref, wkb_ref, kgn_ref, kgr_ref, ind_ref,
                       cos_ref, sup_ref, sdn_ref, q_ref, knew_ref, ckv_ref, ksc_ref):
    qs, ckv, kn, kr, ksc = _mla_common(c_ref, qag_ref, wqb_ref, qg_ref, kvg_ref, wkb_ref, kgn_ref, kgr_ref,
                                       ind_ref, cos_ref, sup_ref, sdn_ref)
    tm = ckv.shape[0]
    rope_lanes = lax.broadcasted_iota(jnp.int32, (tm, MLA_HP), 1) >= NOPE_DIM
    wkb = wkb_ref[...]
    for h in range(MLA_HEADS):
        qh = qs[h] * ATTN_SCALE
        q_lat = lax.dot_general((qh * kgn_ref[...]).astype(BF16), wkb[:, h * MLA_HP:(h + 1) * MLA_HP],
                                (((1,), (1,)), ((), ())), preferred_element_type=F32)
        q_ref[:, h * MLA_QCAT:h * MLA_QCAT + KV_LORA] = q_lat.astype(BF16)
        q_ref[:, h * MLA_QCAT + KV_LORA:(h + 1) * MLA_QCAT] = jnp.where(rope_lanes, qh, 0.0).astype(BF16)
    knew_ref[:, :KV_LORA] = ckv.astype(BF16)
    knew_ref[:, KV_LORA:] = kr.astype(BF16)
    ckv_ref[...] = ckv
    ksc_ref[...] = ksc


def mla_sample_proj(mla_cols, pos, lp, seq, tm):
    T = mla_cols.shape[0]
    w = _mla_weights(lp)
    pos_tok = jnp.tile(pos, tm // seq)
    tables = _rope_tables(pos_tok)
    specs = _mla_in_specs(tm, tm)
    return pl.pallas_call(
        _mla_sample_kernel,
        grid=(T // tm,),
        in_specs=specs,
        out_specs=[_tok_spec(tm, MLA_HEADS * MLA_QCAT), _tok_spec(tm, MLA_QCAT), _tok_spec(tm, KV_LORA),
                   _tok_spec(tm, MLA_HEADS)],
        out_shape=[jax.ShapeDtypeStruct((T, MLA_HEADS * MLA_QCAT), BF16), jax.ShapeDtypeStruct((T, MLA_QCAT), BF16),
                   jax.ShapeDtypeStruct((T, KV_LORA), F32), jax.ShapeDtypeStruct((T, MLA_HEADS), F32)],
        compiler_params=_cparams(("parallel",)),
        name="mla_sample_proj",
    )(mla_cols, w['qag'], w['wqb'], w['qg'], w['kvg'], w['wkb'], w['kgn'], w['kgr'], w['ind'], *tables)


PAGES_PER_STEP = 32


def _paged_attn_kernel(pt_ref, q_ref, knew_ref, kscn_ref, cos_ref, sin_ref, kgr_ref, *rest):
    n = PAGES_PER_STEP
    c_refs, side_refs = rest[:n], rest[n:2 * n]
    o_ref, m_scr, l_scr, acc_scr = rest[2 * n:]
    j = pl.program_id(1)
    L = n * PAGE_SIZE
    rows = q_ref.shape[0]
    T = rows // MLA_HEADS
    half = ROPE_DIM // 2
    eye = (lax.broadcasted_iota(jnp.int32, (MLA_HEADS, MLA_HEADS), 0)
           == lax.broadcasted_iota(jnp.int32, (MLA_HEADS, MLA_HEADS), 1)).astype(F32)
    nt = (((1,), (1,)), ((), ()))

    @pl.when(j == 0)
    def _():
        m_scr[...] = jnp.full(m_scr.shape, -jnp.inf, F32)
        l_scr[...] = jnp.zeros(l_scr.shape, F32)
        acc_scr[...] = jnp.zeros(acc_scr.shape, F32)

    def update(s, value_fn):
        m_old = m_scr[...]
        m_new = jnp.maximum(m_old, jnp.max(s, axis=-1, keepdims=True))
        alpha = jnp.exp(m_old - m_new)
        p = jnp.exp(s - m_new)
        l_scr[...] = alpha * l_scr[...] + jnp.sum(p, axis=-1, keepdims=True)
        acc_scr[...] = alpha * acc_scr[...] + value_fn(p)
        m_scr[...] = m_new

    q_lat = q_ref[:, :KV_LORA]
    q_rope = q_ref[:, KV_LORA + NOPE_DIM:KV_LORA + QK_DIM]
    s_parts, c_parts = [], []
    for i in range(n):
        c_i = c_refs[i][...].astype(BF16)
        x = side_refs[i][:ROPE_DIM, :] * kgr_ref[...]
        cs, sn = cos_ref[j * n + i], sin_ref[j * n + i]
        x1, x2 = x[:half], x[half:]
        kr = jnp.concatenate([x1 * cs - x2 * sn, x1 * sn + x2 * cs], axis=0).astype(BF16)
        s_i = (lax.dot_general(q_lat, c_i, nt, preferred_element_type=F32)
               + jnp.dot(q_rope, kr, preferred_element_type=F32))
        ksc_i = side_refs[i][ROPE_DIM:, :]
        s_parts.append((s_i.reshape(T, MLA_HEADS, PAGE_SIZE) * ksc_i[None]).reshape(rows, PAGE_SIZE))
        c_parts.append(c_i)
    s = jnp.concatenate(s_parts, axis=1)
    update(s, lambda p: sum(jnp.dot(p[:, i * PAGE_SIZE:(i + 1) * PAGE_SIZE].astype(BF16), c_parts[i],
                                    preferred_element_type=F32) for i in range(n)))

    @pl.when(j == pl.num_programs(1) - 1)
    def _():
        knew = knew_ref[...]
        s_new = (lax.dot_general(q_lat, knew[:, :KV_LORA], nt, preferred_element_type=F32)
                 + lax.dot_general(q_rope, knew[:, KV_LORA + NOPE_DIM:KV_LORA + QK_DIM], nt,
                                   preferred_element_type=F32))
        kscn_t = lax.dot_general(eye, kscn_ref[...], nt, precision=lax.Precision.HIGHEST,
                                 preferred_element_type=F32)
        s_new = (s_new.reshape(T, MLA_HEADS, T) * kscn_t[None]).reshape(rows, T)
        q_tok = lax.broadcasted_iota(jnp.int32, (rows, T), 0) // MLA_HEADS
        s_new = jnp.where(lax.broadcasted_iota(jnp.int32, (rows, T), 1) <= q_tok, s_new, -jnp.inf)
        c_new = knew[:, :KV_LORA].astype(F32)

        def new_values(p):
            pb = p.astype(BF16).astype(F32)
            return sum(pb[:, t:t + 1] * c_new[t:t + 1, :] for t in range(T))

        update(s_new, new_values)
        o_ref[...] = acc_scr[...] / l_scr[...]


def paged_attention(q, knew, ksc_new, cache_ckv, cache_side, page_table, layer, gk_rope):
    B, rows, _ = q.shape
    T = rows // MLA_HEADS
    n_pages = page_table.shape[1]
    n = PAGES_PER_STEP
    half = ROPE_DIM // 2
    inv = jnp.power(ROPE_BASE, -jnp.arange(half, dtype=F32) / half)
    pos = jnp.arange(n_pages * PAGE_SIZE).astype(F32).reshape(n_pages, 1, PAGE_SIZE)
    ang = pos * inv[None, :, None]
    cos3, sin3 = jnp.cos(ang), jnp.sin(ang)

    def page_spec(shape, i):
        return pl.BlockSpec((None, None) + shape, lambda b, j, pt: (layer, pt[b, j * n + i], 0, 0))

    grid_spec = pltpu.PrefetchScalarGridSpec(
        num_scalar_prefetch=1,
        grid=(B, n_pages // n),
        in_specs=[pl.BlockSpec((None, rows, MLA_QCAT), lambda b, j, pt: (b, 0, 0)),
                  pl.BlockSpec((None, T, MLA_QCAT), lambda b, j, pt: (b, 0, 0)),
                  pl.BlockSpec((None, T, MLA_HEADS), lambda b, j, pt: (b, 0, 0)),
                  pl.BlockSpec((n_pages, half, PAGE_SIZE), lambda b, j, pt: (0, 0, 0)),
                  pl.BlockSpec((n_pages, half, PAGE_SIZE), lambda b, j, pt: (0, 0, 0)),
                  pl.BlockSpec((ROPE_DIM, 1), lambda b, j, pt: (0, 0))]
        + [page_spec((PAGE_SIZE, KV_LORA), i) for i in range(n)]
        + [page_spec((ROPE_DIM + MLA_HEADS, PAGE_SIZE), i) for i in range(n)],
        out_specs=pl.BlockSpec((None, rows, KV_LORA), lambda b, j, pt: (b, 0, 0)),
        scratch_shapes=[pltpu.VMEM((rows, 1), F32), pltpu.VMEM((rows, 1), F32), pltpu.VMEM((rows, KV_LORA), F32)],
    )
    return pl.pallas_call(
        _paged_attn_kernel,
        grid_spec=grid_spec,
        out_shape=jax.ShapeDtypeStruct((B, rows, KV_LORA), F32),
        compiler_params=_cparams(("parallel", "arbitrary")),
        name="paged_attention",
    )(page_table, q, knew, ksc_new, cos3, sin3, gk_rope[:, None],
      *([cache_ckv] * n), *([cache_side] * n))


def _vproj_kernel(o_ref, w_ref, out_ref):
    out_ref[...] = jnp.concatenate(
        [jnp.dot(o_ref[:, h * KV_LORA:(h + 1) * KV_LORA].astype(BF16), w_ref[h], preferred_element_type=F32)
         for h in range(MLA_HEADS)], axis=-1)


def mla_value_proj(o_lat, wvb):
    T = o_lat.shape[0]
    w = wvb.transpose(1, 0, 2).astype(BF16)
    return pl.pallas_call(
        _vproj_kernel,
        grid=(1,),
        in_specs=[_full_spec((T, MLA_HEADS * KV_LORA)), _full_spec((MLA_HEADS, KV_LORA, V_DIM))],
        out_specs=_full_spec((T, MLA_HEADS * V_DIM)),
        out_shape=jax.ShapeDtypeStruct((T, MLA_HEADS * V_DIM), F32),
        compiler_params=_cparams(("arbitrary",)),
        name="mla_value_proj",
    )(o_lat, w)


SSM_BC = SSM_GROUPS * SSM_STATE


def _softplus(x):
    return jnp.maximum(x, 0.0) + jnp.log(1.0 + jnp.exp(-jnp.abs(x)))


def _silu(x):
    return x * jax.nn.sigmoid(x)


def _gated_norm(y, z, g):
    y = y * _silu(z)
    return y * lax.rsqrt(jnp.mean(y * y, axis=-1, keepdims=True) + EPS) * g


def _ssd_prompt_kernel(c_ref, buf_ref, s0_ref, cw_ref, cb_ref, dtb_ref, a_ref, d_ref, ng_ref,
                       o_ref, conv_ref, sT_ref, carry_scr, s_scr):
    ci = pl.program_id(1)
    Lc = c_ref.shape[0]

    @pl.when(ci == 0)
    def _():
        carry_scr[...] = buf_ref[...]
        s_scr[...] = s0_ref[...]

    cols = c_ref[...]
    z = cols[:, :SSM_W]
    xbc = cols[:, SSM_W:SSM_W + CONV_CH]
    dtr = cols[:, SSM_W + CONV_CH:SSM_W + CONV_CH + LANE]
    ext = jnp.concatenate([carry_scr[...], xbc], axis=0)
    conv = cb_ref[...]
    for i in range(CONV_W):
        lo = 8 - (CONV_W - 1) + i
        conv = conv + ext[lo:lo + Lc, :] * cw_ref[i:i + 1, :]
    carry_scr[...] = xbc[Lc - 8:, :]
    conv = _silu(conv)
    xs, Bm, Cm = conv[:, :SSM_W], conv[:, SSM_W:SSM_W + SSM_BC], conv[:, SSM_W + SSM_BC:]
    dt = _softplus(dtr + dtb_ref[...])
    a = dt * a_ref[...]
    ri = lax.broadcasted_iota(jnp.int32, (Lc, Lc), 0)
    cj = lax.broadcasted_iota(jnp.int32, (Lc, Lc), 1)
    causal = cj <= ri
    acs = jnp.dot(causal.astype(F32), a, precision=lax.Precision.HIGHEST, preferred_element_type=F32)
    acs_t = acs.T
    nt = (((1,), (1,)), ((), ()))
    ys = []
    for g in range(SSM_GROUPS):
        Bg = Bm[:, g * SSM_STATE:(g + 1) * SSM_STATE].astype(BF16)
        Cg = Cm[:, g * SSM_STATE:(g + 1) * SSM_STATE].astype(BF16)
        cb = lax.dot_general(Cg, Bg, nt, preferred_element_type=F32)
        for e in range(SSM_EPG):
            hd = g * SSM_EPG + e
            col = acs[:, hd:hd + 1]
            lmat = jnp.exp(jnp.where(causal, col - acs_t[hd:hd + 1, :], -jnp.inf))
            x_e = xs[:, hd * SSM_HD:(hd + 1) * SSM_HD]
            xdt = x_e * dt[:, hd:hd + 1]
            y = jnp.dot((cb * lmat).astype(BF16), xdt.astype(BF16), preferred_element_type=F32)
            last = acs[Lc - 1:Lc, hd:hd + 1]
            s_in = s_scr[hd]
            y = y + lax.dot_general(Cg, s_in.astype(BF16), nt, preferred_element_type=F32) * jnp.exp(col)
            st = lax.dot_general((xdt * jnp.exp(last - col)).astype(BF16), Bg, (((0,), (0,)), ((), ())),
                                 preferred_element_type=F32)
            s_scr[hd] = s_in * jnp.exp(last) + st
            ys.append(y + x_e * d_ref[:, hd:hd + 1])
    o_ref[...] = _gated_norm(jnp.concatenate(ys, axis=-1), z, ng_ref[...])

    @pl.when(ci == pl.num_programs(1) - 1)
    def _():
        conv_ref[...] = xbc[Lc - (CONV_W - 1):, :]
        sT_ref[...] = s_scr[...]


def _ssm_params(lp):
    pad = lambda v: jnp.pad(v, (0, LANE - SSM_HEADS))[None, :]
    return dict(cw=lp['ssm_conv_w'], cb=lp['ssm_conv_b'][None, :], dtb=pad(lp['ssm_dt_bias']),
                a=pad(-jnp.exp(lp['ssm_a_log'])), d=pad(lp['ssm_d']), ng=lp['ssm_norm_g'][None, :])


def ssd_prompt(ssm_cols, conv_buf, s0, lp, B, seq):
    Lc = SSD_CHUNK
    nc = seq // Lc
    p = _ssm_params(lp)
    buf8 = jnp.pad(conv_buf, ((0, 0), (8 - (CONV_W - 1), 0), (0, 0)))
    st_spec = pl.BlockSpec((None, SSM_HEADS, SSM_HD, SSM_STATE), lambda b, c: (b, 0, 0, 0))
    return pl.pallas_call(
        _ssd_prompt_kernel,
        grid=(B, nc),
        in_specs=[pl.BlockSpec((Lc, SSM_PAD), lambda b, c: (b * nc + c, 0)),
                  pl.BlockSpec((None, 8, CONV_CH), lambda b, c: (b, 0, 0)), st_spec,
                  _full_spec((CONV_W, CONV_CH)), _full_spec((1, CONV_CH)), _full_spec((1, LANE)),
                  _full_spec((1, LANE)), _full_spec((1, LANE)), _full_spec((1, SSM_W))],
        out_specs=[pl.BlockSpec((Lc, SSM_W), lambda b, c: (b * nc + c, 0)),
                   pl.BlockSpec((None, CONV_W - 1, CONV_CH), lambda b, c: (b, 0, 0)), st_spec],
        out_shape=[jax.ShapeDtypeStruct((B * seq, SSM_W), F32), jax.ShapeDtypeStruct((B, CONV_W - 1, CONV_CH), F32),
                   jax.ShapeDtypeStruct((B, SSM_HEADS, SSM_HD, SSM_STATE), F32)],
        scratch_shapes=[pltpu.VMEM((8, CONV_CH), F32), pltpu.VMEM((SSM_HEADS, SSM_HD, SSM_STATE), F32)],
        compiler_params=_cparams(("parallel", "arbitrary")),
        name="ssd_prompt",
    )(ssm_cols, buf8, s0, p['cw'], p['cb'], p['dtb'], p['a'], p['d'], p['ng'])


def _ssd_step_prep_kernel(c_ref, x0_ref, x1_ref, x2_ref, cw_ref, cb_ref, dtb_ref, a_ref, exp_ref, expg_ref,
                          w_ref, k_ref, r_ref, v_ref, xs_ref):
    cols = c_ref[...]
    xbc = cols[:, SSM_W:SSM_W + CONV_CH]
    dtr = cols[:, SSM_W + CONV_CH:SSM_W + CONV_CH + LANE]
    conv = cb_ref[...] + x0_ref[...] * cw_ref[0:1, :] + x1_ref[...] * cw_ref[1:2, :] + x2_ref[...] * cw_ref[2:3, :] \
        + xbc * cw_ref[3:4, :]
    conv = _silu(conv)
    xs, Bm, Cm = conv[:, :SSM_W], conv[:, SSM_W:SSM_W + SSM_BC], conv[:, SSM_W + SSM_BC:]
    dt = _softplus(dtr + dtb_ref[...])
    hi = lax.Precision.HIGHEST
    dt_h = jnp.dot(dt, exp_ref[...], precision=hi, preferred_element_type=F32)
    a_h = jnp.dot(dt * a_ref[...], exp_ref[...], precision=hi, preferred_element_type=F32)
    w_ref[...] = jnp.exp(a_h)
    k_ref[...] = jnp.dot(Bm, expg_ref[...], precision=hi, preferred_element_type=F32)
    r_ref[...] = jnp.dot(Cm, expg_ref[...], precision=hi, preferred_element_type=F32)
    v_ref[...] = xs * dt_h
    xs_ref[...] = xs


def _ssd_step_post_kernel(y_ref, xs_ref, c_ref, dh_ref, ng_ref, o_ref):
    y = y_ref[...] + xs_ref[...] * dh_ref[...]
    o_ref[...] = _gated_norm(y, c_ref[:, :SSM_W], ng_ref[...])


def ssd_decode(ssm_cols, conv_buf, s0, lp, B, seq, tm):
    T = B * seq
    p = _ssm_params(lp)
    xbc = ssm_cols[:, SSM_W:SSM_W + CONV_CH].reshape(B, seq, CONV_CH)
    xpad = jnp.concatenate([conv_buf, xbc], axis=1)
    shifted = [xpad[:, i:i + seq].reshape(T, CONV_CH) for i in range(CONV_W - 1)]
    head_of_lane = jnp.arange(SSM_W) // SSM_HD
    expand = (jnp.arange(LANE)[:, None] == head_of_lane[None, :]).astype(F32)
    src = (head_of_lane // SSM_EPG) * SSM_STATE + jnp.arange(SSM_W) % SSM_STATE
    expand_g = (jnp.arange(SSM_BC)[:, None] == src[None, :]).astype(F32)
    out = jax.ShapeDtypeStruct((T, SSM_W), F32)
    w, k, r, v, xs = pl.pallas_call(
        _ssd_step_prep_kernel,
        grid=(T // tm,),
        in_specs=[_tok_spec(tm, SSM_PAD)] + [_tok_spec(tm, CONV_CH)] * 3
        + [_full_spec((CONV_W, CONV_CH)), _full_spec((1, CONV_CH)), _full_spec((1, LANE)), _full_spec((1, LANE)),
           _full_spec((LANE, SSM_W)), _full_spec((SSM_BC, SSM_W))],
        out_specs=[_tok_spec(tm, SSM_W)] * 5,
        out_shape=[out] * 5,
        compiler_params=_cparams(("parallel",)),
        name="ssd_step_prep",
    )(ssm_cols, *shifted, p['cw'], p['cb'], p['dtb'], p['a'], expand, expand_g)
    sh = lambda t: t.reshape(B, seq, SSM_W)
    zeros = jnp.zeros((B, seq, SSM_W), F32)
    y, sT = state_scan(sh(w), zeros, zeros, sh(k), sh(r), sh(v), s0, seq)
    d_h = jnp.repeat(lp['ssm_d'], SSM_HD)[None, :]
    o = pl.pallas_call(
        _ssd_step_post_kernel,
        grid=(T // tm,),
        in_specs=[_tok_spec(tm, SSM_W), _tok_spec(tm, SSM_W), _tok_spec(tm, SSM_PAD), _full_spec((1, SSM_W)),
                  _full_spec((1, SSM_W))],
        out_specs=_tok_spec(tm, SSM_W),
        out_shape=out,
        compiler_params=_cparams(("parallel",)),
        name="ssd_step_post",
    )(y.reshape(T, SSM_W), xs, ssm_cols, d_h, p['ng'])
    return o, xpad[:, -(CONV_W - 1):], sT


def _merge_kernel(oa_ref, ob_ref, oc_ref, gate_ref, x_ref, g1_ref, wb_ref, wo_ref, o_ref):
    D = x_ref.shape[1]
    acc = jnp.zeros(x_ref.shape, F32)
    for i, ref in enumerate((oa_ref, ob_ref, oc_ref)):
        u = jnp.dot(ref[...].astype(BF16), wb_ref[i], preferred_element_type=F32)
        acc = acc + jax.nn.sigmoid(gate_ref[:, i * D:(i + 1) * D]) * u
    y = jnp.dot(acc.astype(BF16), wo_ref[...], preferred_element_type=F32)
    o_ref[...] = x_ref[...] + g1_ref[...] * y


def branch_merge(o_a, o_b, o_c, gates, x2, g1, wb_bf, wo_bf, seq, tm):
    T, D = x2.shape
    return pl.pallas_call(
        _merge_kernel,
        grid=(T // tm,),
        in_specs=[_tok_spec(tm, BR_W)] * 3 + [_tok_spec(tm, N_BRANCH * D), _tok_spec(tm, D), _row_spec(seq, tm, D),
                                              _full_spec((N_BRANCH, BR_W, D)), _full_spec((D, D))],
        out_specs=_tok_spec(tm, D),
        out_shape=jax.ShapeDtypeStruct((T, D), F32),
        compiler_params=_cparams(("parallel",)),
        name="branch_merge",
    )(o_a, o_b, o_c, gates, x2, _row_operand(g1, seq, tm), wb_bf, wo_bf)


TM_TOKENS = 512
TM_INPROJ = 256
TM_ROUTE = 256
SCAN_CHUNK = 32
FLASH_BLOCK = 2048


def _layer(x2, B, seq, c, lw, st, pos, layer, decode_ctx):
    T = B * seq
    tm = min(TM_TOKENS, T)
    sh1, sc1, g1, sh2, sc2, g2 = jnp.split(ada_mod(c, lw['ada_w'], lw['ada_b'], layer), 6, axis=-1)
    lp = lw['lp']
    rw, mla, ssm, gates = in_proj(x2, sc1, sh1, lp['norm1_g'], lw['w_in_pad'], seq, min(TM_INPROJ, T))
    o_a, rw_shift, rw_state = rwkv_branch(rw, B, seq, st[0], st[1], lp, tm, min(SCAN_CHUNK, seq))
    if decode_ctx is None:
        q, k, v, ckv, ksc = mla_prompt_proj(mla, pos, lp, B, seq, min(tm, seq))
        o_b = flash_attention(q, k, v, min(FLASH_BLOCK, seq)).reshape(T, BR_W)
        o_c, conv_buf, ssm_state = ssd_prompt(ssm, st[2], st[3], lp, B, seq)
    else:
        cache_ckv, cache_side, page_table = decode_ctx
        q, knew, ckv, ksc = mla_sample_proj(mla, pos, lp, seq, tm)
        o_lat = paged_attention(q.reshape(B, seq * MLA_HEADS, MLA_QCAT), knew.reshape(B, seq, MLA_QCAT),
                                ksc.reshape(B, seq, MLA_HEADS), cache_ckv, cache_side, page_table,
                                layer, lp['mla_k_g'][NOPE_DIM:])
        o_b = mla_value_proj(o_lat.reshape(T, MLA_HEADS * KV_LORA), lp['mla_wvb'])
        o_c, conv_buf, ssm_state = ssd_decode(ssm, st[2], st[3], lp, B, seq, tm)
    x2 = branch_merge(o_a, o_b, o_c, gates, x2, g1, lw['wb_bf'], lw['wo_bf'], seq, tm)
    h_tiles, x_tiles, eid, gate = peer_route(x2, sc2, sh2, lp['norm2_g'], lw['wq_bf'], lw['keys_bf'], seq,
                                             min(TM_ROUTE, T))
    x2 = peer_gather(eid, h_tiles, gate, x_tiles, g2, seq, lw['uv'])
    kpe = mla[:, Q_LORA + KV_LORA:MLA_COLS]
    return x2, (ckv.reshape(B, seq, KV_LORA), kpe.reshape(B, seq, ROPE_DIM), ksc.reshape(B, seq, MLA_HEADS),
                rw_shift, rw_state, conv_buf, ssm_state)


def kernel(x_prompt, x_sample, cache_ckv, cache_kpe, cache_kscale, state_rwkv_shift, state_rwkv_wkv,
           state_ssm_conv, state_ssm, page_table, c_prompt, c_sample, ada_w, ada_b, norm1_g, norm2_g, w_in,
           rw_mu, rw_w0, rw_w2, rw_a0, rw_a2, rw_g2, rw_kk, rw_ka, rw_rk, rw_ln_w, rw_ln_b,
           mla_qa_g, mla_wqb, mla_kv_g, mla_wkb, mla_wvb, mla_q_g, mla_k_g,
           ssm_conv_w, ssm_conv_b, ssm_dt_bias, ssm_a_log, ssm_d, ssm_norm_g,
           w_branch, w_out, pk_wq, pk_keys, pk_u, pk_v):
    small = dict(norm1_g=norm1_g, norm2_g=norm2_g,
                 rw_mu=rw_mu, rw_w0=rw_w0, rw_w2=rw_w2, rw_a0=rw_a0, rw_a2=rw_a2, rw_g2=rw_g2,
                 rw_kk=rw_kk, rw_ka=rw_ka, rw_rk=rw_rk, rw_ln_w=rw_ln_w, rw_ln_b=rw_ln_b,
                 mla_qa_g=mla_qa_g, mla_wqb=mla_wqb, mla_kv_g=mla_kv_g, mla_wkb=mla_wkb, mla_wvb=mla_wvb,
                 mla_q_g=mla_q_g, mla_k_g=mla_k_g,
                 ssm_conv_w=ssm_conv_w, ssm_conv_b=ssm_conv_b, ssm_dt_bias=ssm_dt_bias, ssm_a_log=ssm_a_log,
                 ssm_d=ssm_d, ssm_norm_g=ssm_norm_g)
    depth = ada_w.shape[0]
    Bp, S, D = x_prompt.shape
    Bd, T, _ = x_sample.shape
    assert SSD_CHUNK % 8 == 0 and S % SSD_CHUNK == 0 and T < SSD_CHUNK
    past_len = page_table.shape[1] * PAGE_SIZE
    pos_p = jnp.arange(S)
    pos_s = past_len + jnp.arange(T)
    zero_state = (jnp.zeros((Bp, RW_COLS), F32), jnp.zeros((Bp, RW_HEADS, RW_HD, RW_HD), F32),
                  jnp.zeros((Bp, CONV_W - 1, CONV_CH), F32), jnp.zeros((Bp, SSM_HEADS, SSM_HD, SSM_STATE), F32))
    hp, hs = x_prompt.reshape(Bp * S, D), x_sample.reshape(Bd * T, D)
    cache_side = jnp.concatenate([jnp.swapaxes(cache_kpe, -1, -2), jnp.swapaxes(cache_kscale, -1, -2)], axis=-2)
    decode_ctx = (cache_ckv, cache_side, page_table)
    new_p, new_s = [], []
    for l in range(depth):
        lw = dict(lp={name: arr[l] for name, arr in small.items()}, ada_w=ada_w, ada_b=ada_b,
                  w_in_pad=_pad_w_in(w_in[l]), wb_bf=w_branch[l].astype(BF16), wo_bf=w_out[l].astype(BF16),
                  wq_bf=pk_wq[l].astype(BF16),
                  keys_bf=pk_keys[l].reshape(2 * PK_HEADS, PK_NKEYS, PK_DHALF).astype(BF16),
                  uv=peer_table(pk_u[l], pk_v[l]))
        hp, st_p = _layer(hp, Bp, S, c_prompt, lw, zero_state, pos_p, l, None)
        st_in = (state_rwkv_shift[l], state_rwkv_wkv[l], state_ssm_conv[l], state_ssm[l])
        hs, st_s = _layer(hs, Bd, T, c_sample, lw, st_in, pos_s, l, decode_ctx)
        new_p.append(st_p)
        new_s.append(st_s)
    hp, hs = hp.reshape(Bp, S, D), hs.reshape(Bd, T, D)
    return (hp, hs,
            jnp.stack([s[0] for s in new_p]), jnp.stack([s[0] for s in new_s]),
            jnp.stack([s[1] for s in new_p]), jnp.stack([s[1] for s in new_s]),
            jnp.stack([s[2] for s in new_p]), jnp.stack([s[2] for s in new_s]),
            jnp.stack([s[3] for s in new_p]), jnp.stack([s[3] for s in new_s]),
            jnp.stack([s[4] for s in new_p]), jnp.stack([s[4] for s in new_s]),
            jnp.stack([s[5] for s in new_p]), jnp.stack([s[5] for s in new_s]),
            jnp.stack([s[6] for s in new_p]), jnp.stack([s[6] for s in new_s]))
```

```python
import math
import jax, jax.numpy as jnp
from jax import lax
from jax.experimental import pallas as pl
from jax.experimental.pallas import tpu as pltpu

D_MODEL = 1024
PAGE_SIZE = 128
EPS = 1e-6
F32 = jnp.float32
BF16 = jnp.bfloat16
RW_HEADS = 8
RW_HD = 64
RW_W = RW_HEADS * RW_HD
RW_DECAY_LORA = 64
RW_AAA_LORA = 64
RW_GATE_LORA = 128
RW_COLS = 3 * RW_W + RW_DECAY_LORA + RW_AAA_LORA + RW_GATE_LORA
RW_SPLITS = (RW_W, 2 * RW_W, 3 * RW_W, 3 * RW_W + RW_DECAY_LORA, 3 * RW_W + RW_DECAY_LORA + RW_AAA_LORA)
RW_GN_EPS = 64e-5
MLA_HEADS = 8
Q_LORA = 256
KV_LORA = 256
NOPE_DIM = 64
ROPE_DIM = 32
V_DIM = 64
QK_DIM = NOPE_DIM + ROPE_DIM
MLA_COLS = Q_LORA + KV_LORA + ROPE_DIM
ROPE_BASE = 10000.0
ATTN_SCALE = QK_DIM ** -0.5
SSM_HEADS = 8
SSM_HD = 64
SSM_W = SSM_HEADS * SSM_HD
SSM_GROUPS = 2
SSM_EPG = SSM_HEADS // SSM_GROUPS
SSM_STATE = 64
CONV_W = 4
CONV_CH = SSM_W + 2 * SSM_GROUPS * SSM_STATE
SSM_COLS = SSM_W + CONV_CH + SSM_HEADS
SSD_CHUNK = 128
N_BRANCH = 3
BR_W = RW_W
IN_COLS = RW_COLS + MLA_COLS + SSM_COLS + N_BRANCH * D_MODEL
IN_SPLITS = (RW_COLS, RW_COLS + MLA_COLS, RW_COLS + MLA_COLS + SSM_COLS)
PK_HEADS = 8
PK_DKEY = 256
PK_DHALF = PK_DKEY // 2
PK_NKEYS = 128
PK_TOPK = 16

LANE = 128
VMEM_LIMIT = 56 * 1024 * 1024

MLA_PAD = 640
SSM_PAD = 1408
PROJ_OFF = (0, RW_COLS, RW_COLS + MLA_PAD, RW_COLS + MLA_PAD + SSM_PAD)
PROJ_COLS = PROJ_OFF[3] + N_BRANCH * D_MODEL


def _cparams(sem):
    return pltpu.CompilerParams(dimension_semantics=sem, vmem_limit_bytes=VMEM_LIMIT)


def _pad_w_in(w_in):
    rw, mla, ssm, gate = jnp.split(w_in, IN_SPLITS, axis=-1)
    mla = jnp.pad(mla, ((0, 0), (0, MLA_PAD - MLA_COLS)))
    ssm = jnp.pad(ssm, ((0, 0), (0, SSM_PAD - SSM_COLS)))
    return jnp.concatenate([rw, mla, ssm, gate], axis=-1).astype(BF16)


PROJ_WIDTHS = (RW_COLS, MLA_PAD, SSM_PAD, N_BRANCH * D_MODEL)


def _inproj_kernel(x_ref, sc_ref, sh_ref, g_ref, w_rw, w_mla, w_ssm, w_gate, o_rw, o_mla, o_ssm, o_gate):
    x = x_ref[...]
    y = x * lax.rsqrt(jnp.mean(x * x, axis=-1, keepdims=True) + EPS) * g_ref[...]
    h = (y * (1.0 + sc_ref[...]) + sh_ref[...]).astype(BF16)
    for w_ref, o_ref in ((w_rw, o_rw), (w_mla, o_mla), (w_ssm, o_ssm), (w_gate, o_gate)):
        o_ref[...] = jnp.dot(h, w_ref[...], preferred_element_type=F32)


def _row_spec(rows_per_group, tm, d):
    if rows_per_group >= tm:
        tiles = rows_per_group // tm
        return pl.BlockSpec((None, 1, d), lambda i, *_: (i // tiles, 0, 0))
    return pl.BlockSpec((None, tm, d), lambda i, *_: (i, 0, 0))


def _row_operand(v, seq, tm):
    if seq >= tm:
        return v[:, None, :]
    return jnp.repeat(v, seq, axis=0).reshape(-1, tm, v.shape[-1])


def in_proj(x2, sc, sh, g, w_pad, seq, tm):
    T, D = x2.shape
    ws = [w_pad[:, PROJ_OFF[i]:PROJ_OFF[i] + PROJ_WIDTHS[i]] for i in range(4)]
    return pl.pallas_call(
        _inproj_kernel,
        grid=(T // tm,),
        in_specs=[_tok_spec(tm, D), _row_spec(seq, tm, D), _row_spec(seq, tm, D), _full_spec((1, D))]
        + [_full_spec((D, n)) for n in PROJ_WIDTHS],
        out_specs=[_tok_spec(tm, n) for n in PROJ_WIDTHS],
        out_shape=[jax.ShapeDtypeStruct((T, n), F32) for n in PROJ_WIDTHS],
        compiler_params=_cparams(("parallel",)),
        name="in_proj",
    )(x2, _row_operand(sc, seq, tm), _row_operand(sh, seq, tm), g[None, :], *ws)


def _seg_sum(x, seg):
    hi = x.astype(BF16)
    lo = (x - hi.astype(F32)).astype(BF16)
    return (jnp.dot(hi, seg, preferred_element_type=F32) + jnp.dot(lo, seg, preferred_element_type=F32))


def _seg_matrix(width, seg):
    i = jnp.arange(width) // seg
    return (i[:, None] == i[None, :]).astype(BF16)


def _gelu_tanh(x):
    return 0.5 * x * (1.0 + jnp.tanh(math.sqrt(2.0 / math.pi) * (x + 0.044715 * (x * x * x))))


def _tok_spec(tm, d, col_block=0):
    return pl.BlockSpec((tm, d), lambda i, *_: (i, col_block))


def _full_spec(shape):
    n = len(shape)
    return pl.BlockSpec(shape, lambda *_: (0,) * n)


def _rwkv_prep_kernel(c_ref, xp_ref, mu_ref, w0_ref, w2_ref, a0_ref, a2_ref, g2_ref, kk_ref, ka_ref, seg_ref,
                      r_ref, w_ref, k_ref, v_ref, a_ref, b_ref, g_ref):
    c = c_ref[...]
    m = c + (xp_ref[...] - c) * mu_ref[...]
    r, k, v = m[:, :RW_W], m[:, RW_W:2 * RW_W], m[:, 2 * RW_W:3 * RW_W]
    wl = m[:, RW_SPLITS[2]:RW_SPLITS[3]]
    al = m[:, RW_SPLITS[3]:RW_SPLITS[4]]
    gl = m[:, RW_SPLITS[4]:]
    nz = -(w0_ref[...] + jnp.dot(jnp.tanh(wl), w2_ref[...], preferred_element_type=F32))
    softplus = jnp.maximum(nz, 0.0) + jnp.log(1.0 + jnp.exp(-jnp.abs(nz)))
    decay = jnp.exp(-jnp.exp(-softplus - 0.5))
    a = jax.nn.sigmoid(a0_ref[...] + jnp.dot(al, a2_ref[...], preferred_element_type=F32))
    g = jnp.dot(jax.nn.sigmoid(gl), g2_ref[...], preferred_element_type=F32)
    kk = k * kk_ref[...]
    k = k * (1.0 + (a - 1.0) * ka_ref[...])
    kk = kk / jnp.maximum(jnp.sqrt(_seg_sum(kk * kk, seg_ref[...])), 1e-12)
    r_ref[...] = r
    w_ref[...] = decay
    k_ref[...] = k
    v_ref[...] = v
    a_ref[...] = -kk
    b_ref[...] = kk * a
    g_ref[...] = g


def rwkv_prep(proj, xprev, lp, tm):
    T = proj.shape[0]
    row = lambda v: v.reshape(1, -1)
    out = jax.ShapeDtypeStruct((T, RW_W), F32)
    return pl.pallas_call(
        _rwkv_prep_kernel,
        grid=(T // tm,),
        in_specs=[_tok_spec(tm, RW_COLS), _tok_spec(tm, RW_COLS), _full_spec((1, RW_COLS)),
                  _full_spec((1, RW_W)), _full_spec((RW_DECAY_LORA, RW_W)),
                  _full_spec((1, RW_W)), _full_spec((RW_AAA_LORA, RW_W)),
                  _full_spec((RW_GATE_LORA, RW_W)), _full_spec((1, RW_W)), _full_spec((1, RW_W)),
                  _full_spec((RW_W, RW_W))],
        out_specs=[_tok_spec(tm, RW_W)] * 7,
        out_shape=[out] * 7,
        compiler_params=_cparams(("parallel",)),
        name="rwkv_prep",
    )(proj, xprev, row(lp['rw_mu']), row(lp['rw_w0']), lp['rw_w2'], row(lp['rw_a0']), lp['rw_a2'], lp['rw_g2'],
      row(lp['rw_kk']), row(lp['rw_ka']), _seg_matrix(RW_W, RW_HD))


SCAN_HEADS = 32
SCAN_VM = LANE // SCAN_HEADS
SCAN_VD = RW_HD // SCAN_VM


def _scan_kernel(w_ref, a_ref, b_ref, k_ref, r_ref, v_ref, s0_ref, y_ref, sT_ref, s_scr):
    c = pl.program_id(1)

    @pl.when(c == 0)
    def _():
        s_scr[...] = s0_ref[...]

    def step(t, carry):
        A, W, Bm, K, R = a_ref[t], w_ref[t], b_ref[t], k_ref[t], r_ref[t]
        V = v_ref[t]
        ys = []
        for vd in range(SCAN_VD):
            S = s_scr[vd]
            sa = jnp.sum(S * A, axis=0, keepdims=True)
            S = S * W + sa * Bm + V[vd:vd + 1, :] * K
            s_scr[vd] = S
            ys.append(jnp.sum(S * R, axis=0, keepdims=True))
        y_ref[t] = jnp.concatenate(ys, axis=0)
        return carry

    lax.fori_loop(0, y_ref.shape[0], step, 0)

    @pl.when(c == pl.num_programs(1) - 1)
    def _():
        sT_ref[...] = s_scr[...]


def _to_scan_kh(x, n_groups):
    B, T, _ = x.shape
    x = x.reshape(B, T, RW_HEADS, RW_HD).transpose(1, 3, 0, 2).reshape(T, RW_HD, n_groups, SCAN_HEADS)
    x = jnp.broadcast_to(x[..., None], (T, RW_HD, n_groups, SCAN_HEADS, SCAN_VM))
    return x.transpose(2, 0, 1, 3, 4).reshape(n_groups, T, RW_HD, LANE)


def _to_scan_v(x, n_groups):
    B, T, _ = x.shape
    x = x.reshape(B, T, RW_HEADS, SCAN_VD, SCAN_VM).transpose(1, 3, 0, 2, 4)
    x = x.reshape(T, SCAN_VD, n_groups, SCAN_HEADS * SCAN_VM)
    return x.transpose(2, 0, 1, 3)


def _from_scan_v(y, B):
    G, T = y.shape[:2]
    y = y.transpose(1, 2, 0, 3).reshape(T, SCAN_VD, B, RW_HEADS, SCAN_VM)
    return y.transpose(2, 0, 3, 1, 4).reshape(B, T, RW_W)


def _state_to_scan(s, n_groups):
    B = s.shape[0]
    s = s.reshape(B, RW_HEADS, SCAN_VD, SCAN_VM, RW_HD).transpose(2, 4, 0, 1, 3)
    s = s.reshape(SCAN_VD, RW_HD, n_groups, SCAN_HEADS * SCAN_VM)
    return s.transpose(2, 0, 1, 3)


def _state_from_scan(s, B):
    s = s.transpose(1, 2, 0, 3).reshape(SCAN_VD, RW_HD, B, RW_HEADS, SCAN_VM)
    return s.transpose(2, 3, 0, 4, 1).reshape(B, RW_HEADS, RW_HD, RW_HD)


def state_scan(w, a, b, k, r, v, s0, tc):
    B, T, _ = w.shape
    G = (B * RW_HEADS) // SCAN_HEADS
    kh = [_to_scan_kh(t, G) for t in (w, a, b, k, r)]
    vv = _to_scan_v(v, G)
    ss = _state_to_scan(s0, G)
    kh_spec = pl.BlockSpec((None, tc, RW_HD, LANE), lambda g, c: (g, c, 0, 0))
    v_spec = pl.BlockSpec((None, tc, SCAN_VD, LANE), lambda g, c: (g, c, 0, 0))
    s_spec = pl.BlockSpec((None, SCAN_VD, RW_HD, LANE), lambda g, c: (g, 0, 0, 0))
    y, sT = pl.pallas_call(
        _scan_kernel,
        grid=(G, T // tc),
        in_specs=[kh_spec] * 5 + [v_spec, s_spec],
        out_specs=[v_spec, s_spec],
        out_shape=[jax.ShapeDtypeStruct((G, T, SCAN_VD, LANE), F32),
                   jax.ShapeDtypeStruct((G, SCAN_VD, RW_HD, LANE), F32)],
        scratch_shapes=[pltpu.VMEM((SCAN_VD, RW_HD, LANE), F32)],
        compiler_params=_cparams(("parallel", "arbitrary")),
        name="state_scan",
    )(*kh, vv, ss)
    return _from_scan_v(y, B), _state_from_scan(sT, B)


def _rwkv_post_kernel(y_ref, r_ref, k_ref, v_ref, g_ref, lnw_ref, lnb_ref, rk_ref, seg_ref, o_ref):
    seg = seg_ref[...]
    y = y_ref[...]
    d = y - _seg_sum(y, seg) * (1.0 / RW_HD)
    var = _seg_sum(d * d, seg) * (1.0 / RW_HD)
    yn = d * lax.rsqrt(var + RW_GN_EPS) * lnw_ref[...] + lnb_ref[...]
    bonus = _seg_sum(r_ref[...] * k_ref[...] * rk_ref[...], seg) * v_ref[...]
    o_ref[...] = (yn + bonus) * g_ref[...]


def rwkv_post(y, r, k, v, g, lp, tm):
    T = y.shape[0]
    row = lambda t: t.reshape(1, -1)
    return pl.pallas_call(
        _rwkv_post_kernel,
        grid=(T // tm,),
        in_specs=[_tok_spec(tm, RW_W)] * 5 + [_full_spec((1, RW_W))] * 3 + [_full_spec((RW_W, RW_W))],
        out_specs=_tok_spec(tm, RW_W),
        out_shape=jax.ShapeDtypeStruct((T, RW_W), F32),
        compiler_params=_cparams(("parallel",)),
        name="rwkv_post",
    )(y, r, k, v, g, row(lp['rw_ln_w']), row(lp['rw_ln_b']), row(lp['rw_rk']), _seg_matrix(RW_W, RW_HD))


def rwkv_branch(rw2, B, T, prev_row, s0, lp, tm, tc):
    rw_cols = rw2.reshape(B, T, RW_COLS)
    xprev = jnp.concatenate([prev_row[:, None, :], rw_cols[:, :-1]], axis=1).reshape(B * T, RW_COLS)
    r, w, k, v, a, b, g = rwkv_prep(rw2, xprev, lp, tm)
    sh = lambda t: t.reshape(B, T, RW_W)
    y, sT = state_scan(sh(w), sh(a), sh(b), sh(k), sh(r), sh(v), s0, tc)
    o = rwkv_post(y.reshape(B * T, RW_W), r, k, v, g, lp, tm)
    return o, rw_cols[:, -1], sT


def _topk_rows(s, rows, k, payload=None):
    n = s.shape[0]
    vals, outs = [], []
    for _ in range(k):
        m = jnp.max(s, axis=0, keepdims=True)
        pos = jnp.min(jnp.where(s == m, rows, n), axis=0, keepdims=True)
        sel = rows == pos
        vals.append(m)
        outs.append(pos if payload is None else jnp.sum(jnp.where(sel, payload, 0), axis=0, keepdims=True))
        s = jnp.where(sel, -jnp.inf, s)
    return jnp.concatenate(vals, axis=0), jnp.concatenate(outs, axis=0)


_PAIR_BLOCKS = ((0, 1, PK_TOPK, PK_TOPK),) + tuple((a, 1, 8, PK_TOPK // (a + 1)) for a in range(1, 8)) + ((8, 8, 1, 8),)
_PAIR_ROWS = sum(max(na, nb) for _, na, nb, _ in _PAIR_BLOCKS)


def _peer_route_kernel(x_ref, sc_ref, sh_ref, g_ref, wq_ref, keys_ref, h_ref, xt_ref, eid_ref, gate_ref):
    x = x_ref[...]
    tm = x.shape[0]
    y = x * lax.rsqrt(jnp.mean(x * x, axis=-1, keepdims=True) + EPS) * g_ref[...]
    h = y * (1.0 + sc_ref[...]) + sh_ref[...]
    h_ref[...] = h.reshape(h_ref.shape)
    xt_ref[...] = x.reshape(xt_ref.shape)
    q = jnp.dot(h.astype(BF16), wq_ref[...], preferred_element_type=F32)
    rows_k = lax.broadcasted_iota(jnp.int32, (PK_NKEYS, tm), 0)
    rows_c = lax.broadcasted_iota(jnp.int32, (_PAIR_ROWS, tm), 0)
    eids, gates = [], []
    for hh in range(PK_HEADS):
        tops = []
        for i in range(2):
            c0 = (hh * 2 + i) * PK_DHALF
            s = lax.dot_general(keys_ref[hh * 2 + i], q[:, c0:c0 + PK_DHALF].astype(BF16),
                                (((1,), (1,)), ((), ())), preferred_element_type=F32)
            tops.append(_topk_rows(s, rows_k, PK_TOPK))
        (s0, i0), (s1, i1) = tops
        cs, ci = [], []
        for a0, na, nb, keep_n in _PAIR_BLOCKS:
            blk = s0[a0:a0 + na] + s1[0:nb]
            if keep_n < blk.shape[0]:
                blk = jnp.where(lax.broadcasted_iota(jnp.int32, blk.shape, 0) < keep_n, blk, -jnp.inf)
            cs.append(blk)
            ci.append(i0[a0:a0 + na] * PK_NKEYS + i1[0:nb])
        cand = jnp.concatenate(cs, axis=0)
        cid = jnp.concatenate(ci, axis=0)
        best, eid = _topk_rows(cand, rows_c, PK_TOPK, payload=cid)
        e = jnp.exp(best - best[0:1])
        gates.append(e / jnp.sum(e, axis=0, keepdims=True))
        eids.append(eid)
    eid_ref[...] = jnp.concatenate(eids, axis=0).T
    gate_ref[...] = jnp.concatenate(gates, axis=0).T


def peer_route(x2, sc, sh, norm_g, wq_bf, keys_bf, seq, tm):
    T, D = x2.shape
    sub = D // LANE
    nq = PK_HEADS * PK_DKEY
    nsel = PK_HEADS * PK_TOPK
    return pl.pallas_call(
        _peer_route_kernel,
        grid=(T // tm,),
        in_specs=[_tok_spec(tm, D), _row_spec(seq, tm, D), _row_spec(seq, tm, D), _full_spec((1, D)),
                  _full_spec((D, nq)), _full_spec((2 * PK_HEADS, PK_NKEYS, PK_DHALF))],
        out_specs=[_tok_spec(tm * sub, LANE), _tok_spec(tm * sub, LANE), _tok_spec(tm, nsel), _tok_spec(tm, nsel)],
        out_shape=[jax.ShapeDtypeStruct((T * sub, LANE), F32), jax.ShapeDtypeStruct((T * sub, LANE), F32),
                   jax.ShapeDtypeStruct((T, nsel), jnp.int32), jax.ShapeDtypeStruct((T, nsel), F32)],
        compiler_params=_cparams(("parallel",)),
        name="peer_route",
    )(x2, _row_operand(sc, seq, tm), _row_operand(sh, seq, tm), norm_g[None, :], wq_bf, keys_bf)


PEER_TOK = 256
PEER_SLOTS = 8
PEER_SEL = PK_HEADS * PK_TOPK


_HI16 = -65536
PEER_ROWS = D_MODEL // LANE


def _fold_pairs(x, shift):
    n = x.shape[0] // 2
    x = x.reshape(n, 2, 8, LANE)
    a, b = x[:, 0], x[:, 1]
    low = (lax.broadcasted_iota(jnp.int32, (n, 8, LANE), 1) & shift) == 0
    return jnp.where(low, a + pltpu.roll(a, 8 - shift, 1), b + pltpu.roll(b, shift, 1))


def _fold_order(r):
    s = r & 7
    return (r & ~7) | ((s & 1) << 2) | (s & 2) | ((s >> 2) & 1)


def _peer_gather_kernel(eid_ref, h_ref, gate_ref, x_ref, g2_ref, uv_hbm, o_ref,
                        eid_smem, w_scr, y_scr, *rest):
    bufs, (sem_idx, sem) = rest[:PEER_SLOTS], rest[PEER_SLOTS:]
    idx_copy = pltpu.make_async_copy(eid_ref, eid_smem, sem_idx)
    idx_copy.start()
    idx_copy.wait()
    ns = PEER_SLOTS
    n_tok = gate_ref.shape[0]
    n_groups = n_tok // ns
    rows = PEER_ROWS

    def issue(t, slot):
        base = t * PEER_SEL
        for kk in range(PEER_SEL):
            src = pl.multiple_of(eid_smem[base + kk] * rows, rows)
            pltpu.make_async_copy(uv_hbm.at[pl.ds(src, rows)], bufs[slot].at[pl.ds(kk * rows, rows)],
                                  sem.at[slot]).start(priority=kk % 2)

    def wait(slot):
        pltpu.make_async_copy(uv_hbm.at[pl.ds(0, PEER_SEL * rows)], bufs[slot], sem.at[slot]).wait()

    rr = lax.broadcasted_iota(jnp.int32, (PEER_SEL, PEER_SEL), 0)
    pick = _fold_order(rr) == lax.broadcasted_iota(jnp.int32, (PEER_SEL, PEER_SEL), 1)

    def compute(t, slot):
        buf = bufs[slot]
        t8 = pl.multiple_of(t * 8, 8)
        h = h_ref[pl.ds(t8, 8), :]
        u = lax.bitcast_convert_type(buf[...] & _HI16, F32).reshape(PEER_SEL, rows, LANE)
        part = _fold_pairs(_fold_pairs(_fold_pairs(u * h[None], 4), 2), 1)
        act = jnp.sum(part.reshape(PEER_SEL, LANE), axis=-1, keepdims=True)
        gate = jnp.sum(jnp.where(pick, gate_ref[pl.ds(t, 1), :], 0.0), axis=-1, keepdims=True)
        w_scr[...] = jnp.broadcast_to(gate * _gelu_tanh(act), (PEER_SEL, LANE))
        acc = [jnp.zeros((8, LANE), F32) for _ in range(4)]
        for r in range(PEER_SEL):
            kk = _fold_order(r)
            v = lax.bitcast_convert_type(buf[kk * rows:(kk + 1) * rows, :] << 16, F32)
            acc[r % 4] = acc[r % 4] + jnp.broadcast_to(w_scr[r:r + 1, :], (8, LANE)) * v
        y = (acc[0] + acc[1]) + (acc[2] + acc[3])
        g2 = g2_ref[...] if g2_ref.shape[0] == 8 else g2_ref[pl.ds(t8, 8), :]
        y_scr[pl.ds(t8, 8), :] = x_ref[pl.ds(t8, 8), :] + g2 * y

    for j in range(ns - 1):
        issue(j, j)

    def group(gi, carry):
        for j in range(ns):
            t = gi * ns + j
            wait(j)
            issue(t + ns - 1, (j + ns - 1) % ns)
            compute(t, j)
        return carry

    lax.fori_loop(0, n_groups - 1, group, 0)
    t0 = (n_groups - 1) * ns
    for j in range(ns):
        wait(j)
        if j == 0:
            issue(t0 + ns - 1, ns - 1)
        compute(t0 + j, j)
    o_ref[...] = y_scr[...].reshape(o_ref.shape)


def peer_table(pk_u, pk_v):
    E, D = pk_u.shape
    bits = lambda t: lax.bitcast_convert_type(t.astype(BF16), jnp.uint16).astype(jnp.uint32)
    packed = (bits(pk_u) << 16) | bits(pk_v)
    return lax.bitcast_convert_type(packed, jnp.int32).reshape(E * PEER_ROWS, LANE)


def peer_gather(eid, h_tiles, gate, x_tiles, g2, seq, uv):
    sub = PEER_ROWS
    T, D = h_tiles.shape[0] // sub, sub * LANE
    tb = min(PEER_TOK, T)
    assert T % tb == 0 and tb % PEER_SLOTS == 0 and tb // PEER_SLOTS >= 2 and sub == 8
    tiles = lambda a: a.reshape(-1, LANE)
    if seq >= tb:
        per_seq = seq // tb
        g2_op, g2_spec = tiles(g2), pl.BlockSpec((sub, LANE), lambda i: (i // per_seq, 0))
    else:
        g2_op, g2_spec = tiles(jnp.repeat(g2, seq, axis=0)), _tok_spec(tb * sub, LANE)
    return pl.pallas_call(
        _peer_gather_kernel,
        grid=(T // tb,),
        in_specs=[pl.BlockSpec((tb * PEER_SEL,), lambda i: (i,)), _tok_spec(tb * sub, LANE), _tok_spec(tb, PEER_SEL),
                  _tok_spec(tb * sub, LANE), g2_spec, pl.BlockSpec(memory_space=pl.ANY)],
        out_specs=_tok_spec(tb, D),
        out_shape=jax.ShapeDtypeStruct((T, D), F32),
        scratch_shapes=[pltpu.SMEM((tb * PEER_SEL,), jnp.int32), pltpu.VMEM((PEER_SEL, LANE), F32),
                        pltpu.VMEM((tb * sub, LANE), F32)]
        + [pltpu.VMEM((PEER_SEL * PEER_ROWS, LANE), jnp.int32)] * PEER_SLOTS
        + [pltpu.SemaphoreType.DMA, pltpu.SemaphoreType.DMA((PEER_SLOTS,))],
        compiler_params=_cparams(("arbitrary",)),
        name="peer_gather",
    )(eid.reshape(T * PEER_SEL), h_tiles, gate, x_tiles, g2_op, uv)


def _ada_kernel(c_ref, w_ref, b_ref, o_ref):
    c = c_ref[...]
    o_ref[...] = jnp.dot(c * jax.nn.sigmoid(c), w_ref[...], preferred_element_type=F32) + b_ref[...]


def ada_mod(c, ada_w, ada_b, layer):
    B, D = c.shape
    N = ada_w.shape[2]
    tn = D
    return pl.pallas_call(
        _ada_kernel,
        grid=(N // tn,),
        in_specs=[_full_spec((B, D)), pl.BlockSpec((None, D, tn), lambda j: (layer, 0, j)),
                  pl.BlockSpec((None, 1, tn), lambda j: (layer, 0, j))],
        out_specs=pl.BlockSpec((B, tn), lambda j: (0, j)),
        out_shape=jax.ShapeDtypeStruct((B, N), F32),
        compiler_params=_cparams(("parallel",)),
        name="ada_mod",
    )(c, ada_w, ada_b[:, None, :])


MLA_HP = LANE


def _rope_tables(pos):
    half = ROPE_DIM // 2
    inv = jnp.power(ROPE_BASE, -jnp.arange(half, dtype=F32) / half)
    ang = pos[:, None].astype(F32) * inv
    cos, sin = jnp.cos(ang), jnp.sin(ang)
    n = pos.shape[0]
    one, zero = jnp.ones((n, NOPE_DIM), F32), jnp.zeros((n, NOPE_DIM), F32)
    tail = jnp.zeros((n, MLA_HP - QK_DIM), F32)
    c = jnp.concatenate([one, cos, cos, tail], axis=-1)
    s_up = jnp.concatenate([zero, jnp.zeros_like(sin), sin, tail], axis=-1)
    s_dn = jnp.concatenate([zero, -sin, jnp.zeros_like(sin), tail], axis=-1)
    return c, s_up, s_dn


def _rope_apply(x, c, s_up, s_dn):
    half = ROPE_DIM // 2
    return x * c + pltpu.roll(x, half, 1) * s_up + pltpu.roll(x, MLA_HP - half, 1) * s_dn


def _head_ind(width, seg, n):
    return (jnp.arange(width)[:, None] // seg == jnp.arange(n)[None, :]).astype(BF16)


def _mla_common(c_ref, qag_ref, wqb_ref, qg_ref, kvg_ref, wkb_ref, kgn_ref, kgr_ref, ind_ref,
                cos_ref, sup_ref, sdn_ref):
    c = c_ref[...]
    cos, sup, sdn = cos_ref[...], sup_ref[...], sdn_ref[...]
    qa, ckv, kpe = c[:, :Q_LORA], c[:, Q_LORA:Q_LORA + KV_LORA], c[:, Q_LORA + KV_LORA:Q_LORA + KV_LORA + LANE]
    qa = qa * lax.rsqrt(jnp.mean(qa * qa, axis=-1, keepdims=True) + EPS) * qag_ref[...]
    q = jnp.dot(qa.astype(BF16), wqb_ref[...], preferred_element_type=F32)
    ind = ind_ref[...]
    qss = _seg_sum(q * q, ind) * (1.0 / QK_DIM)
    qs = []
    for h in range(MLA_HEADS):
        qh = q[:, h * MLA_HP:(h + 1) * MLA_HP] * lax.rsqrt(qss[:, h:h + 1] + EPS) * qg_ref[...]
        qs.append(_rope_apply(qh, cos, sup, sdn))
    ckv = ckv * lax.rsqrt(jnp.mean(ckv * ckv, axis=-1, keepdims=True) + EPS) * kvg_ref[...]
    kn = jnp.dot(ckv.astype(BF16), wkb_ref[...], preferred_element_type=F32)
    pe2 = jnp.sum(kpe * kpe, axis=-1, keepdims=True)
    ksc = lax.rsqrt((_seg_sum(kn * kn, ind) + pe2) * (1.0 / QK_DIM) + EPS)
    kr = _rope_apply(pltpu.roll(kpe, NOPE_DIM, 1) * kgr_ref[...], cos, sup, sdn)
    return qs, ckv, kn, kr, ksc


def _mla_prompt_kernel(c_ref, qag_ref, wqb_ref, qg_ref, kvg_ref, wkb_ref, kgn_ref, kgr_ref, ind_ref,
                       cos_ref, sup_ref, sdn_ref, wvb_ref, q_ref, k_ref, v_ref, ckv_ref, ksc_ref):
    qs, ckv, kn, kr, ksc = _mla_common(c_ref, qag_ref, wqb_ref, qg_ref, kvg_ref, wkb_ref, kgn_ref, kgr_ref,
                                       ind_ref, cos_ref, sup_ref, sdn_ref)
    for h in range(MLA_HEADS):
        q_ref[h] = (qs[h] * ATTN_SCALE).astype(BF16)
        kh = (kn[:, h * MLA_HP:(h + 1) * MLA_HP] * kgn_ref[...] + kr) * ksc[:, h:h + 1]
        k_ref[h] = kh.astype(BF16)
    v_ref[...] = jnp.dot(ckv.astype(BF16), wvb_ref[...], preferred_element_type=F32).astype(BF16)
    ckv_ref[...] = ckv
    ksc_ref[...] = ksc


def _mla_weights(lp):
    pad_h = lambda w, d: jnp.pad(w.reshape(w.shape[0], MLA_HEADS, d), ((0, 0), (0, 0), (0, MLA_HP - d))
                                 ).reshape(w.shape[0], MLA_HEADS * MLA_HP).astype(BF16)
    gk = lp['mla_k_g']
    return dict(
        qag=lp['mla_qa_g'][None, :], wqb=pad_h(lp['mla_wqb'], QK_DIM),
        qg=jnp.pad(lp['mla_q_g'], (0, MLA_HP - QK_DIM))[None, :], kvg=lp['mla_kv_g'][None, :],
        wkb=pad_h(lp['mla_wkb'].reshape(KV_LORA, -1), NOPE_DIM),
        kgn=jnp.pad(gk[:NOPE_DIM], (0, MLA_HP - NOPE_DIM))[None, :],
        kgr=jnp.pad(gk[NOPE_DIM:], (NOPE_DIM, MLA_HP - QK_DIM))[None, :],
        ind=_head_ind(MLA_HEADS * MLA_HP, MLA_HP, MLA_HEADS),
        wvb=lp['mla_wvb'].reshape(KV_LORA, MLA_HEADS * V_DIM).astype(BF16))


def _mla_in_specs(tm, seq):
    tiles = max(seq // tm, 1)
    tab = pl.BlockSpec((tm, MLA_HP), lambda i: (i % tiles, 0))
    return [_tok_spec(tm, MLA_PAD), _full_spec((1, Q_LORA)), _full_spec((Q_LORA, MLA_HEADS * MLA_HP)),
            _full_spec((1, MLA_HP)), _full_spec((1, KV_LORA)), _full_spec((KV_LORA, MLA_HEADS * MLA_HP)),
            _full_spec((1, MLA_HP)), _full_spec((1, MLA_HP)), _full_spec((MLA_HEADS * MLA_HP, MLA_HEADS)),
            tab, tab, tab]


def mla_prompt_proj(mla_cols, pos, lp, B, seq, tm):
    T = B * seq
    w = _mla_weights(lp)
    tiles = seq // tm
    hspec = pl.BlockSpec((None, MLA_HEADS, tm, MLA_HP), lambda i: (i // tiles, 0, i % tiles, 0))
    q, k, v, ckv, ksc = pl.pallas_call(
        _mla_prompt_kernel,
        grid=(T // tm,),
        in_specs=_mla_in_specs(tm, seq) + [_full_spec((KV_LORA, MLA_HEADS * V_DIM))],
        out_specs=[hspec, hspec, _tok_spec(tm, MLA_HEADS * V_DIM), _tok_spec(tm, KV_LORA), _tok_spec(tm, MLA_HEADS)],
        out_shape=[jax.ShapeDtypeStruct((B, MLA_HEADS, seq, MLA_HP), BF16)] * 2
        + [jax.ShapeDtypeStruct((T, MLA_HEADS * V_DIM), BF16), jax.ShapeDtypeStruct((T, KV_LORA), F32),
           jax.ShapeDtypeStruct((T, MLA_HEADS), F32)],
        compiler_params=_cparams(("parallel",)),
        name="mla_prompt_proj",
    )(mla_cols, w['qag'], w['wqb'], w['qg'], w['kvg'], w['wkb'], w['kgn'], w['kgr'], w['ind'],
      *_rope_tables(pos), w['wvb'])
    return q, k, v.reshape(B, seq, -1), ckv, ksc


def _flash_kernel(q_ref, k_ref, v_ref, o_ref, m_scr, l_scr, acc_scr):
    qi, ki = pl.program_id(2), pl.program_id(3)
    tq, tk = q_ref.shape[1], k_ref.shape[1]

    @pl.when(ki == 0)
    def _():
        m_scr[...] = jnp.full(m_scr.shape, -jnp.inf, F32)
        l_scr[...] = jnp.zeros(l_scr.shape, F32)
        acc_scr[...] = jnp.zeros(acc_scr.shape, F32)

    def block(on_diagonal):
        for h in range(2):
            s = lax.dot_general(q_ref[h], k_ref[h], (((1,), (1,)), ((), ())), preferred_element_type=F32)
            if on_diagonal:
                keep = lax.broadcasted_iota(jnp.int32, (tq, tk), 1) <= lax.broadcasted_iota(jnp.int32, (tq, tk), 0)
                s = jnp.where(keep, s, -jnp.inf)
            m_old = m_scr[h]
            m_new = jnp.maximum(m_old, jnp.max(s, axis=-1, keepdims=True))
            alpha = jnp.exp(m_old - m_new)
            p = jnp.exp(s - m_new)
            l_scr[h] = alpha * l_scr[h] + jnp.sum(p, axis=-1, keepdims=True)
            acc_scr[h] = alpha * acc_scr[h] + jnp.dot(p.astype(BF16), v_ref[:, h * V_DIM:(h + 1) * V_DIM],
                                                      preferred_element_type=F32)
            m_scr[h] = m_new

    @pl.when(ki < qi)
    def _():
        block(False)

    @pl.when(ki == qi)
    def _():
        block(True)
        o_ref[...] = jnp.concatenate([acc_scr[h] / l_scr[h] for h in range(2)], axis=-1)


def flash_attention(q, k, v, tq):
    B, H, S, _ = q.shape
    n = S // tq
    return pl.pallas_call(
        _flash_kernel,
        grid=(B, H // 2, n, n),
        in_specs=[pl.BlockSpec((None, 2, tq, MLA_HP), lambda b, h, qi, ki: (b, h, qi, 0)),
                  pl.BlockSpec((None, 2, tq, MLA_HP), lambda b, h, qi, ki: (b, h, jnp.minimum(ki, qi), 0)),
                  pl.BlockSpec((None, tq, 2 * V_DIM), lambda b, h, qi, ki: (b, jnp.minimum(ki, qi), h))],
        out_specs=pl.BlockSpec((None, tq, 2 * V_DIM), lambda b, h, qi, ki: (b, qi, h)),
        out_shape=jax.ShapeDtypeStruct((B, S, H * V_DIM), F32),
        scratch_shapes=[pltpu.VMEM((2, tq, 1), F32), pltpu.VMEM((2, tq, 1), F32), pltpu.VMEM((2, tq, V_DIM), F32)],
        compiler_params=_cparams(("parallel", "parallel", "parallel", "arbitrary")),
        name="flash_attention",
    )(q, k, v)


MLA_QCAT = KV_LORA + MLA_HP


def _mla_sample_kernel(c_ref, qag_ref, wqb_ref, qg_ref, kvg_ref, wkb_ref, kgn_ref, kgr_ref, ind_ref,
                       cos_ref, sup_ref, sdn_ref, q_ref, knew_ref, ckv_ref, ksc_ref):
    qs, ckv, kn, kr, ksc = _mla_common(c_ref, qag_ref, wqb_ref, qg_ref, kvg_ref, wkb_ref, kgn_ref, kgr_ref,
                                       ind_ref, cos_ref, sup_ref, sdn_ref)
    tm = ckv.shape[0]
    rope_lanes = lax.broadcasted_iota(jnp.int32, (tm, MLA_HP), 1) >= NOPE_DIM
    wkb = wkb_ref[...]
    for h in range(MLA_HEADS):
        qh = qs[h] * ATTN_SCALE
        q_lat = lax.dot_general((qh * kgn_ref[...]).astype(BF16), wkb[:, h * MLA_HP:(h + 1) * MLA_HP],
                                (((1,), (1,)), ((), ())), preferred_element_type=F32)
        q_ref[:, h * MLA_QCAT:h * MLA_QCAT + KV_LORA] = q_lat.astype(BF16)
        q_ref[:, h * MLA_QCAT + KV_LORA:(h + 1) * MLA_QCAT] = jnp.where(rope_lanes, qh, 0.0).astype(BF16)
    knew_ref[:, :KV_LORA] = ckv.astype(BF16)
    knew_ref[:, KV_LORA:] = kr.astype(BF16)
    ckv_ref[...] = ckv
    ksc_ref[...] = ksc


def mla_sample_proj(mla_cols, pos, lp, seq, tm):
    T = mla_cols.shape[0]
    w = _mla_weights(lp)
    pos_tok = jnp.tile(pos, tm // seq)
    tables = _rope_tables(pos_tok)
    specs = _mla_in_specs(tm, tm)
    return pl.pallas_call(
        _mla_sample_kernel,
        grid=(T // tm,),
        in_specs=specs,
        out_specs=[_tok_spec(tm, MLA_HEADS * MLA_QCAT), _tok_spec(tm, MLA_QCAT), _tok_spec(tm, KV_LORA),
                   _tok_spec(tm, MLA_HEADS)],
        out_shape=[jax.ShapeDtypeStruct((T, MLA_HEADS * MLA_QCAT), BF16), jax.ShapeDtypeStruct((T, MLA_QCAT), BF16),
                   jax.ShapeDtypeStruct((T, KV_LORA), F32), jax.ShapeDtypeStruct((T, MLA_HEADS), F32)],
        compiler_params=_cparams(("parallel",)),
        name="mla_sample_proj",
    )(mla_cols, w['qag'], w['wqb'], w['qg'], w['kvg'], w['wkb'], w['kgn'], w['kgr'], w['ind'], *tables)


PAGES_PER_STEP = 32


def _paged_attn_kernel(pt_ref, q_ref, knew_ref, kscn_ref, cos_ref, sin_ref, kgr_ref, *rest):
    n = PAGES_PER_STEP
    c_refs, side_refs = rest[:n], rest[n:2 * n]
    o_ref, m_scr, l_scr, acc_scr = rest[2 * n:]
    j = pl.program_id(1)
    rows = q_ref.shape[0]
    T = rows // MLA_HEADS
    half = ROPE_DIM // 2
    eye = (lax.broadcasted_iota(jnp.int32, (MLA_HEADS, MLA_HEADS), 0)
           == lax.broadcasted_iota(jnp.int32, (MLA_HEADS, MLA_HEADS), 1)).astype(F32)
    nt = (((1,), (1,)), ((), ()))

    @pl.when(j == 0)
    def _():
        m_scr[...] = jnp.full(m_scr.shape, -jnp.inf, F32)
        l_scr[...] = jnp.zeros(l_scr.shape, F32)
        acc_scr[...] = jnp.zeros(acc_scr.shape, F32)

    def update(s, value_fn):
        m_old = m_scr[...]
        m_new = jnp.maximum(m_old, jnp.max(s, axis=-1, keepdims=True))
        alpha = jnp.exp(m_old - m_new)
        p = jnp.exp(s - m_new)
        l_scr[...] = alpha * l_scr[...] + jnp.sum(p, axis=-1, keepdims=True)
        acc_scr[...] = alpha * acc_scr[...] + value_fn(p)
        m_scr[...] = m_new

    q_lat = q_ref[:, :KV_LORA]
    q_rope = q_ref[:, KV_LORA + NOPE_DIM:KV_LORA + QK_DIM]
    s_parts, c_parts = [], []
    for i in range(n):
        c_i = c_refs[i][...].astype(BF16)
        x = side_refs[i][:ROPE_DIM, :] * kgr_ref[...]
        cs, sn = cos_ref[j * n + i], sin_ref[j * n + i]
        x1, x2 = x[:half], x[half:]
        kr = jnp.concatenate([x1 * cs - x2 * sn, x1 * sn + x2 * cs], axis=0).astype(BF16)
        s_i = (lax.dot_general(q_lat, c_i, nt, preferred_element_type=F32)
               + jnp.dot(q_rope, kr, preferred_element_type=F32))
        ksc_i = side_refs[i][ROPE_DIM:, :]
        s_parts.append((s_i.reshape(T, MLA_HEADS, PAGE_SIZE) * ksc_i[None]).reshape(rows, PAGE_SIZE))
        c_parts.append(c_i)
    s = jnp.concatenate(s_parts, axis=1)
    update(s, lambda p: sum(jnp.dot(p[:, i * PAGE_SIZE:(i + 1) * PAGE_SIZE].astype(BF16), c_parts[i],
                                    preferred_element_type=F32) for i in range(n)))

    @pl.when(j == pl.num_programs(1) - 1)
    def _():
        knew = knew_ref[...]
        s_new = (lax.dot_general(q_lat, knew[:, :KV_LORA], nt, preferred_element_type=F32)
                 + lax.dot_general(q_rope, knew[:, KV_LORA + NOPE_DIM:KV_LORA + QK_DIM], nt,
                                   preferred_element_type=F32))
        kscn_t = lax.dot_general(eye, kscn_ref[...], nt, precision=lax.Precision.HIGHEST,
                                 preferred_element_type=F32)
        s_new = (s_new.reshape(T, MLA_HEADS, T) * kscn_t[None]).reshape(rows, T)
        q_tok = lax.broadcasted_iota(jnp.int32, (rows, T), 0) // MLA_HEADS
        s_new = jnp.where(lax.broadcasted_iota(jnp.int32, (rows, T), 1) <= q_tok, s_new, -jnp.inf)
        c_new = knew[:, :KV_LORA].astype(F32)

        def new_values(p):
            pb = p.astype(BF16).astype(F32)
            return sum(pb[:, t:t + 1] * c_new[t:t + 1, :] for t in range(T))

        update(s_new, new_values)
        o_ref[...] = acc_scr[...] / l_scr[...]


def paged_attention(q, knew, ksc_new, cache_ckv, cache_side, page_table, layer, gk_rope):
    B, rows, _ = q.shape
    T = rows // MLA_HEADS
    n_pages = page_table.shape[1]
    n = PAGES_PER_STEP
    half = ROPE_DIM // 2
    inv = jnp.power(ROPE_BASE, -jnp.arange(half, dtype=F32) / half)
    pos = jnp.arange(n_pages * PAGE_SIZE).astype(F32).reshape(n_pages, 1, PAGE_SIZE)
    ang = pos * inv[None, :, None]
    cos3, sin3 = jnp.cos(ang), jnp.sin(ang)

    def page_spec(shape, i):
        return pl.BlockSpec((None, None) + shape, lambda b, j, pt: (layer, pt[b, j * n + i], 0, 0))

    grid_spec = pltpu.PrefetchScalarGridSpec(
        num_scalar_prefetch=1,
        grid=(B, n_pages // n),
        in_specs=[pl.BlockSpec((None, rows, MLA_QCAT), lambda b, j, pt: (b, 0, 0)),
                  pl.BlockSpec((None, T, MLA_QCAT), lambda b, j, pt: (b, 0, 0)),
                  pl.BlockSpec((None, T, MLA_HEADS), lambda b, j, pt: (b, 0, 0)),
                  pl.BlockSpec((n_pages, half, PAGE_SIZE), lambda b, j, pt: (0, 0, 0)),
                  pl.BlockSpec((n_pages, half, PAGE_SIZE), lambda b, j, pt: (0, 0, 0)),
                  pl.BlockSpec((ROPE_DIM, 1), lambda b, j, pt: (0, 0))]
        + [page_spec((PAGE_SIZE, KV_LORA), i) for i in range(n)]
        + [page_spec((ROPE_DIM + MLA_HEADS, PAGE_SIZE), i) for i in range(n)],
        out_specs=pl.BlockSpec((None, rows, KV_LORA), lambda b, j, pt: (b, 0, 0)),
        scratch_shapes=[pltpu.VMEM((rows, 1), F32), pltpu.VMEM((rows, 1), F32), pltpu.VMEM((rows, KV_LORA), F32)],
    )
    return pl.pallas_call(
        _paged_attn_kernel,
        grid_spec=grid_spec,
        out_shape=jax.ShapeDtypeStruct((B, rows, KV_LORA), F32),
        compiler_params=_cparams(("parallel", "arbitrary")),
        name="paged_attention",
    )(page_table, q, knew, ksc_new, cos3, sin3, gk_rope[:, None],
      *([cache_ckv] * n), *([cache_side] * n))


def _vproj_kernel(o_ref, w_ref, out_ref):
    out_ref[...] = jnp.concatenate(
        [jnp.dot(o_ref[:, h * KV_LORA:(h + 1) * KV_LORA].astype(BF16), w_ref[h], preferred_element_type=F32)
         for h in range(MLA_HEADS)], axis=-1)


def mla_value_proj(o_lat, wvb):
    T = o_lat.shape[0]
    w = wvb.transpose(1, 0, 2).astype(BF16)
    return pl.pallas_call(
        _vproj_kernel,
        grid=(1,),
        in_specs=[_full_spec((T, MLA_HEADS * KV_LORA)), _full_spec((MLA_HEADS, KV_LORA, V_DIM))],
        out_specs=_full_spec((T, MLA_HEADS * V_DIM)),
        out_shape=jax.ShapeDtypeStruct((T, MLA_HEADS * V_DIM), F32),
        compiler_params=_cparams(("arbitrary",)),
        name="mla_value_proj",
    )(o_lat, w)


SSM_BC = SSM_GROUPS * SSM_STATE


def _softplus(x):
    return jnp.maximum(x, 0.0) + jnp.log(1.0 + jnp.exp(-jnp.abs(x)))


def _silu(x):
    return x * jax.nn.sigmoid(x)


def _gated_norm(y, z, g):
    y = y * _silu(z)
    return y * lax.rsqrt(jnp.mean(y * y, axis=-1, keepdims=True) + EPS) * g


def _ssd_prompt_kernel(c_ref, buf_ref, s0_ref, cw_ref, cb_ref, dtb_ref, a_ref, d_ref, ng_ref,
                       o_ref, conv_ref, sT_ref, carry_scr, s_scr):
    ci = pl.program_id(1)
    Lc = c_ref.shape[0]

    @pl.when(ci == 0)
    def _():
        carry_scr[...] = buf_ref[...]
        s_scr[...] = s0_ref[...]

    cols = c_ref[...]
    z = cols[:, :SSM_W]
    xbc = cols[:, SSM_W:SSM_W + CONV_CH]
    dtr = cols[:, SSM_W + CONV_CH:SSM_W + CONV_CH + LANE]
    ext = jnp.concatenate([carry_scr[...], xbc], axis=0)
    conv = cb_ref[...]
    for i in range(CONV_W):
        lo = 8 - (CONV_W - 1) + i
        conv = conv + ext[lo:lo + Lc, :] * cw_ref[i:i + 1, :]
    carry_scr[...] = xbc[Lc - 8:, :]
    conv = _silu(conv)
    xs, Bm, Cm = conv[:, :SSM_W], conv[:, SSM_W:SSM_W + SSM_BC], conv[:, SSM_W + SSM_BC:]
    dt = _softplus(dtr + dtb_ref[...])
    a = dt * a_ref[...]
    ri = lax.broadcasted_iota(jnp.int32, (Lc, Lc), 0)
    cj = lax.broadcasted_iota(jnp.int32, (Lc, Lc), 1)
    causal = cj <= ri
    acs = jnp.dot(causal.astype(F32), a, precision=lax.Precision.HIGHEST, preferred_element_type=F32)
    acs_t = acs.T
    nt = (((1,), (1,)), ((), ()))
    ys = []
    for g in range(SSM_GROUPS):
        Bg = Bm[:, g * SSM_STATE:(g + 1) * SSM_STATE].astype(BF16)
        Cg = Cm[:, g * SSM_STATE:(g + 1) * SSM_STATE].astype(BF16)
        cb = lax.dot_general(Cg, Bg, nt, preferred_element_type=F32)
        for e in range(SSM_EPG):
            hd = g * SSM_EPG + e
            col = acs[:, hd:hd + 1]
            lmat = jnp.exp(jnp.where(causal, col - acs_t[hd:hd + 1, :], -jnp.inf))
            x_e = xs[:, hd * SSM_HD:(hd + 1) * SSM_HD]
            xdt = x_e * dt[:, hd:hd + 1]
            y = jnp.dot((cb * lmat).astype(BF16), xdt.astype(BF16), preferred_element_type=F32)
            last = acs[Lc - 1:Lc, hd:hd + 1]
            s_in = s_scr[hd]
            y = y + lax.dot_general(Cg, s_in.astype(BF16), nt, preferred_element_type=F32) * jnp.exp(col)
            st = lax.dot_general((xdt * jnp.exp(last - col)).astype(BF16), Bg, (((0,), (0,)), ((), ())),
                                 preferred_element_type=F32)
            s_scr[hd] = s_in * jnp.exp(last) + st
            ys.append(y + x_e * d_ref[:, hd:hd + 1])
    o_ref[...] = _gated_norm(jnp.concatenate(ys, axis=-1), z, ng_ref[...])

    @pl.when(ci == pl.num_programs(1) - 1)
    def _():
        conv_ref[...] = xbc[Lc - (CONV_W - 1):, :]
        sT_ref[...] = s_scr[...]


def _ssm_params(lp):
    pad = lambda v: jnp.pad(v, (0, LANE - SSM_HEADS))[None, :]
    return dict(cw=lp['ssm_conv_w'], cb=lp['ssm_conv_b'][None, :], dtb=pad(lp['ssm_dt_bias']),
                a=pad(-jnp.exp(lp['ssm_a_log'])), d=pad(lp['ssm_d']), ng=lp['ssm_norm_g'][None, :])


def ssd_prompt(ssm_cols, conv_buf, s0, lp, B, seq):
    Lc = SSD_CHUNK
    nc = seq // Lc
    p = _ssm_params(lp)
    buf8 = jnp.pad(conv_buf, ((0, 0), (8 - (CONV_W - 1), 0), (0, 0)))
    st_spec = pl.BlockSpec((None, SSM_HEADS, SSM_HD, SSM_STATE), lambda b, c: (b, 0, 0, 0))
    return pl.pallas_call(
        _ssd_prompt_kernel,
        grid=(B, nc),
        in_specs=[pl.BlockSpec((Lc, SSM_PAD), lambda b, c: (b * nc + c, 0)),
                  pl.BlockSpec((None, 8, CONV_CH), lambda b, c: (b, 0, 0)), st_spec,
                  _full_spec((CONV_W, CONV_CH)), _full_spec((1, CONV_CH)), _full_spec((1, LANE)),
                  _full_spec((1, LANE)), _full_spec((1, LANE)), _full_spec((1, SSM_W))],
        out_specs=[pl.BlockSpec((Lc, SSM_W), lambda b, c: (b * nc + c, 0)),
                   pl.BlockSpec((None, CONV_W - 1, CONV_CH), lambda b, c: (b, 0, 0)), st_spec],
        out_shape=[jax.ShapeDtypeStruct((B * seq, SSM_W), F32), jax.ShapeDtypeStruct((B, CONV_W - 1, CONV_CH), F32),
                   jax.ShapeDtypeStruct((B, SSM_HEADS, SSM_HD, SSM_STATE), F32)],
        scratch_shapes=[pltpu.VMEM((8, CONV_CH), F32), pltpu.VMEM((SSM_HEADS, SSM_HD, SSM_STATE), F32)],
        compiler_params=_cparams(("parallel", "arbitrary")),
        name="ssd_prompt",
    )(ssm_cols, buf8, s0, p['cw'], p['cb'], p['dtb'], p['a'], p['d'], p['ng'])


def _ssd_step_prep_kernel(c_ref, x0_ref, x1_ref, x2_ref, cw_ref, cb_ref, dtb_ref, a_ref, exp_ref, expg_ref,
                          w_ref, k_ref, r_ref, v_ref, xs_ref):
    cols = c_ref[...]
    xbc = cols[:, SSM_W:SSM_W + CONV_CH]
    dtr = cols[:, SSM_W + CONV_CH:SSM_W + CONV_CH + LANE]
    conv = cb_ref[...] + x0_ref[...] * cw_ref[0:1, :] + x1_ref[...] * cw_ref[1:2, :] + x2_ref[...] * cw_ref[2:3, :] \
        + xbc * cw_ref[3:4, :]
    conv = _silu(conv)
    xs, Bm, Cm = conv[:, :SSM_W], conv[:, SSM_W:SSM_W + SSM_BC], conv[:, SSM_W + SSM_BC:]
    dt = _softplus(dtr + dtb_ref[...])
    hi = lax.Precision.HIGHEST
    dt_h = jnp.dot(dt, exp_ref[...], precision=hi, preferred_element_type=F32)
    a_h = jnp.dot(dt * a_ref[...], exp_ref[...], precision=hi, preferred_element_type=F32)
    w_ref[...] = jnp.exp(a_h)
    k_ref[...] = jnp.dot(Bm, expg_ref[...], precision=hi, preferred_element_type=F32)
    r_ref[...] = jnp.dot(Cm, expg_ref[...], precision=hi, preferred_element_type=F32)
    v_ref[...] = xs * dt_h
    xs_ref[...] = xs


def _ssd_step_post_kernel(y_ref, xs_ref, c_ref, dh_ref, ng_ref, o_ref):
    y = y_ref[...] + xs_ref[...] * dh_ref[...]
    o_ref[...] = _gated_norm(y, c_ref[:, :SSM_W], ng_ref[...])


def ssd_decode(ssm_cols, conv_buf, s0, lp, B, seq, tm):
    T = B * seq
    p = _ssm_params(lp)
    xbc = ssm_cols[:, SSM_W:SSM_W + CONV_CH].reshape(B, seq, CONV_CH)
    xpad = jnp.concatenate([conv_buf, xbc], axis=1)
    shifted = [xpad[:, i:i + seq].reshape(T, CONV_CH) for i in range(CONV_W - 1)]
    head_of_lane = jnp.arange(SSM_W) // SSM_HD
    expand = (jnp.arange(LANE)[:, None] == head_of_lane[None, :]).astype(F32)
    src = (head_of_lane // SSM_EPG) * SSM_STATE + jnp.arange(SSM_W) % SSM_STATE
    expand_g = (jnp.arange(SSM_BC)[:, None] == src[None, :]).astype(F32)
    out = jax.ShapeDtypeStruct((T, SSM_W), F32)
    w, k, r, v, xs = pl.pallas_call(
        _ssd_step_prep_kernel,
        grid=(T // tm,),
        in_specs=[_tok_spec(tm, SSM_PAD)] + [_tok_spec(tm, CONV_CH)] * 3
        + [_full_spec((CONV_W, CONV_CH)), _full_spec((1, CONV_CH)), _full_spec((1, LANE)), _full_spec((1, LANE)),
           _full_spec((LANE, SSM_W)), _full_spec((SSM_BC, SSM_W))],
        out_specs=[_tok_spec(tm, SSM_W)] * 5,
        out_shape=[out] * 5,
        compiler_params=_cparams(("parallel",)),
        name="ssd_step_prep",
    )(ssm_cols, *shifted, p['cw'], p['cb'], p['dtb'], p['a'], expand, expand_g)
    sh = lambda t: t.reshape(B, seq, SSM_W)
    zeros = jnp.zeros((B, seq, SSM_W), F32)
    y, sT = state_scan(sh(w), zeros, zeros, sh(k), sh(r), sh(v), s0, seq)
    d_h = jnp.repeat(lp['ssm_d'], SSM_HD)[None, :]
    o = pl.pallas_call(
        _ssd_step_post_kernel,
        grid=(T // tm,),
        in_specs=[_tok_spec(tm, SSM_W), _tok_spec(tm, SSM_W), _tok_spec(tm, SSM_PAD), _full_spec((1, SSM_W)),
                  _full_spec((1, SSM_W))],
        out_specs=_tok_spec(tm, SSM_W),
        out_shape=out,
        compiler_params=_cparams(("parallel",)),
        name="ssd_step_post",
    )(y.reshape(T, SSM_W), xs, ssm_cols, d_h, p['ng'])
    return o, xpad[:, -(CONV_W - 1):], sT


def _merge_kernel(oa_ref, ob_ref, oc_ref, gate_ref, x_ref, g1_ref, wb_ref, wo_ref, o_ref):
    D = x_ref.shape[1]
    acc = jnp.zeros(x_ref.shape, F32)
    for i, ref in enumerate((oa_ref, ob_ref, oc_ref)):
        u = jnp.dot(ref[...].astype(BF16), wb_ref[i], preferred_element_type=F32)
        acc = acc + jax.nn.sigmoid(gate_ref[:, i * D:(i + 1) * D]) * u
    y = jnp.dot(acc.astype(BF16), wo_ref[...], preferred_element_type=F32)
    o_ref[...] = x_ref[...] + g1_ref[...] * y


def branch_merge(o_a, o_b, o_c, gates, x2, g1, wb_bf, wo_bf, seq, tm):
    T, D = x2.shape
    return pl.pallas_call(
        _merge_kernel,
        grid=(T // tm,),
        in_specs=[_tok_spec(tm, BR_W)] * 3 + [_tok_spec(tm, N_BRANCH * D), _tok_spec(tm, D), _row_spec(seq, tm, D),
                                              _full_spec((N_BRANCH, BR_W, D)), _full_spec((D, D))],
        out_specs=_tok_spec(tm, D),
        out_shape=jax.ShapeDtypeStruct((T, D), F32),
        compiler_params=_cparams(("parallel",)),
        name="branch_merge",
    )(o_a, o_b, o_c, gates, x2, _row_operand(g1, seq, tm), wb_bf, wo_bf)


TM_TOKENS = 512
TM_INPROJ = 256
TM_ROUTE = 256
SCAN_CHUNK = 32
FLASH_BLOCK = 2048


def _layer(x2, B, seq, c, lw, st, pos, layer, decode_ctx):
    T = B * seq
    tm = min(TM_TOKENS, T)
    sh1, sc1, g1, sh2, sc2, g2 = jnp.split(ada_mod(c, lw['ada_w'], lw['ada_b'], layer), 6, axis=-1)
    lp = lw['lp']
    rw, mla, ssm, gates = in_proj(x2, sc1, sh1, lp['norm1_g'], lw['w_in_pad'], seq, min(TM_INPROJ, T))
    o_a, rw_shift, rw_state = rwkv_branch(rw, B, seq, st[0], st[1], lp, tm, min(SCAN_CHUNK, seq))
    if decode_ctx is None:
        q, k, v, ckv, ksc = mla_prompt_proj(mla, pos, lp, B, seq, min(tm, seq))
        o_b = flash_attention(q, k, v, min(FLASH_BLOCK, seq)).reshape(T, BR_W)
        o_c, conv_buf, ssm_state = ssd_prompt(ssm, st[2], st[3], lp, B, seq)
    else:
        cache_ckv, cache_side, page_table = decode_ctx
        q, knew, ckv, ksc = mla_sample_proj(mla, pos, lp, seq, tm)
        o_lat = paged_attention(q.reshape(B, seq * MLA_HEADS, MLA_QCAT), knew.reshape(B, seq, MLA_QCAT),
                                ksc.reshape(B, seq, MLA_HEADS), cache_ckv, cache_side, page_table,
                                layer, lp['mla_k_g'][NOPE_DIM:])
        o_b = mla_value_proj(o_lat.reshape(T, MLA_HEADS * KV_LORA), lp['mla_wvb'])
        o_c, conv_buf, ssm_state = ssd_decode(ssm, st[2], st[3], lp, B, seq, tm)
    x2 = branch_merge(o_a, o_b, o_c, gates, x2, g1, lw['wb_bf'], lw['wo_bf'], seq, tm)
    h_tiles, x_tiles, eid, gate = peer_route(x2, sc2, sh2, lp['norm2_g'], lw['wq_bf'], lw['keys_bf'], seq,
                                             min(TM_ROUTE, T))
    x2 = peer_gather(eid, h_tiles, gate, x_tiles, g2, seq, lw['uv'])
    kpe = mla[:, Q_LORA + KV_LORA:MLA_COLS]
    return x2, (ckv.reshape(B, seq, KV_LORA), kpe.reshape(B, seq, ROPE_DIM), ksc.reshape(B, seq, MLA_HEADS),
                rw_shift, rw_state, conv_buf, ssm_state)


def kernel(x_prompt, x_sample, cache_ckv, cache_kpe, cache_kscale, state_rwkv_shift, state_rwkv_wkv,
           state_ssm_conv, state_ssm, page_table, c_prompt, c_sample, ada_w, ada_b, norm1_g, norm2_g, w_in,
           rw_mu, rw_w0, rw_w2, rw_a0, rw_a2, rw_g2, rw_kk, rw_ka, rw_rk, rw_ln_w, rw_ln_b,
           mla_qa_g, mla_wqb, mla_kv_g, mla_wkb, mla_wvb, mla_q_g, mla_k_g,
           ssm_conv_w, ssm_conv_b, ssm_dt_bias, ssm_a_log, ssm_d, ssm_norm_g,
           w_branch, w_out, pk_wq, pk_keys, pk_u, pk_v):
    small = dict(norm1_g=norm1_g, norm2_g=norm2_g,
                 rw_mu=rw_mu, rw_w0=rw_w0, rw_w2=rw_w2, rw_a0=rw_a0, rw_a2=rw_a2, rw_g2=rw_g2,
                 rw_kk=rw_kk, rw_ka=rw_ka, rw_rk=rw_rk, rw_ln_w=rw_ln_w, rw_ln_b=rw_ln_b,
                 mla_qa_g=mla_qa_g, mla_wqb=mla_wqb, mla_kv_g=mla_kv_g, mla_wkb=mla_wkb, mla_wvb=mla_wvb,
                 mla_q_g=mla_q_g, mla_k_g=mla_k_g,
                 ssm_conv_w=ssm_conv_w, ssm_conv_b=ssm_conv_b, ssm_dt_bias=ssm_dt_bias, ssm_a_log=ssm_a_log,
                 ssm_d=ssm_d, ssm_norm_g=ssm_norm_g)
    depth = ada_w.shape[0]
    Bp, S, D = x_prompt.shape
    Bd, T, _ = x_sample.shape
    assert SSD_CHUNK % 8 == 0 and S % SSD_CHUNK == 0 and T < SSD_CHUNK
    past_len = page_table.shape[1] * PAGE_SIZE
    pos_p = jnp.arange(S)
    pos_s = past_len + jnp.arange(T)
    zero_state = (jnp.zeros((Bp, RW_COLS), F32), jnp.zeros((Bp, RW_HEADS, RW_HD, RW_HD), F32),
                  jnp.zeros((Bp, CONV_W - 1, CONV_CH), F32), jnp.zeros((Bp, SSM_HEADS, SSM_HD, SSM_STATE), F32))
    hp, hs = x_prompt.reshape(Bp * S, D), x_sample.reshape(Bd * T, D)
    cache_side = jnp.concatenate([jnp.swapaxes(cache_kpe, -1, -2), jnp.swapaxes(cache_kscale, -1, -2)], axis=-2)
    decode_ctx = (cache_ckv, cache_side, page_table)
    new_p, new_s = [], []
    for l in range(depth):
        lw = dict(lp={name: arr[l] for name, arr in small.items()}, ada_w=ada_w, ada_b=ada_b,
                  w_in_pad=_pad_w_in(w_in[l]), wb_bf=w_branch[l].astype(BF16), wo_bf=w_out[l].astype(BF16),
                  wq_bf=pk_wq[l].astype(BF16),
                  keys_bf=pk_keys[l].reshape(2 * PK_HEADS, PK_NKEYS, PK_DHALF).astype(BF16),
                  uv=peer_table(pk_u[l], pk_v[l]))
        hp, st_p = _layer(hp, Bp, S, c_prompt, lw, zero_state, pos_p, l, None)
        st_in = (state_rwkv_shift[l], state_rwkv_wkv[l], state_ssm_conv[l], state_ssm[l])
        hs, st_s = _layer(hs, Bd, T, c_sample, lw, st_in, pos_s, l, decode_ctx)
        new_p.append(st_p)
        new_s.append(st_s)
    hp, hs = hp.reshape(Bp, S, D), hs.reshape(Bd, T, D)
    return (hp, hs,
            jnp.stack([s[0] for s in new_p]), jnp.stack([s[0] for s in new_s]),
            jnp.stack([s[1] for s in new_p]), jnp.stack([s[1] for s in new_s]),
            jnp.stack([s[2] for s in new_p]), jnp.stack([s[2] for s in new_s]),
            jnp.stack([s[3] for s in new_p]), jnp.stack([s[3] for s in new_s]),
            jnp.stack([s[4] for s in new_p]), jnp.stack([s[4] for s in new_s]),
            jnp.stack([s[5] for s in new_p]), jnp.stack([s[5] for s in new_s]),
            jnp.stack([s[6] for s in new_p]), jnp.stack([s[6] for s in new_s]))
```
